```python
import math
import jax, jax.numpy as jnp
from jax import lax
import numpy as np

D_MODEL = 1024
BATCH = 2
SEQ = 8192
DEPTH = 1

PLE_DIM = 256
DIFF_HEADS = 4
DIFF_HEAD_DIM = 64
DIFF_WIDTH = DIFF_HEADS * 2 * DIFF_HEAD_DIM
MOBA_HEADS = 8
MOBA_HEAD_DIM = 64
MOBA_WIDTH = MOBA_HEADS * MOBA_HEAD_DIM
MOBA_BLOCK = 256
MOBA_TOPK = 3
ROT_DIM = 16
ROPE_THETA = 500000.0
Q_CHUNK = 128
N_BRANCHES = 2
IN_COLS = 4 * DIFF_WIDTH + 4 * MOBA_WIDTH + N_BRANCHES * D_MODEL
EPS = 1e-6
SUBLN_EPS = 1e-5

kernel_name = "hybrid_diffattn_moba_gated_block"


def rms_norm(x, g, eps=EPS):
    xf = x.astype(jnp.float32)
    y = xf * lax.rsqrt(jnp.mean(xf * xf, axis=-1, keepdims=True) + eps)
    return (y * g.astype(jnp.float32)).astype(x.dtype)


def rope_tables(seq):
    inv = ROPE_THETA ** (-jnp.arange(0, ROT_DIM, 2, dtype=jnp.float32) / ROT_DIM)
    ang = jnp.arange(seq, dtype=jnp.float32)[:, None] * inv[None, :]
    return jnp.cos(ang), jnp.sin(ang)


def partial_rope(x, cos, sin):
    half = ROT_DIM // 2
    shape = (1, cos.shape[0]) + (1,) * (x.ndim - 3) + (half,)
    c = cos.reshape(shape).astype(x.dtype)
    s = sin.reshape(shape).astype(x.dtype)
    x1 = x[..., :half]
    x2 = x[..., half:ROT_DIM]
    rot = jnp.concatenate([x1 * c - x2 * s, x2 * c + x1 * s], axis=-1)
    return jnp.concatenate([rot, x[..., ROT_DIM:]], axis=-1)


def split_in_proj(proj):
    sizes = (DIFF_WIDTH,) * 4 + (MOBA_WIDTH,) * 4 + (N_BRANCHES * D_MODEL,)
    idx = []
    acc = 0
    for s in sizes[:-1]:
        acc += s
        idx.append(acc)
    return jnp.split(proj, idx, axis=-1)


def diff_attention(q, k, v, lam, lam_init, subln_g):
    B, S, H, _, dh = q.shape
    n_chunks = S // Q_CHUNK
    scale = dh ** -0.5
    key_pos = jnp.arange(S)
    qc = q.reshape(B, n_chunks, Q_CHUNK, H, 2, dh).transpose(1, 0, 2, 3, 4, 5)

    def one_chunk(args):
        c, q_blk = args
        s = jnp.einsum('bqhmd,bkhmd->bhmqk', q_blk, k).astype(jnp.float32) * scale
        q_pos = c * Q_CHUNK + jnp.arange(Q_CHUNK)
        s = jnp.where(key_pos[None, :] <= q_pos[:, None], s, -jnp.inf)
        a = jax.nn.softmax(s, axis=-1)
        w = a[:, :, 0] - lam * a[:, :, 1]
        return jnp.einsum('bhqk,bkhe->bqhe', w.astype(v.dtype), v)

    o = lax.map(one_chunk, (jnp.arange(n_chunks), qc))
    o = o.transpose(1, 0, 2, 3, 4).reshape(B, S, H, 2 * dh)
    o = rms_norm(o, subln_g, SUBLN_EPS) * (1.0 - lam_init)
    return o.reshape(B, S, H * 2 * dh)


def moba_attention(q, k, v):
    B, S, H, dh = q.shape
    nb = -(-S // MOBA_BLOCK)
    pad = nb * MOBA_BLOCK - S
    kp = jnp.pad(k, ((0, 0), (0, pad), (0, 0), (0, 0)))
    vp = jnp.pad(v, ((0, 0), (0, pad), (0, 0), (0, 0)))
    k_blocks = kp.reshape(B, nb, MOBA_BLOCK, H, dh).transpose(0, 3, 1, 2, 4)
    v_blocks = vp.reshape(B, nb, MOBA_BLOCK, H, dh).transpose(0, 3, 1, 2, 4)
    k_mean = jnp.mean(k_blocks.astype(jnp.float32), axis=3).astype(k.dtype)
    topk = min(MOBA_TOPK, nb)
    scale = dh ** -0.5
    n_chunks = S // Q_CHUNK
    qc = q.reshape(B, n_chunks, Q_CHUNK, H, dh).transpose(1, 0, 3, 2, 4)
    b_idx = jnp.arange(B)[:, None, None, None]
    h_idx = jnp.arange(H)[None, :, None, None]
    blk_ids = jnp.arange(nb)

    def one_chunk(args):
        c, q_blk = args
        q_start = c * Q_CHUNK
        own = q_start // MOBA_BLOCK
        q_pos = q_start + jnp.arange(Q_CHUNK)
        gate = jnp.einsum('bhqd,bhnd->bhqn', q_blk, k_mean).astype(jnp.float32)
        gate = jnp.where(blk_ids < own, gate, -jnp.inf)
        _, sel = lax.top_k(gate, topk)
        sel_valid = jnp.arange(topk) < own
        k_sel = k_blocks[b_idx, h_idx, sel]
        v_sel = v_blocks[b_idx, h_idx, sel]
        s_sel = jnp.einsum('bhqd,bhqrkd->bhqrk', q_blk, k_sel).astype(jnp.float32) * scale
        s_sel = jnp.where(sel_valid[:, None], s_sel, -jnp.inf).reshape(B, H, Q_CHUNK, topk * MOBA_BLOCK)
        k_own = lax.dynamic_index_in_dim(k_blocks, own, axis=2, keepdims=False)
        v_own = lax.dynamic_index_in_dim(v_blocks, own, axis=2, keepdims=False)
        s_own = jnp.einsum('bhqd,bhkd->bhqk', q_blk, k_own).astype(jnp.float32) * scale
        own_pos = own * MOBA_BLOCK + jnp.arange(MOBA_BLOCK)
        s_own = jnp.where(own_pos[None, :] <= q_pos[:, None], s_own, -jnp.inf)
        pr = jax.nn.softmax(jnp.concatenate([s_sel, s_own], axis=-1), axis=-1).astype(v.dtype)
        p_sel = pr[..., :topk * MOBA_BLOCK].reshape(B, H, Q_CHUNK, topk, MOBA_BLOCK)
        p_own = pr[..., topk * MOBA_BLOCK:]
        return (jnp.einsum('bhqrk,bhqrkd->bhqd', p_sel, v_sel)
                + jnp.einsum('bhqk,bhkd->bhqd', p_own, v_own))

    o = lax.map(one_chunk, (jnp.arange(n_chunks), qc))
    return o.transpose(1, 0, 3, 2, 4).reshape(B, S, H * dh)


def setup_inputs(seed: int = 0) -> dict:
    key = jax.random.key(seed)
    ks = jax.random.split(key, 16)
    f32 = jnp.float32
    nrm = lambda k, shape, sc: jax.random.normal(k, shape, f32) * sc
    return {
        'x': nrm(ks[0], (BATCH, SEQ, D_MODEL), 1.0),
        'p': nrm(ks[1], (DEPTH, BATCH, SEQ, PLE_DIM), 1.0),
        'norm_g': 1.0 + nrm(ks[2], (DEPTH, D_MODEL), 0.02),
        'w_in': nrm(ks[3], (DEPTH, D_MODEL, IN_COLS), D_MODEL ** -0.5),
        'lambda_q1': nrm(ks[4], (DEPTH, DIFF_HEAD_DIM), 0.1),
        'lambda_k1': nrm(ks[5], (DEPTH, DIFF_HEAD_DIM), 0.1),
        'lambda_q2': nrm(ks[6], (DEPTH, DIFF_HEAD_DIM), 0.1),
        'lambda_k2': nrm(ks[7], (DEPTH, DIFF_HEAD_DIM), 0.1),
        'subln_g': 1.0 + nrm(ks[8], (DEPTH, 2 * DIFF_HEAD_DIM), 0.02),
        'w_branch_diff': nrm(ks[9], (DEPTH, DIFF_WIDTH, D_MODEL), DIFF_WIDTH ** -0.5),
        'w_branch_moba': nrm(ks[10], (DEPTH, MOBA_WIDTH, D_MODEL), MOBA_WIDTH ** -0.5),
        'w_out': nrm(ks[11], (DEPTH, D_MODEL, D_MODEL), D_MODEL ** -0.5),
        'w_ple': nrm(ks[12], (DEPTH, PLE_DIM, D_MODEL), PLE_DIM ** -0.5),
        'w_ple_gate': nrm(ks[13], (DEPTH, D_MODEL, D_MODEL), D_MODEL ** -0.5),
        'final_g': 1.0 + nrm(ks[14], (D_MODEL,), 0.02),
    }


def reference(x, p, norm_g, w_in, lambda_q1, lambda_k1, lambda_q2, lambda_k2, subln_g,
              w_branch_diff, w_branch_moba, w_out, w_ple, w_ple_gate, final_g):
    B, S, _ = x.shape
    cos, sin = rope_tables(S)
    for i in range(DEPTH):
        lam_init = 0.8 - 0.6 * math.exp(-0.3 * i)
        h = rms_norm(x, norm_g[i])
        proj = h @ w_in[i]
        dq, dk, dv, dg, mq, mk, mv, mg, gates = split_in_proj(proj)
        dq = partial_rope(dq.reshape(B, S, DIFF_HEADS, 2, DIFF_HEAD_DIM), cos, sin)
        dk = partial_rope(dk.reshape(B, S, DIFF_HEADS, 2, DIFF_HEAD_DIM), cos, sin)
        dv = dv.reshape(B, S, DIFF_HEADS, 2 * DIFF_HEAD_DIM)
        lam = (jnp.exp(jnp.sum(lambda_q1[i].astype(jnp.float32) * lambda_k1[i].astype(jnp.float32)))
               - jnp.exp(jnp.sum(lambda_q2[i].astype(jnp.float32) * lambda_k2[i].astype(jnp.float32)))
               + lam_init)
        o_a = diff_attention(dq, dk, dv, lam, lam_init, subln_g[i]) * jax.nn.silu(dg)
        y_a = o_a @ w_branch_diff[i]
        mq = partial_rope(mq.reshape(B, S, MOBA_HEADS, MOBA_HEAD_DIM), cos, sin)
        mk = partial_rope(mk.reshape(B, S, MOBA_HEADS, MOBA_HEAD_DIM), cos, sin)
        mv = mv.reshape(B, S, MOBA_HEADS, MOBA_HEAD_DIM)
        o_b = moba_attention(mq, mk, mv) * jax.nn.silu(mg)
        y_b = o_b @ w_branch_moba[i]
        g_a, g_b = jnp.split(gates, N_BRANCHES, axis=-1)
        merged = jax.nn.sigmoid(g_a) * y_a + jax.nn.sigmoid(g_b) * y_b
        x = x + merged @ w_out[i]
        x = x + jax.nn.sigmoid(x @ w_ple_gate[i]) * (p[i] @ w_ple[i])
    return rms_norm(x, final_g)
```

```python
import functools
import math

import jax
import jax.numpy as jnp
from jax import lax
from jax.experimental import pallas as pl
from jax.experimental.pallas import tpu as pltpu

D_MODEL = 1024
PLE_DIM = 256
DIFF_HEADS = 4
DIFF_HEAD_DIM = 64
DIFF_WIDTH = DIFF_HEADS * 2 * DIFF_HEAD_DIM
MOBA_HEADS = 8
MOBA_HEAD_DIM = 64
MOBA_WIDTH = MOBA_HEADS * MOBA_HEAD_DIM
MOBA_BLOCK = 256
MOBA_TOPK = 3
ROT_DIM = 16
ROPE_THETA = 500000.0
EPS = 1e-6
SUBLN_EPS = 1e-5
LAM_INIT = 0.8 - 0.6 * math.exp(-0.3 * 0)

LANES = 128
CHUNK = 512
N_CHUNKS = 12
ROW_TILE = 512
ATT_TILE = MOBA_BLOCK
VMEM_LIMIT = 56 * 1024 * 1024

_F32 = jnp.float32
_BF16 = jnp.bfloat16
_NT = (((1,), (1,)), ((), ()))


def _rope_rows(a, cosv, sin_lo, sin_hi):
    outs = []
    for j in range(CHUNK // LANES):
        xs = a[:, LANES * j:LANES * (j + 1)]
        up = pltpu.roll(xs, LANES - ROT_DIM // 2, 1)
        dn = pltpu.roll(xs, ROT_DIM // 2, 1)
        outs.append(xs * cosv + up * sin_lo + dn * sin_hi)
    return jnp.concatenate(outs, axis=1)


def _proj_kernel(x_ref, g_ref, w_ref, cos_ref, slo_ref, shi_ref,
                 dq_ref, dk_ref, dvt_ref, dg_ref, mq_ref, mk_ref, mvt_ref, mg_ref,
                 km_ref, gates_ref):
    x = x_ref[...]
    ms = jnp.mean(x * x, axis=-1, keepdims=True)
    h = (x * lax.rsqrt(ms + EPS) * g_ref[...]).astype(_BF16)
    cosv, sin_lo, sin_hi = cos_ref[...], slo_ref[...], shi_ref[...]

    def proj(c):
        return jnp.dot(h, w_ref[:, CHUNK * c:CHUNK * (c + 1)], preferred_element_type=_F32)

    def store_transposed(ref, a):
        at = a.T.astype(_BF16)
        for j in range(ROW_TILE // ATT_TILE):
            ref[0, j] = at[:, ATT_TILE * j:ATT_TILE * (j + 1)]

    dq_ref[...] = (_rope_rows(proj(0), cosv, sin_lo, sin_hi) * (DIFF_HEAD_DIM ** -0.5)).astype(_BF16)
    dk_ref[...] = _rope_rows(proj(1), cosv, sin_lo, sin_hi).astype(_BF16)
    store_transposed(dvt_ref, proj(2))
    dg_ref[...] = jax.nn.silu(proj(3))
    mq_ref[...] = _rope_rows(proj(4), cosv, sin_lo, sin_hi)
    mk = _rope_rows(proj(5), cosv, sin_lo, sin_hi)
    mk_ref[...] = mk.astype(_BF16)
    for j in range(ROW_TILE // MOBA_BLOCK):
        km_ref[0, j:j + 1, :] = jnp.mean(mk[MOBA_BLOCK * j:MOBA_BLOCK * (j + 1)], axis=0, keepdims=True)
    store_transposed(mvt_ref, proj(6))
    mg_ref[...] = jax.nn.silu(proj(7))
    for c in range(8, N_CHUNKS):
        gates_ref[:, CHUNK * (c - 8):CHUNK * (c - 7)] = jax.nn.sigmoid(proj(c))


def _flash_tile(s, m, l, acc_ref, vt, first):
    m_new = jnp.max(s, axis=0, keepdims=True)
    if not first:
        m_new = jnp.maximum(m, m_new)
    p = jnp.exp(s - m_new)
    pv = jnp.dot(vt, p.astype(_BF16), preferred_element_type=_F32)
    psum = jnp.sum(p, axis=0, keepdims=True)
    if first:
        acc_ref[...] = pv
        return m_new, psum
    alpha = jnp.exp(m - m_new)
    acc_ref[...] = alpha * acc_ref[...] + pv
    return m_new, alpha * l + psum


def _causal_mask(s):
    kpos = lax.broadcasted_iota(jnp.int32, s.shape, 0)
    qpos = lax.broadcasted_iota(jnp.int32, s.shape, 1)
    return jnp.where(kpos <= qpos, s, -jnp.inf)


def _split_heads(q):
    lane = lax.broadcasted_iota(jnp.int32, q.shape, 1)
    zero = jnp.zeros_like(q)
    return jnp.where(lane < 64, q, zero), jnp.where(lane >= 64, q, zero)


def _diff_kernel(lq1_ref, lk1_ref, lq2_ref, lk2_ref, q_ref, k_ref, vt_ref, dg_ref, sg_ref,
                 o_ref, acc1_ref, acc2_ref):
    qi = pl.program_id(2)
    q1, q2 = _split_heads(q_ref[...])

    def tile(kv, carry, first):
        m1, l1, m2, l2 = carry
        k = k_ref[pl.ds(pl.multiple_of(kv * ATT_TILE, ATT_TILE), ATT_TILE), :]
        vt = vt_ref[0, kv]
        s1 = lax.dot_general(k, q1, _NT, preferred_element_type=_F32)
        s2 = lax.dot_general(k, q2, _NT, preferred_element_type=_F32)
        if first:
            s1, s2 = _causal_mask(s1), _causal_mask(s2)
        m1, l1 = _flash_tile(s1, m1, l1, acc1_ref, vt, first)
        m2, l2 = _flash_tile(s2, m2, l2, acc2_ref, vt, first)
        return m1, l1, m2, l2

    z = jnp.zeros((1, ATT_TILE), _F32)
    carry = tile(qi, (z, z, z, z), True)
    _, l1, _, l2 = lax.fori_loop(0, qi, lambda kv, c: tile(kv, c, False), carry)

    lam = (jnp.exp(jnp.sum(lq1_ref[...] * lk1_ref[...], axis=-1, keepdims=True))
           - jnp.exp(jnp.sum(lq2_ref[...] * lk2_ref[...], axis=-1, keepdims=True)) + LAM_INIT)
    ot = acc1_ref[...] / l1 - lam * (acc2_ref[...] / l2)
    ms = jnp.mean(ot * ot, axis=0, keepdims=True)
    ot = ot * lax.rsqrt(ms + SUBLN_EPS) * sg_ref[...] * (1.0 - LAM_INIT)
    o_ref[...] = (ot.T * dg_ref[...]).astype(_BF16)


def _moba_kernel(q_ref, k_ref, vt_ref, km_ref, mg_ref, o_ref, acca_ref, accb_ref, bias_ref):
    qi = pl.program_id(2)
    q = q_ref[...]
    qa, qb = _split_heads((q * (MOBA_HEAD_DIM ** -0.5)).astype(_BF16))
    nb = km_ref.shape[0]

    blk = lax.broadcasted_iota(jnp.int32, (nb, ATT_TILE), 0)
    for hh, kmh in enumerate(_split_heads(km_ref[...])):
        g = lax.dot_general(kmh, q, _NT, preferred_element_type=_F32, precision=lax.Precision.HIGHEST)
        g = jnp.where(blk < qi, g, -jnp.inf)
        sel = jnp.zeros(g.shape, jnp.bool_)
        for _ in range(MOBA_TOPK):
            mx = jnp.max(g, axis=0, keepdims=True)
            first_idx = jnp.min(jnp.where(g == mx, blk, nb), axis=0, keepdims=True)
            pick = (blk == first_idx) & (mx > -jnp.inf)
            sel = sel | pick
            g = jnp.where(pick, -jnp.inf, g)
        bias_ref[hh] = jnp.where(sel, 0.0, -jnp.inf)

    def tile(kv, carry, first):
        ma, la, mb, lb = carry
        k = k_ref[pl.ds(pl.multiple_of(kv * ATT_TILE, ATT_TILE), ATT_TILE), :]
        out = []
        for hh, (qh, m, l, acc_ref) in enumerate(((qa, ma, la, acca_ref), (qb, mb, lb, accb_ref))):
            s = lax.dot_general(k, qh, _NT, preferred_element_type=_F32)
            vth = vt_ref[0, kv, MOBA_HEAD_DIM * hh:MOBA_HEAD_DIM * (hh + 1), :]
            if first:
                out.extend(_flash_tile(_causal_mask(s), m, l, acc_ref, vth, True))
            else:
                out.extend(_flash_tile(s + bias_ref[hh, pl.ds(kv, 1), :], m, l, acc_ref, vth, False))
        return tuple(out)

    z = jnp.zeros((1, ATT_TILE), _F32)
    carry = tile(qi, (z, z, z, z), True)
    _, la, _, lb = lax.fori_loop(0, qi, lambda kv, c: tile(kv, c, False), carry)
    ot = jnp.concatenate([acca_ref[...] / la, accb_ref[...] / lb], axis=0)
    o_ref[...] = (ot.T * mg_ref[...]).astype(_BF16)


def _out_kernel(oa_ref, ob_ref, gates_ref, x_ref, p_ref, wbd_ref, wbm_ref, wout_ref, wpg_ref,
                wple_ref, fg_ref, o_ref):
    ya = jnp.dot(oa_ref[...], wbd_ref[...], preferred_element_type=_F32)
    yb = jnp.dot(ob_ref[...], wbm_ref[...], preferred_element_type=_F32)
    merged = gates_ref[:, :D_MODEL] * ya + gates_ref[:, D_MODEL:] * yb
    x1 = x_ref[...] + jnp.dot(merged.astype(_BF16), wout_ref[...], preferred_element_type=_F32)
    t = jnp.dot(x1.astype(_BF16), wpg_ref[...], preferred_element_type=_F32)
    pe = jnp.dot(p_ref[...].astype(_BF16), wple_ref[...], preferred_element_type=_F32)
    x2 = x1 + jax.nn.sigmoid(t) * pe
    ms = jnp.mean(x2 * x2, axis=-1, keepdims=True)
    o_ref[...] = x2 * lax.rsqrt(ms + EPS) * fg_ref[...]


def _rope_lane_tables(seq):
    half = ROT_DIM // 2
    inv = ROPE_THETA ** (-jnp.arange(0, ROT_DIM, 2, dtype=_F32) / ROT_DIM)
    ang = jnp.arange(seq, dtype=_F32)[:, None] * inv[None, :]
    cos, sin = jnp.cos(ang), jnp.sin(ang)
    pad = MOBA_HEAD_DIM - ROT_DIM
    one, zero = jnp.ones((seq, pad), _F32), jnp.zeros((seq, pad), _F32)
    zh = jnp.zeros((seq, half), _F32)
    cos64 = jnp.concatenate([cos, cos, one], axis=1)
    slo64 = jnp.concatenate([-sin, zh, zero], axis=1)
    shi64 = jnp.concatenate([zh, sin, zero], axis=1)
    return tuple(jnp.tile(t, (1, LANES // MOBA_HEAD_DIM)) for t in (cos64, slo64, shi64))


def _resident(shape):
    return pl.BlockSpec(shape, lambda *_: (0,) * len(shape), pipeline_mode=pl.Buffered(1))


def _params(n_axes):
    return pltpu.CompilerParams(dimension_semantics=("arbitrary",) * n_axes,
                                vmem_limit_bytes=VMEM_LIMIT)


def kernel(x, p, norm_g, w_in, lambda_q1, lambda_k1, lambda_q2, lambda_k2, subln_g,
           w_branch_diff, w_branch_moba, w_out, w_ple, w_ple_gate, final_g):
    B, S, _ = x.shape
    assert w_in.shape == (1, D_MODEL, N_CHUNKS * CHUNK) and S % ROW_TILE == 0
    rows = B * S
    n_row_tiles = rows // ROW_TILE
    n_kv = S // ATT_TILE
    x2d = x.reshape(rows, D_MODEL)
    cosv, sin_lo, sin_hi = _rope_lane_tables(S)

    row_spec = lambda w: pl.BlockSpec((ROW_TILE, w), lambda i: (i, 0))
    tab_spec = pl.BlockSpec((ROW_TILE, LANES), lambda i: (i % (S // ROW_TILE), 0))
    vt_spec = pl.BlockSpec((1, ROW_TILE // ATT_TILE, CHUNK, ATT_TILE),
                           lambda i: (i // (S // ROW_TILE), i % (S // ROW_TILE), 0, 0))
    vt_shape = jax.ShapeDtypeStruct((B, n_kv, CHUNK, ATT_TILE), _BF16)
    rows_bf16 = jax.ShapeDtypeStruct((rows, CHUNK), _BF16)
    rows_f32 = jax.ShapeDtypeStruct((rows, CHUNK), _F32)
    km_rows = ROW_TILE // MOBA_BLOCK
    (dq, dk, dvt, dgs, mq, mk, mvt, mgs, kmean, gates) = pl.pallas_call(
        _proj_kernel,
        grid=(n_row_tiles,),
        in_specs=[row_spec(D_MODEL), _resident((1, D_MODEL)), _resident((D_MODEL, N_CHUNKS * CHUNK)),
                  tab_spec, tab_spec, tab_spec],
        out_specs=[row_spec(CHUNK), row_spec(CHUNK), vt_spec, row_spec(CHUNK), row_spec(CHUNK),
                   row_spec(CHUNK), vt_spec, row_spec(CHUNK),
                   pl.BlockSpec((1, km_rows, CHUNK), lambda i: (i, 0, 0)),
                   row_spec(4 * CHUNK)],
        out_shape=[rows_bf16, rows_bf16, vt_shape, rows_f32, rows_f32, rows_bf16, vt_shape, rows_f32,
                   jax.ShapeDtypeStruct((n_row_tiles, km_rows, CHUNK), _F32),
                   jax.ShapeDtypeStruct((rows, 4 * CHUNK), _F32)],
        compiler_params=_params(1),
        name="proj",
    )(x2d, norm_g[0].reshape(1, D_MODEL), w_in[0].astype(_BF16), cosv, sin_lo, sin_hi)
    kmean = kmean.reshape(B * n_kv, CHUNK)

    n_groups = CHUNK // LANES
    q_spec = pl.BlockSpec((ATT_TILE, LANES), lambda b, h, i: (b * n_kv + i, h))
    k_spec = pl.BlockSpec((S, LANES), lambda b, h, i: (b, h))
    v_spec = pl.BlockSpec((1, n_kv, LANES, ATT_TILE), lambda b, h, i: (b, 0, h, 0))
    lam_spec = _resident((1, DIFF_HEAD_DIM))
    acc = lambda r: pltpu.VMEM((r, ATT_TILE), _F32)

    o_a = pl.pallas_call(
        _diff_kernel,
        grid=(B, n_groups, n_kv),
        in_specs=[lam_spec, lam_spec, lam_spec, lam_spec, q_spec, k_spec, v_spec, q_spec,
                  _resident((2 * DIFF_HEAD_DIM, 1))],
        out_specs=q_spec,
        out_shape=rows_bf16,
        scratch_shapes=[acc(2 * DIFF_HEAD_DIM), acc(2 * DIFF_HEAD_DIM)],
        compiler_params=_params(3),
        name="diff_attn",
    )(lambda_q1[0:1], lambda_k1[0:1], lambda_q2[0:1], lambda_k2[0:1], dq, dk, dvt, dgs,
      subln_g[0].reshape(2 * DIFF_HEAD_DIM, 1))

    o_b = pl.pallas_call(
        _moba_kernel,
        grid=(B, n_groups, n_kv),
        in_specs=[q_spec, k_spec, v_spec, pl.BlockSpec((n_kv, LANES), lambda b, h, i: (b, h)), q_spec],
        out_specs=q_spec,
        out_shape=rows_bf16,
        scratch_shapes=[acc(MOBA_HEAD_DIM), acc(MOBA_HEAD_DIM),
                        pltpu.VMEM((2, n_kv, ATT_TILE), _F32)],
        compiler_params=_params(3),
        name="moba_attn",
    )(mq, mk, mvt, kmean, mgs)

    out = pl.pallas_call(
        _out_kernel,
        grid=(n_row_tiles,),
        in_specs=[row_spec(CHUNK), row_spec(CHUNK), row_spec(4 * CHUNK), row_spec(D_MODEL),
                  row_spec(PLE_DIM), _resident((DIFF_WIDTH, D_MODEL)), _resident((MOBA_WIDTH, D_MODEL)),
                  _resident((D_MODEL, D_MODEL)), _resident((D_MODEL, D_MODEL)),
                  _resident((PLE_DIM, D_MODEL)), _resident((1, D_MODEL))],
        out_specs=row_spec(D_MODEL),
        out_shape=jax.ShapeDtypeStruct((rows, D_MODEL), _F32),
        compiler_params=_params(1),
        name="out_proj",
    )(o_a, o_b, gates, x2d, p[0].reshape(rows, PLE_DIM), w_branch_diff[0].astype(_BF16),
      w_branch_moba[0].astype(_BF16), w_out[0].astype(_BF16), w_ple_gate[0].astype(_BF16),
      w_ple[0].astype(_BF16), final_g.reshape(1, D_MODEL))
    return out.reshape(B, S, D_MODEL)
```

```python
import functools
import math

import jax
import jax.numpy as jnp
from jax import lax
from jax.experimental import pallas as pl
from jax.experimental.pallas import tpu as pltpu

D_MODEL = 1024
PLE_DIM = 256
DIFF_HEADS = 4
DIFF_HEAD_DIM = 64
DIFF_WIDTH = DIFF_HEADS * 2 * DIFF_HEAD_DIM
MOBA_HEADS = 8
MOBA_HEAD_DIM = 64
MOBA_WIDTH = MOBA_HEADS * MOBA_HEAD_DIM
MOBA_BLOCK = 256
MOBA_TOPK = 3
ROT_DIM = 16
ROPE_THETA = 500000.0
EPS = 1e-6
SUBLN_EPS = 1e-5
LAM_INIT = 0.8 - 0.6 * math.exp(-0.3 * 0)

LANES = 128
CHUNK = 512
N_CHUNKS = 12
ROW_TILE = 512
ATT_TILE = 512
VMEM_LIMIT = 56 * 1024 * 1024
_SCORE_SCALE = DIFF_HEAD_DIM ** -0.5 * math.log2(math.e)
assert DIFF_HEAD_DIM == MOBA_HEAD_DIM and ATT_TILE % MOBA_BLOCK == 0 and ROW_TILE % ATT_TILE == 0

_F32 = jnp.float32
_BF16 = jnp.bfloat16
_NT = (((1,), (1,)), ((), ()))


def _rope_rows(a, cosv, sin_lo, sin_hi):
    outs = []
    for j in range(CHUNK // LANES):
        xs = a[:, LANES * j:LANES * (j + 1)]
        up = pltpu.roll(xs, LANES - ROT_DIM // 2, 1)
        dn = pltpu.roll(xs, ROT_DIM // 2, 1)
        outs.append(xs * cosv + up * sin_lo + dn * sin_hi)
    return jnp.concatenate(outs, axis=1)


def _proj_kernel(x_ref, g_ref, w_ref, cos_ref, slo_ref, shi_ref,
                 dq_ref, dk_ref, dvt_ref, dg_ref, mq_ref, mk_ref, mvt_ref, mg_ref,
                 km_ref, gates_ref):
    x = x_ref[...]
    ms = jnp.mean(x * x, axis=-1, keepdims=True)
    h = (x * lax.rsqrt(ms + EPS) * g_ref[...]).astype(_BF16)
    cosv, sin_lo, sin_hi = cos_ref[...], slo_ref[...], shi_ref[...]

    def proj(c):
        return jnp.dot(h, w_ref[:, CHUNK * c:CHUNK * (c + 1)], preferred_element_type=_F32)

    def store_transposed(ref, a):
        at = a.T.astype(_BF16)
        for j in range(ROW_TILE // ATT_TILE):
            ref[0, j] = at[:, ATT_TILE * j:ATT_TILE * (j + 1)]

    dq_ref[...] = (_rope_rows(proj(0), cosv, sin_lo, sin_hi) * _SCORE_SCALE).astype(_BF16)
    dk_ref[...] = _rope_rows(proj(1), cosv, sin_lo, sin_hi).astype(_BF16)
    store_transposed(dvt_ref, proj(2))
    dg_ref[...] = jax.nn.silu(proj(3))
    mq_ref[...] = _rope_rows(proj(4), cosv, sin_lo, sin_hi)
    mk = _rope_rows(proj(5), cosv, sin_lo, sin_hi)
    mk_ref[...] = mk.astype(_BF16)
    for j in range(ROW_TILE // MOBA_BLOCK):
        km_ref[0, j:j + 1, :] = jnp.mean(mk[MOBA_BLOCK * j:MOBA_BLOCK * (j + 1)], axis=0, keepdims=True)
    store_transposed(mvt_ref, proj(6))
    mg_ref[...] = jax.nn.silu(proj(7))
    for c in range(8, N_CHUNKS):
        gates_ref[:, CHUNK * (c - 8):CHUNK * (c - 7)] = jax.nn.sigmoid(proj(c))


def _flash_tile(segs, m, l, acc_ref, vt, first):
    cands = []
    for s, bias in segs:
        cm = jnp.max(s, axis=0, keepdims=True)
        cands.append(cm if bias is None else cm + bias)
    m_new = functools.reduce(jnp.maximum, cands if first else [m] + cands)
    ps = []
    for s, bias in segs:
        ps.append(jnp.exp2(s - (m_new if bias is None else m_new - bias)))
    psum = functools.reduce(jnp.add, [jnp.sum(p, axis=0, keepdims=True) for p in ps])
    pb = [p.astype(_BF16) for p in ps]
    pv = jnp.dot(vt, pb[0] if len(pb) == 1 else jnp.concatenate(pb, axis=0),
                 preferred_element_type=_F32)
    if first:
        acc_ref[...] = pv
        return m_new, psum
    alpha = jnp.exp2(m - m_new)
    acc_ref[...] = alpha * acc_ref[...] + pv
    return m_new, alpha * l + psum


def _causal_mask(s):
    kpos = lax.broadcasted_iota(jnp.int32, s.shape, 0)
    qpos = lax.broadcasted_iota(jnp.int32, s.shape, 1)
    return jnp.where(kpos <= qpos, s, -jnp.inf)


def _split_heads(q):
    lane = lax.broadcasted_iota(jnp.int32, q.shape, 1)
    zero = jnp.zeros_like(q)
    return jnp.where(lane < 64, q, zero), jnp.where(lane >= 64, q, zero)


def _diff_kernel(lq1_ref, lk1_ref, lq2_ref, lk2_ref, q_ref, k_ref, vt_ref, dg_ref, sg_ref,
                 o_ref, acc1_ref, acc2_ref):
    qi = pl.program_id(2)
    q1, q2 = _split_heads(q_ref[...])

    def tile(kv, carry, first):
        m1, l1, m2, l2 = carry
        k = k_ref[pl.ds(pl.multiple_of(kv * ATT_TILE, ATT_TILE), ATT_TILE), :]
        vt = vt_ref[0, kv]
        s1 = lax.dot_general(k, q1, _NT, preferred_element_type=_F32)
        s2 = lax.dot_general(k, q2, _NT, preferred_element_type=_F32)
        if first:
            s1, s2 = _causal_mask(s1), _causal_mask(s2)
        m1, l1 = _flash_tile([(s1, None)], m1, l1, acc1_ref, vt, first)
        m2, l2 = _flash_tile([(s2, None)], m2, l2, acc2_ref, vt, first)
        return m1, l1, m2, l2

    z = jnp.zeros((1, ATT_TILE), _F32)
    carry = tile(qi, (z, z, z, z), True)
    _, l1, _, l2 = lax.fori_loop(0, qi, lambda kv, c: tile(kv, c, False), carry)

    lam = (jnp.exp(jnp.sum(lq1_ref[...] * lk1_ref[...], axis=-1, keepdims=True))
           - jnp.exp(jnp.sum(lq2_ref[...] * lk2_ref[...], axis=-1, keepdims=True)) + LAM_INIT)
    ot = acc1_ref[...] / l1 - lam * (acc2_ref[...] / l2)
    ms = jnp.mean(ot * ot, axis=0, keepdims=True)
    ot = ot * lax.rsqrt(ms + SUBLN_EPS) * sg_ref[...] * (1.0 - LAM_INIT)
    o_ref[...] = (ot.T * dg_ref[...]).astype(_BF16)


def _moba_kernel(q_ref, k_ref, vt_ref, km_ref, mg_ref, o_ref, acca_ref, accb_ref, bias_ref):
    qi = pl.program_id(2)
    q = q_ref[...]
    qa, qb = _split_heads((q * _SCORE_SCALE).astype(_BF16))
    nb = km_ref.shape[0]
    per_tile = ATT_TILE // MOBA_BLOCK

    blk = lax.broadcasted_iota(jnp.int32, (nb, ATT_TILE), 0)
    col = lax.broadcasted_iota(jnp.int32, (nb, ATT_TILE), 1)
    own = per_tile * qi + lax.shift_right_logical(col, MOBA_BLOCK.bit_length() - 1)
    for hh, kmh in enumerate(_split_heads(km_ref[...])):
        g = lax.dot_general(kmh, q, _NT, preferred_element_type=_F32, precision=lax.Precision.HIGHEST)
        g = jnp.where(blk < own, g, -jnp.inf)
        sel = jnp.zeros(g.shape, jnp.bool_)
        for _ in range(MOBA_TOPK):
            mx = jnp.max(g, axis=0, keepdims=True)
            first_idx = jnp.min(jnp.where(g == mx, blk, nb), axis=0, keepdims=True)
            pick = (blk == first_idx) & (mx > -jnp.inf)
            sel = sel | pick
            g = jnp.where(pick, -jnp.inf, g)
        bias_ref[hh] = jnp.where(sel | (blk == own), 0.0, -jnp.inf)

    def tile(kv, carry, first):
        ma, la, mb, lb = carry
        k = k_ref[pl.ds(pl.multiple_of(kv * ATT_TILE, ATT_TILE), ATT_TILE), :]
        out = []
        for hh, (qh, m, l, acc_ref) in enumerate(((qa, ma, la, acca_ref), (qb, mb, lb, accb_ref))):
            s = lax.dot_general(k, qh, _NT, preferred_element_type=_F32)
            if first:
                s = _causal_mask(s)
            vth = vt_ref[0, kv, MOBA_HEAD_DIM * hh:MOBA_HEAD_DIM * (hh + 1), :]
            segs = [(s[MOBA_BLOCK * j:MOBA_BLOCK * (j + 1)],
                     bias_ref[hh, pl.ds(per_tile * kv + j, 1), :]) for j in range(per_tile)]
            out.extend(_flash_tile(segs, m, l, acc_ref, vth, first))
        return tuple(out)

    z = jnp.zeros((1, ATT_TILE), _F32)
    carry = tile(qi, (z, z, z, z), True)
    _, la, _, lb = lax.fori_loop(0, qi, lambda kv, c: tile(kv, c, False), carry)
    ot = jnp.concatenate([acca_ref[...] / la, accb_ref[...] / lb], axis=0)
    o_ref[...] = (ot.T * mg_ref[...]).astype(_BF16)


def _out_kernel(oa_ref, ob_ref, gates_ref, x_ref, p_ref, wbd_ref, wbm_ref, wout_ref, wpg_ref,
                wple_ref, fg_ref, o_ref):
    ya = jnp.dot(oa_ref[...], wbd_ref[...], preferred_element_type=_F32)
    yb = jnp.dot(ob_ref[...], wbm_ref[...], preferred_element_type=_F32)
    merged = gates_ref[:, :D_MODEL] * ya + gates_ref[:, D_MODEL:] * yb
    x1 = x_ref[...] + jnp.dot(merged.astype(_BF16), wout_ref[...], preferred_element_type=_F32)
    t = jnp.dot(x1.astype(_BF16), wpg_ref[...], preferred_element_type=_F32)
    pe = jnp.dot(p_ref[...].astype(_BF16), wple_ref[...], preferred_element_type=_F32)
    x2 = x1 + jax.nn.sigmoid(t) * pe
    ms = jnp.mean(x2 * x2, axis=-1, keepdims=True)
    o_ref[...] = x2 * lax.rsqrt(ms + EPS) * fg_ref[...]


def _rope_lane_tables(seq):
    half = ROT_DIM // 2
    inv = ROPE_THETA ** (-jnp.arange(0, ROT_DIM, 2, dtype=_F32) / ROT_DIM)
    ang = jnp.arange(seq, dtype=_F32)[:, None] * inv[None, :]
    cos, sin = jnp.cos(ang), jnp.sin(ang)
    pad = MOBA_HEAD_DIM - ROT_DIM
    one, zero = jnp.ones((seq, pad), _F32), jnp.zeros((seq, pad), _F32)
    zh = jnp.zeros((seq, half), _F32)
    cos64 = jnp.concatenate([cos, cos, one], axis=1)
    slo64 = jnp.concatenate([-sin, zh, zero], axis=1)
    shi64 = jnp.concatenate([zh, sin, zero], axis=1)
    return tuple(jnp.tile(t, (1, LANES // MOBA_HEAD_DIM)) for t in (cos64, slo64, shi64))


def _resident(shape):
    return pl.BlockSpec(shape, lambda *_: (0,) * len(shape), pipeline_mode=pl.Buffered(1))


def _params(n_axes):
    return pltpu.CompilerParams(dimension_semantics=("arbitrary",) * n_axes,
                                vmem_limit_bytes=VMEM_LIMIT)


def kernel(x, p, norm_g, w_in, lambda_q1, lambda_k1, lambda_q2, lambda_k2, subln_g,
           w_branch_diff, w_branch_moba, w_out, w_ple, w_ple_gate, final_g):
    B, S, _ = x.shape
    assert w_in.shape == (1, D_MODEL, N_CHUNKS * CHUNK) and S % ROW_TILE == 0
    rows = B * S
    n_row_tiles = rows // ROW_TILE
    n_kv = S // ATT_TILE
    n_blk = S // MOBA_BLOCK
    x2d = x.reshape(rows, D_MODEL)
    cosv, sin_lo, sin_hi = _rope_lane_tables(S)

    row_spec = lambda w: pl.BlockSpec((ROW_TILE, w), lambda i: (i, 0))
    tab_spec = pl.BlockSpec((ROW_TILE, LANES), lambda i: (i % (S // ROW_TILE), 0))
    vt_spec = pl.BlockSpec((1, ROW_TILE // ATT_TILE, CHUNK, ATT_TILE),
                           lambda i: (i // (S // ROW_TILE), i % (S // ROW_TILE), 0, 0))
    vt_shape = jax.ShapeDtypeStruct((B, n_kv, CHUNK, ATT_TILE), _BF16)
    rows_bf16 = jax.ShapeDtypeStruct((rows, CHUNK), _BF16)
    rows_f32 = jax.ShapeDtypeStruct((rows, CHUNK), _F32)
    km_rows = ROW_TILE // MOBA_BLOCK
    (dq, dk, dvt, dgs, mq, mk, mvt, mgs, kmean, gates) = pl.pallas_call(
        _proj_kernel,
        grid=(n_row_tiles,),
        in_specs=[row_spec(D_MODEL), _resident((1, D_MODEL)), _resident((D_MODEL, N_CHUNKS * CHUNK)),
                  tab_spec, tab_spec, tab_spec],
        out_specs=[row_spec(CHUNK), row_spec(CHUNK), vt_spec, row_spec(CHUNK), row_spec(CHUNK),
                   row_spec(CHUNK), vt_spec, row_spec(CHUNK),
                   pl.BlockSpec((1, km_rows, CHUNK), lambda i: (i, 0, 0)),
                   row_spec(4 * CHUNK)],
        out_shape=[rows_bf16, rows_bf16, vt_shape, rows_f32, rows_f32, rows_bf16, vt_shape, rows_f32,
                   jax.ShapeDtypeStruct((n_row_tiles, km_rows, CHUNK), _F32),
                   jax.ShapeDtypeStruct((rows, 4 * CHUNK), _F32)],
        compiler_params=_params(1),
        name="proj",
    )(x2d, norm_g[0].reshape(1, D_MODEL), w_in[0].astype(_BF16), cosv, sin_lo, sin_hi)
    kmean = kmean.reshape(B * n_blk, CHUNK)

    n_groups = CHUNK // LANES
    q_spec = pl.BlockSpec((ATT_TILE, LANES), lambda b, h, i: (b * n_kv + i, h))
    k_spec = pl.BlockSpec((S, LANES), lambda b, h, i: (b, h))
    v_spec = pl.BlockSpec((1, n_kv, LANES, ATT_TILE), lambda b, h, i: (b, 0, h, 0))
    lam_spec = _resident((1, DIFF_HEAD_DIM))
    acc = lambda r: pltpu.VMEM((r, ATT_TILE), _F32)

    o_a = pl.pallas_call(
        _diff_kernel,
        grid=(B, n_groups, n_kv),
        in_specs=[lam_spec, lam_spec, lam_spec, lam_spec, q_spec, k_spec, v_spec, q_spec,
                  _resident((2 * DIFF_HEAD_DIM, 1))],
        out_specs=q_spec,
        out_shape=rows_bf16,
        scratch_shapes=[acc(2 * DIFF_HEAD_DIM), acc(2 * DIFF_HEAD_DIM)],
        compiler_params=_params(3),
        name="diff_attn",
    )(lambda_q1[0:1], lambda_k1[0:1], lambda_q2[0:1], lambda_k2[0:1], dq, dk, dvt, dgs,
      subln_g[0].reshape(2 * DIFF_HEAD_DIM, 1))

    o_b = pl.pallas_call(
        _moba_kernel,
        grid=(B, n_groups, n_kv),
        in_specs=[q_spec, k_spec, v_spec, pl.BlockSpec((n_blk, LANES), lambda b, h, i: (b, h)), q_spec],
        out_specs=q_spec,
        out_shape=rows_bf16,
        scratch_shapes=[acc(MOBA_HEAD_DIM), acc(MOBA_HEAD_DIM),
                        pltpu.VMEM((2, n_blk, ATT_TILE), _F32)],
        compiler_params=_params(3),
        name="moba_attn",
    )(mq, mk, mvt, kmean, mgs)

    out = pl.pallas_call(
        _out_kernel,
        grid=(n_row_tiles,),
        in_specs=[row_spec(CHUNK), row_spec(CHUNK), row_spec(4 * CHUNK), row_spec(D_MODEL),
                  row_spec(PLE_DIM), _resident((DIFF_WIDTH, D_MODEL)), _resident((MOBA_WIDTH, D_MODEL)),
                  _resident((D_MODEL, D_MODEL)), _resident((D_MODEL, D_MODEL)),
                  _resident((PLE_DIM, D_MODEL)), _resident((1, D_MODEL))],
        out_specs=row_spec(D_MODEL),
        out_shape=jax.ShapeDtypeStruct((rows, D_MODEL), _F32),
        compiler_params=_params(1),
        name="out_proj",
    )(o_a, o_b, gates, x2d, p[0].reshape(rows, PLE_DIM), w_branch_diff[0].astype(_BF16),
      w_branch_moba[0].astype(_BF16), w_out[0].astype(_BF16), w_ple_gate[0].astype(_BF16),
      w_ple[0].astype(_BF16), final_g.reshape(1, D_MODEL))
    return out.reshape(B, S, D_MODEL)
```

```python
import functools
import math

import jax
import jax.numpy as jnp
from jax import lax
from jax.experimental import pallas as pl
from jax.experimental.pallas import tpu as pltpu

D_MODEL = 1024
PLE_DIM = 256
DIFF_HEADS = 4
DIFF_HEAD_DIM = 64
DIFF_WIDTH = DIFF_HEADS * 2 * DIFF_HEAD_DIM
MOBA_HEADS = 8
MOBA_HEAD_DIM = 64
MOBA_WIDTH = MOBA_HEADS * MOBA_HEAD_DIM
MOBA_BLOCK = 256
MOBA_TOPK = 3
ROT_DIM = 16
ROPE_THETA = 500000.0
EPS = 1e-6
SUBLN_EPS = 1e-5
LAM_INIT = 0.8 - 0.6 * math.exp(-0.3 * 0)

LANES = 128
BF16_ROWS = 16
CHUNK = 512
N_CHUNKS = 12
ROW_TILE = 512
ATT_TILE = 512
VMEM_LIMIT = 56 * 1024 * 1024
_SCORE_SCALE = DIFF_HEAD_DIM ** -0.5 * math.log2(math.e)
DIFF_V_ROWS = 2 * DIFF_HEAD_DIM + BF16_ROWS
MOBA_V_ROWS = MOBA_HEAD_DIM + BF16_ROWS
assert DIFF_HEAD_DIM == MOBA_HEAD_DIM and ATT_TILE % MOBA_BLOCK == 0 and ROW_TILE == ATT_TILE

_F32 = jnp.float32
_BF16 = jnp.bfloat16
_NT = (((1,), (1,)), ((), ()))


def _rope_rows(a, cosv, sin_lo, sin_hi):
    outs = []
    for j in range(CHUNK // LANES):
        xs = a[:, LANES * j:LANES * (j + 1)]
        up = pltpu.roll(xs, LANES - ROT_DIM // 2, 1)
        dn = pltpu.roll(xs, ROT_DIM // 2, 1)
        outs.append(xs * cosv + up * sin_lo + dn * sin_hi)
    return jnp.concatenate(outs, axis=1)


def _proj_kernel(x_ref, g_ref, w_ref, cos_ref, slo_ref, shi_ref,
                 dq_ref, dk_ref, dvt_ref, dg_ref, mq_ref, mk_ref, mvt_ref, mg_ref,
                 km_ref, gates_ref):
    x = x_ref[...]
    ms = jnp.mean(x * x, axis=-1, keepdims=True)
    h = (x * lax.rsqrt(ms + EPS) * g_ref[...]).astype(_BF16)
    cosv, sin_lo, sin_hi = cos_ref[...], slo_ref[...], shi_ref[...]

    def proj(c):
        return jnp.dot(h, w_ref[:, CHUNK * c:CHUNK * (c + 1)], preferred_element_type=_F32)

    def store_transposed(ref, a, head_rows):
        at = a.T.astype(_BF16)
        ones = jnp.ones((BF16_ROWS, ROW_TILE), _BF16)
        stride = head_rows + BF16_ROWS
        for g in range(CHUNK // head_rows):
            ref[0, 0, stride * g:stride * g + head_rows, :] = at[head_rows * g:head_rows * (g + 1)]
            ref[0, 0, stride * g + head_rows:stride * (g + 1), :] = ones

    dq_ref[...] = (_rope_rows(proj(0), cosv, sin_lo, sin_hi) * _SCORE_SCALE).astype(_BF16)
    dk_ref[...] = _rope_rows(proj(1), cosv, sin_lo, sin_hi).astype(_BF16)
    store_transposed(dvt_ref, proj(2), 2 * DIFF_HEAD_DIM)
    dg_ref[...] = jax.nn.silu(proj(3))
    mq_ref[...] = _rope_rows(proj(4), cosv, sin_lo, sin_hi)
    mk = _rope_rows(proj(5), cosv, sin_lo, sin_hi)
    mk_ref[...] = mk.astype(_BF16)
    for j in range(ROW_TILE // MOBA_BLOCK):
        km_ref[0, j:j + 1, :] = jnp.mean(mk[MOBA_BLOCK * j:MOBA_BLOCK * (j + 1)], axis=0, keepdims=True)
    store_transposed(mvt_ref, proj(6), MOBA_HEAD_DIM)
    mg_ref[...] = jax.nn.silu(proj(7))
    for c in range(8, N_CHUNKS):
        gates_ref[:, CHUNK * (c - 8):CHUNK * (c - 7)] = jax.nn.sigmoid(proj(c))


def _flash_update(segs, m, acc_ref, vt):
    cands = [m]
    for s, bias in segs:
        cm = jnp.max(s, axis=0, keepdims=True)
        cands.append(cm if bias is None else cm + bias)
    m_new = functools.reduce(jnp.maximum, cands)
    pb = [jnp.exp2(s - (m_new if bias is None else m_new - bias)).astype(_BF16) for s, bias in segs]
    pv = jnp.dot(vt, pb[0] if len(pb) == 1 else jnp.concatenate(pb, axis=0),
                 preferred_element_type=_F32)
    acc_ref[...] = jnp.exp2(m - m_new) * acc_ref[...] + pv
    return m_new


def _causal_mask(s):
    kpos = lax.broadcasted_iota(jnp.int32, s.shape, 0)
    qpos = lax.broadcasted_iota(jnp.int32, s.shape, 1)
    return jnp.where(kpos <= qpos, s, -jnp.inf)


def _split_heads(q):
    lane = lax.broadcasted_iota(jnp.int32, q.shape, 1)
    zero = jnp.zeros_like(q)
    return jnp.where(lane < 64, q, zero), jnp.where(lane >= 64, q, zero)


def _pipelined_tiles(qi, scores, update, init):
    scores(qi, 0, True)

    def pair(j, state):
        carry, in_slot0 = state
        scores(2 * j, 1, False)
        carry = update(in_slot0, 0, carry)
        scores(2 * j + 1, 0, False)
        carry = update(2 * j, 1, carry)
        return carry, 2 * j + 1

    carry, in_slot0 = lax.fori_loop(0, qi // 2, pair, (init, qi))

    def odd_tail(carry):
        scores(qi - 1, 1, False)
        carry = update(in_slot0, 0, carry)
        return update(qi - 1, 1, carry)

    return lax.cond(qi % 2 == 1, odd_tail, lambda c: update(in_slot0, 0, c), carry)


def _diff_kernel(lq1_ref, lk1_ref, lq2_ref, lk2_ref, q_ref, k_ref, vt_ref, dg_ref, sg_ref,
                 o_ref, acc1_ref, acc2_ref, s1_ref, s2_ref):
    qi = pl.program_id(2)
    q1, q2 = _split_heads(q_ref[...])
    acc1_ref[...] = jnp.zeros_like(acc1_ref)
    acc2_ref[...] = jnp.zeros_like(acc2_ref)

    def scores(tile, slot, diagonal):
        k = k_ref[pl.ds(pl.multiple_of(tile * ATT_TILE, ATT_TILE), ATT_TILE), :]
        for qm, s_ref in ((q1, s1_ref), (q2, s2_ref)):
            s = lax.dot_general(k, qm, _NT, preferred_element_type=_F32)
            s_ref[slot] = _causal_mask(s) if diagonal else s

    def update(tile, slot, carry):
        vt = vt_ref[0, tile]
        return (_flash_update([(s1_ref[slot], None)], carry[0], acc1_ref, vt),
                _flash_update([(s2_ref[slot], None)], carry[1], acc2_ref, vt))

    ninf = jnp.full((1, ATT_TILE), -jnp.inf, _F32)
    _pipelined_tiles(qi, scores, update, (ninf, ninf))

    dv = 2 * DIFF_HEAD_DIM
    lam = (jnp.exp(jnp.sum(lq1_ref[...] * lk1_ref[...], axis=-1, keepdims=True))
           - jnp.exp(jnp.sum(lq2_ref[...] * lk2_ref[...], axis=-1, keepdims=True)) + LAM_INIT)
    ot = (acc1_ref[:dv] / acc1_ref[dv:dv + 1]
          - lam * (acc2_ref[:dv] / acc2_ref[dv:dv + 1]))
    ms = jnp.mean(ot * ot, axis=0, keepdims=True)
    ot = ot * lax.rsqrt(ms + SUBLN_EPS) * sg_ref[...] * (1.0 - LAM_INIT)
    o_ref[...] = (ot.T * dg_ref[...]).astype(_BF16)


def _moba_kernel(q_ref, k_ref, vt_ref, km_ref, mg_ref, o_ref, acca_ref, accb_ref, sa_ref, sb_ref,
                 bias_ref):
    qi = pl.program_id(2)
    q = q_ref[...]
    qa, qb = _split_heads((q * _SCORE_SCALE).astype(_BF16))
    nb = km_ref.shape[0]
    per_tile = ATT_TILE // MOBA_BLOCK
    acca_ref[...] = jnp.zeros_like(acca_ref)
    accb_ref[...] = jnp.zeros_like(accb_ref)

    blk = lax.broadcasted_iota(jnp.int32, (nb, ATT_TILE), 0)
    col = lax.broadcasted_iota(jnp.int32, (nb, ATT_TILE), 1)
    own = per_tile * qi + lax.shift_right_logical(col, MOBA_BLOCK.bit_length() - 1)
    for hh, kmh in enumerate(_split_heads(km_ref[...])):
        g = lax.dot_general(kmh, q, _NT, preferred_element_type=_F32, precision=lax.Precision.HIGHEST)
        g = jnp.where(blk < own, g, -jnp.inf)
        sel = jnp.zeros(g.shape, jnp.bool_)
        for _ in range(MOBA_TOPK):
            mx = jnp.max(g, axis=0, keepdims=True)
            first_idx = jnp.min(jnp.where(g == mx, blk, nb), axis=0, keepdims=True)
            pick = (blk == first_idx) & (mx > -jnp.inf)
            sel = sel | pick
            g = jnp.where(pick, -jnp.inf, g)
        bias_ref[hh] = jnp.where(sel | (blk == own), 0.0, -jnp.inf)

    def scores(tile, slot, diagonal):
        k = k_ref[pl.ds(pl.multiple_of(tile * ATT_TILE, ATT_TILE), ATT_TILE), :]
        for qh, s_ref in ((qa, sa_ref), (qb, sb_ref)):
            s = lax.dot_general(k, qh, _NT, preferred_element_type=_F32)
            s_ref[slot] = _causal_mask(s) if diagonal else s

    def update(tile, slot, carry):
        out = []
        for hh, (s_ref, acc_ref) in enumerate(((sa_ref, acca_ref), (sb_ref, accb_ref))):
            vth = vt_ref[0, tile, MOBA_V_ROWS * hh:MOBA_V_ROWS * (hh + 1), :]
            segs = [(s_ref[slot, MOBA_BLOCK * j:MOBA_BLOCK * (j + 1), :],
                     bias_ref[hh, pl.ds(per_tile * tile + j, 1), :]) for j in range(per_tile)]
            out.append(_flash_update(segs, carry[hh], acc_ref, vth))
        return tuple(out)

    ninf = jnp.full((1, ATT_TILE), -jnp.inf, _F32)
    _pipelined_tiles(qi, scores, update, (ninf, ninf))

    dh = MOBA_HEAD_DIM
    ot = jnp.concatenate([acca_ref[:dh] / acca_ref[dh:dh + 1], accb_ref[:dh] / accb_ref[dh:dh + 1]],
                         axis=0)
    o_ref[...] = (ot.T * mg_ref[...]).astype(_BF16)


def _out_kernel(oa_ref, ob_ref, gates_ref, x_ref, p_ref, wbd_ref, wbm_ref, wout_ref, wpg_ref,
                wple_ref, fg_ref, o_ref):
    ya = jnp.dot(oa_ref[...], wbd_ref[...], preferred_element_type=_F32)
    yb = jnp.dot(ob_ref[...], wbm_ref[...], preferred_element_type=_F32)
    merged = gates_ref[:, :D_MODEL] * ya + gates_ref[:, D_MODEL:] * yb
    x1 = x_ref[...] + jnp.dot(merged.astype(_BF16), wout_ref[...], preferred_element_type=_F32)
    t = jnp.dot(x1.astype(_BF16), wpg_ref[...], preferred_element_type=_F32)
    pe = jnp.dot(p_ref[...].astype(_BF16), wple_ref[...], preferred_element_type=_F32)
    x2 = x1 + jax.nn.sigmoid(t) * pe
    ms = jnp.mean(x2 * x2, axis=-1, keepdims=True)
    o_ref[...] = x2 * lax.rsqrt(ms + EPS) * fg_ref[...]


def _rope_lane_tables(seq):
    half = ROT_DIM // 2
    inv = ROPE_THETA ** (-jnp.arange(0, ROT_DIM, 2, dtype=_F32) / ROT_DIM)
    ang = jnp.arange(seq, dtype=_F32)[:, None] * inv[None, :]
    cos, sin = jnp.cos(ang), jnp.sin(ang)
    pad = MOBA_HEAD_DIM - ROT_DIM
    one, zero = jnp.ones((seq, pad), _F32), jnp.zeros((seq, pad), _F32)
    zh = jnp.zeros((seq, half), _F32)
    cos64 = jnp.concatenate([cos, cos, one], axis=1)
    slo64 = jnp.concatenate([-sin, zh, zero], axis=1)
    shi64 = jnp.concatenate([zh, sin, zero], axis=1)
    return tuple(jnp.tile(t, (1, LANES // MOBA_HEAD_DIM)) for t in (cos64, slo64, shi64))


def _resident(shape):
    return pl.BlockSpec(shape, lambda *_: (0,) * len(shape), pipeline_mode=pl.Buffered(1))


def _params(n_axes):
    return pltpu.CompilerParams(dimension_semantics=("arbitrary",) * n_axes,
                                vmem_limit_bytes=VMEM_LIMIT)


def kernel(x, p, norm_g, w_in, lambda_q1, lambda_k1, lambda_q2, lambda_k2, subln_g,
           w_branch_diff, w_branch_moba, w_out, w_ple, w_ple_gate, final_g):
    B, S, _ = x.shape
    assert w_in.shape == (1, D_MODEL, N_CHUNKS * CHUNK) and S % ROW_TILE == 0
    rows = B * S
    n_row_tiles = rows // ROW_TILE
    n_kv = S // ATT_TILE
    n_blk = S // MOBA_BLOCK
    x2d = x.reshape(rows, D_MODEL)
    cosv, sin_lo, sin_hi = _rope_lane_tables(S)

    row_spec = lambda w: pl.BlockSpec((ROW_TILE, w), lambda i: (i, 0))
    tab_spec = pl.BlockSpec((ROW_TILE, LANES), lambda i: (i % n_kv, 0))
    dvt_rows, mvt_rows = DIFF_HEADS * DIFF_V_ROWS, MOBA_HEADS * MOBA_V_ROWS
    vt_spec = lambda r: pl.BlockSpec((1, 1, r, ATT_TILE), lambda i: (i // n_kv, i % n_kv, 0, 0))
    vt_shape = lambda r: jax.ShapeDtypeStruct((B, n_kv, r, ATT_TILE), _BF16)
    rows_bf16 = jax.ShapeDtypeStruct((rows, CHUNK), _BF16)
    rows_f32 = jax.ShapeDtypeStruct((rows, CHUNK), _F32)
    km_rows = ROW_TILE // MOBA_BLOCK
    (dq, dk, dvt, dgs, mq, mk, mvt, mgs, kmean, gates) = pl.pallas_call(
        _proj_kernel,
        grid=(n_row_tiles,),
        in_specs=[row_spec(D_MODEL), _resident((1, D_MODEL)), _resident((D_MODEL, N_CHUNKS * CHUNK)),
                  tab_spec, tab_spec, tab_spec],
        out_specs=[row_spec(CHUNK), row_spec(CHUNK), vt_spec(dvt_rows), row_spec(CHUNK), row_spec(CHUNK),
                   row_spec(CHUNK), vt_spec(mvt_rows), row_spec(CHUNK),
                   pl.BlockSpec((1, km_rows, CHUNK), lambda i: (i, 0, 0)),
                   row_spec(4 * CHUNK)],
        out_shape=[rows_bf16, rows_bf16, vt_shape(dvt_rows), rows_f32, rows_f32, rows_bf16,
                   vt_shape(mvt_rows), rows_f32,
                   jax.ShapeDtypeStruct((n_row_tiles, km_rows, CHUNK), _F32),
                   jax.ShapeDtypeStruct((rows, 4 * CHUNK), _F32)],
        compiler_params=_params(1),
        name="proj",
    )(x2d, norm_g[0].reshape(1, D_MODEL), w_in[0].astype(_BF16), cosv, sin_lo, sin_hi)
    kmean = kmean.reshape(B * n_blk, CHUNK)

    n_groups = CHUNK // LANES
    q_spec = pl.BlockSpec((ATT_TILE, LANES), lambda b, h, i: (b * n_kv + i, h))
    k_spec = pl.BlockSpec((S, LANES), lambda b, h, i: (b, h))
    v_spec = lambda r: pl.BlockSpec((1, n_kv, r, ATT_TILE), lambda b, h, i: (b, 0, h, 0))
    lam_spec = _resident((1, DIFF_HEAD_DIM))
    acc = lambda r: pltpu.VMEM((r, ATT_TILE), _F32)
    score_buf = pltpu.VMEM((2, ATT_TILE, ATT_TILE), _F32)

    o_a = pl.pallas_call(
        _diff_kernel,
        grid=(B, n_groups, n_kv),
        in_specs=[lam_spec, lam_spec, lam_spec, lam_spec, q_spec, k_spec, v_spec(DIFF_V_ROWS), q_spec,
                  _resident((2 * DIFF_HEAD_DIM, 1))],
        out_specs=q_spec,
        out_shape=rows_bf16,
        scratch_shapes=[acc(DIFF_V_ROWS), acc(DIFF_V_ROWS), score_buf, score_buf],
        compiler_params=_params(3),
        name="diff_attn",
    )(lambda_q1[0:1], lambda_k1[0:1], lambda_q2[0:1], lambda_k2[0:1], dq, dk, dvt, dgs,
      subln_g[0].reshape(2 * DIFF_HEAD_DIM, 1))

    o_b = pl.pallas_call(
        _moba_kernel,
        grid=(B, n_groups, n_kv),
        in_specs=[q_spec, k_spec, v_spec(2 * MOBA_V_ROWS),
                  pl.BlockSpec((n_blk, LANES), lambda b, h, i: (b, h)), q_spec],
        out_specs=q_spec,
        out_shape=rows_bf16,
        scratch_shapes=[acc(MOBA_V_ROWS), acc(MOBA_V_ROWS), score_buf, score_buf,
                        pltpu.VMEM((2, n_blk, ATT_TILE), _F32)],
        compiler_params=_params(3),
        name="moba_attn",
    )(mq, mk, mvt, kmean, mgs)

    out = pl.pallas_call(
        _out_kernel,
        grid=(n_row_tiles,),
        in_specs=[row_spec(CHUNK), row_spec(CHUNK), row_spec(4 * CHUNK), row_spec(D_MODEL),
                  row_spec(PLE_DIM), _resident((DIFF_WIDTH, D_MODEL)), _resident((MOBA_WIDTH, D_MODEL)),
                  _resident((D_MODEL, D_MODEL)), _resident((D_MODEL, D_MODEL)),
                  _resident((PLE_DIM, D_MODEL)), _resident((1, D_MODEL))],
        out_specs=row_spec(D_MODEL),
        out_shape=jax.ShapeDtypeStruct((rows, D_MODEL), _F32),
        compiler_params=_params(1),
        name="out_proj",
    )(o_a, o_b, gates, x2d, p[0].reshape(rows, PLE_DIM), w_branch_diff[0].astype(_BF16),
      w_branch_moba[0].astype(_BF16), w_out[0].astype(_BF16), w_ple_gate[0].astype(_BF16),
      w_ple[0].astype(_BF16), final_g.reshape(1, D_MODEL))
    return out.reshape(B, S, D_MODEL)
```

```python
import functools
import math

import jax
import jax.numpy as jnp
import numpy as np
from jax import lax
from jax.experimental import pallas as pl
from jax.experimental.pallas import tpu as pltpu

D_MODEL = 1024
PLE_DIM = 256
DIFF_HEADS = 4
DIFF_HEAD_DIM = 64
DIFF_WIDTH = DIFF_HEADS * 2 * DIFF_HEAD_DIM
MOBA_HEADS = 8
MOBA_HEAD_DIM = 64
MOBA_WIDTH = MOBA_HEADS * MOBA_HEAD_DIM
MOBA_BLOCK = 256
MOBA_TOPK = 3
ROT_DIM = 16
ROPE_THETA = 500000.0
EPS = 1e-6
SUBLN_EPS = 1e-5
LAM_INIT = 0.8 - 0.6 * math.exp(-0.3 * 0)

LANES = 128
BF16_ROWS = 16
CHUNK = 512
N_CHUNKS = 12
ROW_TILE = 512
ATT_TILE = 512
PIPE_UNROLL = 8
VMEM_LIMIT = 56 * 1024 * 1024
_SCORE_SCALE = DIFF_HEAD_DIM ** -0.5 * math.log2(math.e)
DIFF_V_ROWS = 2 * DIFF_HEAD_DIM + BF16_ROWS
MOBA_V_ROWS = MOBA_HEAD_DIM + BF16_ROWS
assert DIFF_HEAD_DIM == MOBA_HEAD_DIM and ATT_TILE % MOBA_BLOCK == 0 and ROW_TILE == ATT_TILE

_F32 = jnp.float32
_BF16 = jnp.bfloat16
_NT = (((1,), (1,)), ((), ()))


def _rope_rows(a, cosv, sin_lo, sin_hi):
    outs = []
    for j in range(CHUNK // LANES):
        xs = a[:, LANES * j:LANES * (j + 1)]
        up = pltpu.roll(xs, LANES - ROT_DIM // 2, 1)
        dn = pltpu.roll(xs, ROT_DIM // 2, 1)
        outs.append(xs * cosv + up * sin_lo + dn * sin_hi)
    return jnp.concatenate(outs, axis=1)


def _proj_kernel(x_ref, g_ref, w_ref, cos_ref, slo_ref, shi_ref,
                 dq_ref, dk_ref, dvt_ref, dg_ref, mq_ref, mk_ref, mvt_ref, mg_ref,
                 km_ref, gates_ref):
    x = x_ref[...]
    ms = jnp.mean(x * x, axis=-1, keepdims=True)
    h = (x * lax.rsqrt(ms + EPS) * g_ref[...]).astype(_BF16)
    cosv, sin_lo, sin_hi = cos_ref[...], slo_ref[...], shi_ref[...]

    def proj(c):
        return jnp.dot(h, w_ref[:, CHUNK * c:CHUNK * (c + 1)], preferred_element_type=_F32)

    def store_transposed(ref, a, head_rows):
        at = a.T.astype(_BF16)
        ones = jnp.ones((BF16_ROWS, ROW_TILE), _BF16)
        stride = head_rows + BF16_ROWS
        for g in range(CHUNK // head_rows):
            ref[0, 0, stride * g:stride * g + head_rows, :] = at[head_rows * g:head_rows * (g + 1)]
            ref[0, 0, stride * g + head_rows:stride * (g + 1), :] = ones

    dq_ref[...] = (_rope_rows(proj(0), cosv, sin_lo, sin_hi) * _SCORE_SCALE).astype(_BF16)
    dk_ref[...] = _rope_rows(proj(1), cosv, sin_lo, sin_hi).astype(_BF16)
    store_transposed(dvt_ref, proj(2), 2 * DIFF_HEAD_DIM)
    dg_ref[...] = jax.nn.silu(proj(3))
    mq_ref[...] = _rope_rows(proj(4), cosv, sin_lo, sin_hi)
    mk = _rope_rows(proj(5), cosv, sin_lo, sin_hi)
    mk_ref[...] = mk.astype(_BF16)
    for j in range(ROW_TILE // MOBA_BLOCK):
        km_ref[0, j:j + 1, :] = jnp.mean(mk[MOBA_BLOCK * j:MOBA_BLOCK * (j + 1)], axis=0, keepdims=True)
    store_transposed(mvt_ref, proj(6), MOBA_HEAD_DIM)
    mg_ref[...] = jax.nn.silu(proj(7))
    for c in range(8, N_CHUNKS):
        gates_ref[:, CHUNK * (c - 8):CHUNK * (c - 7)] = jax.nn.sigmoid(proj(c))


def _flash_update(segs, m, acc_ref, vt):
    cands = [] if m is None else [m]
    for s, bias in segs:
        cm = jnp.max(s, axis=0, keepdims=True)
        cands.append(cm if bias is None else cm + bias)
    m_new = functools.reduce(jnp.maximum, cands)
    pb = [jnp.exp2(s - (m_new if bias is None else m_new - bias)).astype(_BF16) for s, bias in segs]
    pv = jnp.dot(vt, pb[0] if len(pb) == 1 else jnp.concatenate(pb, axis=0),
                 preferred_element_type=_F32)
    acc_ref[...] = pv if m is None else jnp.exp2(m - m_new) * acc_ref[...] + pv
    return m_new


def _causal_mask(s):
    kpos = lax.broadcasted_iota(jnp.int32, s.shape, 0)
    qpos = lax.broadcasted_iota(jnp.int32, s.shape, 1)
    return jnp.where(kpos <= qpos, s, -jnp.inf)


def _split_heads(q):
    lane = lax.broadcasted_iota(jnp.int32, q.shape, 1)
    zero = jnp.zeros_like(q)
    return jnp.where(lane < 64, q, zero), jnp.where(lane >= 64, q, zero)


def _tile_rows(i):
    return pl.ds(pl.multiple_of(i * ATT_TILE, ATT_TILE), ATT_TILE)


def _past_items(n_q):
    items = [(qi, t) for qi in range(1, n_q) for t in range(qi)]
    assert len(items) % 2 == 0
    items.append(items[-1])
    return (np.array([i[0] for i in items], np.int32), np.array([i[1] for i in items], np.int32))


def _run_attention(n_q, qtab_ref, ttab_ref, scores, update):
    def diag_pair(j, _):
        scores(2 * j, 2 * j, 0, True)
        scores(2 * j + 1, 2 * j + 1, 1, True)
        update(2 * j, 2 * j, 0, True)
        update(2 * j + 1, 2 * j + 1, 1, True)
        return 0

    lax.fori_loop(0, n_q // 2, diag_pair, 0)

    scores(qtab_ref[0], ttab_ref[0], 0, False)

    def step(j, _):
        for u in range(PIPE_UNROLL):
            t = PIPE_UNROLL * j + u
            scores(qtab_ref[t + 1], ttab_ref[t + 1], (u + 1) % 2, False)
            update(qtab_ref[t], ttab_ref[t], u % 2, False)
        return 0

    n_items = n_q * (n_q - 1) // 2
    assert n_items % PIPE_UNROLL == 0 and PIPE_UNROLL % 2 == 0
    lax.fori_loop(0, n_items // PIPE_UNROLL, step, 0)


def _diff_kernel(qtab_ref, ttab_ref, lq1_ref, lk1_ref, lq2_ref, lk2_ref, q_ref, k_ref, vt_ref,
                 dg_ref, sg_ref, o_ref, acc1_ref, acc2_ref, m1_ref, m2_ref, s1_ref, s2_ref):
    n_q = acc1_ref.shape[0]
    maps = ((s1_ref, acc1_ref, m1_ref), (s2_ref, acc2_ref, m2_ref))

    def scores(qi, tile, slot, diagonal):
        k = k_ref[_tile_rows(tile), :]
        for qm, (s_ref, _, _) in zip(_split_heads(q_ref[_tile_rows(qi), :]), maps):
            s = lax.dot_general(k, qm, _NT, preferred_element_type=_F32)
            s_ref[slot] = _causal_mask(s) if diagonal else s

    def update(qi, tile, slot, first):
        vt = vt_ref[0, tile]
        for s_ref, acc_ref, m_ref in maps:
            m_ref[qi] = _flash_update([(s_ref[slot], None)], None if first else m_ref[qi],
                                      acc_ref.at[qi], vt)

    _run_attention(n_q, qtab_ref, ttab_ref, scores, update)

    dv = 2 * DIFF_HEAD_DIM
    lam = (jnp.exp(jnp.sum(lq1_ref[...] * lk1_ref[...], axis=-1, keepdims=True))
           - jnp.exp(jnp.sum(lq2_ref[...] * lk2_ref[...], axis=-1, keepdims=True)) + LAM_INIT)

    def finalize(qi, _):
        ot = (acc1_ref[qi, :dv] / acc1_ref[qi, dv:dv + 1]
              - lam * (acc2_ref[qi, :dv] / acc2_ref[qi, dv:dv + 1]))
        ms = jnp.mean(ot * ot, axis=0, keepdims=True)
        ot = ot * lax.rsqrt(ms + SUBLN_EPS) * sg_ref[...] * (1.0 - LAM_INIT)
        o_ref[_tile_rows(qi), :] = (ot.T * dg_ref[_tile_rows(qi), :]).astype(_BF16)
        return 0

    lax.fori_loop(0, n_q, finalize, 0)


def _moba_kernel(qtab_ref, ttab_ref, q_ref, k_ref, vt_ref, km_ref, mg_ref, o_ref,
                 acca_ref, accb_ref, ma_ref, mb_ref, sa_ref, sb_ref, bias_ref, qs_ref):
    n_q = acca_ref.shape[0]
    nb = km_ref.shape[0]
    per_tile = ATT_TILE // MOBA_BLOCK
    heads = ((sa_ref, acca_ref, ma_ref), (sb_ref, accb_ref, mb_ref))
    km_heads = _split_heads(km_ref[...])
    blk = lax.broadcasted_iota(jnp.int32, (nb, ATT_TILE), 0)
    col = lax.broadcasted_iota(jnp.int32, (nb, ATT_TILE), 1)
    col_blk = lax.shift_right_logical(col, MOBA_BLOCK.bit_length() - 1)

    def gate(qi, _):
        q = q_ref[_tile_rows(qi), :]
        own = per_tile * qi + col_blk
        for hh, kmh in enumerate(km_heads):
            g = lax.dot_general(kmh, q, _NT, preferred_element_type=_F32,
                                precision=lax.Precision.HIGHEST)
            g = jnp.where(blk < own, g, -jnp.inf)
            sel = jnp.zeros(g.shape, jnp.bool_)
            for _ in range(MOBA_TOPK):
                mx = jnp.max(g, axis=0, keepdims=True)
                first_idx = jnp.min(jnp.where(g == mx, blk, nb), axis=0, keepdims=True)
                pick = (blk == first_idx) & (mx > -jnp.inf)
                sel = sel | pick
                g = jnp.where(pick, -jnp.inf, g)
            bias_ref[hh, qi] = jnp.where(sel | (blk == own), 0.0, -jnp.inf)
        for hh, qh in enumerate(_split_heads((q * _SCORE_SCALE).astype(_BF16))):
            qs_ref[hh, _tile_rows(qi), :] = qh
        return 0

    lax.fori_loop(0, n_q, gate, 0)

    def scores(qi, tile, slot, diagonal):
        k = k_ref[_tile_rows(tile), :]
        for hh, (s_ref, _, _) in enumerate(heads):
            s = lax.dot_general(k, qs_ref[hh, _tile_rows(qi), :], _NT, preferred_element_type=_F32)
            s_ref[slot] = _causal_mask(s) if diagonal else s

    def update(qi, tile, slot, first):
        for hh, (s_ref, acc_ref, m_ref) in enumerate(heads):
            vth = vt_ref[0, tile, MOBA_V_ROWS * hh:MOBA_V_ROWS * (hh + 1), :]
            segs = [(s_ref[slot, MOBA_BLOCK * j:MOBA_BLOCK * (j + 1), :],
                     bias_ref[hh, qi, pl.ds(per_tile * tile + j, 1), :]) for j in range(per_tile)]
            m_ref[qi] = _flash_update(segs, None if first else m_ref[qi], acc_ref.at[qi], vth)

    _run_attention(n_q, qtab_ref, ttab_ref, scores, update)

    dh = MOBA_HEAD_DIM

    def finalize(qi, _):
        ot = jnp.concatenate([acca_ref[qi, :dh] / acca_ref[qi, dh:dh + 1],
                              accb_ref[qi, :dh] / accb_ref[qi, dh:dh + 1]], axis=0)
        o_ref[_tile_rows(qi), :] = (ot.T * mg_ref[_tile_rows(qi), :]).astype(_BF16)
        return 0

    lax.fori_loop(0, n_q, finalize, 0)


def _out_kernel(oa_ref, ob_ref, gates_ref, x_ref, p_ref, wbd_ref, wbm_ref, wout_ref, wpg_ref,
                wple_ref, fg_ref, o_ref):
    ya = jnp.dot(oa_ref[...], wbd_ref[...], preferred_element_type=_F32)
    yb = jnp.dot(ob_ref[...], wbm_ref[...], preferred_element_type=_F32)
    merged = gates_ref[:, :D_MODEL] * ya + gates_ref[:, D_MODEL:] * yb
    x1 = x_ref[...] + jnp.dot(merged.astype(_BF16), wout_ref[...], preferred_element_type=_F32)
    t = jnp.dot(x1.astype(_BF16), wpg_ref[...], preferred_element_type=_F32)
    pe = jnp.dot(p_ref[...].astype(_BF16), wple_ref[...], preferred_element_type=_F32)
    x2 = x1 + jax.nn.sigmoid(t) * pe
    ms = jnp.mean(x2 * x2, axis=-1, keepdims=True)
    o_ref[...] = x2 * lax.rsqrt(ms + EPS) * fg_ref[...]


def _rope_lane_tables(seq):
    half = ROT_DIM // 2
    inv = ROPE_THETA ** (-jnp.arange(0, ROT_DIM, 2, dtype=_F32) / ROT_DIM)
    ang = jnp.arange(seq, dtype=_F32)[:, None] * inv[None, :]
    cos, sin = jnp.cos(ang), jnp.sin(ang)
    pad = MOBA_HEAD_DIM - ROT_DIM
    one, zero = jnp.ones((seq, pad), _F32), jnp.zeros((seq, pad), _F32)
    zh = jnp.zeros((seq, half), _F32)
    cos64 = jnp.concatenate([cos, cos, one], axis=1)
    slo64 = jnp.concatenate([-sin, zh, zero], axis=1)
    shi64 = jnp.concatenate([zh, sin, zero], axis=1)
    return tuple(jnp.tile(t, (1, LANES // MOBA_HEAD_DIM)) for t in (cos64, slo64, shi64))


def _resident(shape):
    return pl.BlockSpec(shape, lambda *_: (0,) * len(shape), pipeline_mode=pl.Buffered(1))


def _params(n_axes):
    return pltpu.CompilerParams(dimension_semantics=("arbitrary",) * n_axes,
                                vmem_limit_bytes=VMEM_LIMIT)


def kernel(x, p, norm_g, w_in, lambda_q1, lambda_k1, lambda_q2, lambda_k2, subln_g,
           w_branch_diff, w_branch_moba, w_out, w_ple, w_ple_gate, final_g):
    B, S, _ = x.shape
    assert w_in.shape == (1, D_MODEL, N_CHUNKS * CHUNK) and S % (2 * ATT_TILE) == 0
    rows = B * S
    n_row_tiles = rows // ROW_TILE
    n_kv = S // ATT_TILE
    n_blk = S // MOBA_BLOCK
    x2d = x.reshape(rows, D_MODEL)
    cosv, sin_lo, sin_hi = _rope_lane_tables(S)

    row_spec = lambda w: pl.BlockSpec((ROW_TILE, w), lambda i: (i, 0))
    tab_spec = pl.BlockSpec((ROW_TILE, LANES), lambda i: (i % n_kv, 0))
    dvt_rows, mvt_rows = DIFF_HEADS * DIFF_V_ROWS, MOBA_HEADS * MOBA_V_ROWS
    vt_spec = lambda r: pl.BlockSpec((1, 1, r, ATT_TILE), lambda i: (i // n_kv, i % n_kv, 0, 0))
    vt_shape = lambda r: jax.ShapeDtypeStruct((B, n_kv, r, ATT_TILE), _BF16)
    rows_bf16 = jax.ShapeDtypeStruct((rows, CHUNK), _BF16)
    rows_f32 = jax.ShapeDtypeStruct((rows, CHUNK), _F32)
    km_rows = ROW_TILE // MOBA_BLOCK
    (dq, dk, dvt, dgs, mq, mk, mvt, mgs, kmean, gates) = pl.pallas_call(
        _proj_kernel,
        grid=(n_row_tiles,),
        in_specs=[row_spec(D_MODEL), _resident((1, D_MODEL)), _resident((D_MODEL, N_CHUNKS * CHUNK)),
                  tab_spec, tab_spec, tab_spec],
        out_specs=[row_spec(CHUNK), row_spec(CHUNK), vt_spec(dvt_rows), row_spec(CHUNK), row_spec(CHUNK),
                   row_spec(CHUNK), vt_spec(mvt_rows), row_spec(CHUNK),
                   pl.BlockSpec((1, km_rows, CHUNK), lambda i: (i, 0, 0)),
                   row_spec(4 * CHUNK)],
        out_shape=[rows_bf16, rows_bf16, vt_shape(dvt_rows), rows_f32, rows_f32, rows_bf16,
                   vt_shape(mvt_rows), rows_f32,
                   jax.ShapeDtypeStruct((n_row_tiles, km_rows, CHUNK), _F32),
                   jax.ShapeDtypeStruct((rows, 4 * CHUNK), _F32)],
        compiler_params=_params(1),
        name="proj",
    )(x2d, norm_g[0].reshape(1, D_MODEL), w_in[0].astype(_BF16), cosv, sin_lo, sin_hi)
    kmean = kmean.reshape(B * n_blk, CHUNK)

    n_groups = CHUNK // LANES
    qtab, ttab = _past_items(n_kv)
    seq_spec = pl.BlockSpec((S, LANES), lambda b, g, *_: (b, g))
    v_spec = lambda r: pl.BlockSpec((1, n_kv, r, ATT_TILE), lambda b, g, *_: (b, 0, g, 0))
    lam_spec = pl.BlockSpec((1, DIFF_HEAD_DIM), lambda b, g, *_: (0, 0))
    acc = lambda r: pltpu.VMEM((n_kv, r, ATT_TILE), _F32)
    run_max = pltpu.VMEM((n_kv, 1, ATT_TILE), _F32)
    score_buf = pltpu.VMEM((2, ATT_TILE, ATT_TILE), _F32)

    o_a = pl.pallas_call(
        _diff_kernel,
        grid_spec=pltpu.PrefetchScalarGridSpec(
            num_scalar_prefetch=2,
            grid=(B, n_groups),
            in_specs=[lam_spec, lam_spec, lam_spec, lam_spec, seq_spec, seq_spec, v_spec(DIFF_V_ROWS),
                      seq_spec, pl.BlockSpec((2 * DIFF_HEAD_DIM, 1), lambda b, g, *_: (0, 0))],
            out_specs=seq_spec,
            scratch_shapes=[acc(DIFF_V_ROWS), acc(DIFF_V_ROWS), run_max, run_max, score_buf, score_buf],
        ),
        out_shape=rows_bf16,
        compiler_params=_params(2),
        name="diff_attn",
    )(qtab, ttab, lambda_q1[0:1], lambda_k1[0:1], lambda_q2[0:1], lambda_k2[0:1], dq, dk, dvt, dgs,
      subln_g[0].reshape(2 * DIFF_HEAD_DIM, 1))

    o_b = pl.pallas_call(
        _moba_kernel,
        grid_spec=pltpu.PrefetchScalarGridSpec(
            num_scalar_prefetch=2,
            grid=(B, n_groups),
            in_specs=[seq_spec, seq_spec, v_spec(2 * MOBA_V_ROWS),
                      pl.BlockSpec((n_blk, LANES), lambda b, g, *_: (b, g)), seq_spec],
            out_specs=seq_spec,
            scratch_shapes=[acc(MOBA_V_ROWS), acc(MOBA_V_ROWS), run_max, run_max, score_buf, score_buf,
                            pltpu.VMEM((2, n_kv, n_blk, ATT_TILE), _F32),
                            pltpu.VMEM((2, S, LANES), _BF16)],
        ),
        out_shape=rows_bf16,
        compiler_params=_params(2),
        name="moba_attn",
    )(qtab, ttab, mq, mk, mvt, kmean, mgs)

    out = pl.pallas_call(
        _out_kernel,
        grid=(n_row_tiles,),
        in_specs=[row_spec(CHUNK), row_spec(CHUNK), row_spec(4 * CHUNK), row_spec(D_MODEL),
                  row_spec(PLE_DIM), _resident((DIFF_WIDTH, D_MODEL)), _resident((MOBA_WIDTH, D_MODEL)),
                  _resident((D_MODEL, D_MODEL)), _resident((D_MODEL, D_MODEL)),
                  _resident((PLE_DIM, D_MODEL)), _resident((1, D_MODEL))],
        out_specs=row_spec(D_MODEL),
        out_shape=jax.ShapeDtypeStruct((rows, D_MODEL), _F32),
        compiler_params=_params(1),
        name="out_proj",
    )(o_a, o_b, gates, x2d, p[0].reshape(rows, PLE_DIM), w_branch_diff[0].astype(_BF16),
      w_branch_moba[0].astype(_BF16), w_out[0].astype(_BF16), w_ple_gate[0].astype(_BF16),
      w_ple[0].astype(_BF16), final_g.reshape(1, D_MODEL))
    return out.reshape(B, S, D_MODEL)
```

```python
import functools
import math

import jax
import jax.numpy as jnp
import numpy as np
from jax import lax
from jax.experimental import pallas as pl
from jax.experimental.pallas import tpu as pltpu

D_MODEL = 1024
PLE_DIM = 256
DIFF_HEADS = 4
DIFF_HEAD_DIM = 64
DIFF_WIDTH = DIFF_HEADS * 2 * DIFF_HEAD_DIM
MOBA_HEADS = 8
MOBA_HEAD_DIM = 64
MOBA_WIDTH = MOBA_HEADS * MOBA_HEAD_DIM
MOBA_BLOCK = 256
MOBA_TOPK = 3
ROT_DIM = 16
ROPE_THETA = 500000.0
EPS = 1e-6
SUBLN_EPS = 1e-5
LAM_INIT = 0.8 - 0.6 * math.exp(-0.3 * 0)

LANES = 128
BF16_ROWS = 16
CHUNK = 512
N_CHUNKS = 12
ROW_TILE = 512
ATT_TILE = 512
PIPE_UNROLL = 8
VMEM_LIMIT = 56 * 1024 * 1024
_SCORE_SCALE = DIFF_HEAD_DIM ** -0.5 * math.log2(math.e)
DIFF_V_ROWS = 2 * DIFF_HEAD_DIM + BF16_ROWS
MOBA_V_ROWS = MOBA_HEAD_DIM + BF16_ROWS
assert DIFF_HEAD_DIM == MOBA_HEAD_DIM and ATT_TILE % MOBA_BLOCK == 0 and ROW_TILE == ATT_TILE

_F32 = jnp.float32
_BF16 = jnp.bfloat16
_NT = (((1,), (1,)), ((), ()))


def _rope_rows(a, cosv, sin_lo, sin_hi):
    outs = []
    for j in range(CHUNK // LANES):
        xs = a[:, LANES * j:LANES * (j + 1)]
        up = pltpu.roll(xs, LANES - ROT_DIM // 2, 1)
        dn = pltpu.roll(xs, ROT_DIM // 2, 1)
        outs.append(xs * cosv + up * sin_lo + dn * sin_hi)
    return jnp.concatenate(outs, axis=1)


def _proj_kernel(x_ref, g_ref, w_ref, cos_ref, slo_ref, shi_ref,
                 dq_ref, dk_ref, dvt_ref, dg_ref, mq_ref, mk_ref, mvt_ref, mg_ref,
                 km_ref, gates_ref):
    x = x_ref[...]
    ms = jnp.mean(x * x, axis=-1, keepdims=True)
    h = (x * lax.rsqrt(ms + EPS) * g_ref[...]).astype(_BF16)
    cosv, sin_lo, sin_hi = cos_ref[...], slo_ref[...], shi_ref[...]

    def proj(c):
        return jnp.dot(h, w_ref[:, CHUNK * c:CHUNK * (c + 1)], preferred_element_type=_F32)

    def store_transposed(ref, a, head_rows):
        at = a.T.astype(_BF16)
        ones = jnp.ones((BF16_ROWS, ROW_TILE), _BF16)
        stride = head_rows + BF16_ROWS
        for g in range(CHUNK // head_rows):
            ref[0, 0, stride * g:stride * g + head_rows, :] = at[head_rows * g:head_rows * (g + 1)]
            ref[0, 0, stride * g + head_rows:stride * (g + 1), :] = ones

    dq_ref[...] = (_rope_rows(proj(0), cosv, sin_lo, sin_hi) * _SCORE_SCALE).astype(_BF16)
    dk_ref[...] = _rope_rows(proj(1), cosv, sin_lo, sin_hi).astype(_BF16)
    store_transposed(dvt_ref, proj(2), 2 * DIFF_HEAD_DIM)
    dg_ref[...] = jax.nn.silu(proj(3))
    mq_ref[...] = _rope_rows(proj(4), cosv, sin_lo, sin_hi)
    mk = _rope_rows(proj(5), cosv, sin_lo, sin_hi)
    mk_ref[...] = mk.astype(_BF16)
    for j in range(ROW_TILE // MOBA_BLOCK):
        km_ref[0, j:j + 1, :] = jnp.mean(mk[MOBA_BLOCK * j:MOBA_BLOCK * (j + 1)], axis=0, keepdims=True)
    store_transposed(mvt_ref, proj(6), MOBA_HEAD_DIM)
    mg_ref[...] = jax.nn.silu(proj(7))
    for c in range(8, N_CHUNKS):
        gates_ref[:, CHUNK * (c - 8):CHUNK * (c - 7)] = jax.nn.sigmoid(proj(c))


def _flash_update(segs, m, acc_ref, vt):
    cands = [] if m is None else [m]
    cands += [cm if bias is None else cm + bias for _, cm, bias in segs]
    m_new = functools.reduce(jnp.maximum, cands)
    pb = [jnp.exp2(s - (m_new if bias is None else m_new - bias)).astype(_BF16) for s, _, bias in segs]
    pv = jnp.dot(vt, pb[0] if len(pb) == 1 else jnp.concatenate(pb, axis=0),
                 preferred_element_type=_F32)
    acc_ref[...] = pv if m is None else jnp.exp2(m - m_new) * acc_ref[...] + pv
    return m_new


def _causal_mask(s):
    kpos = lax.broadcasted_iota(jnp.int32, s.shape, 0)
    qpos = lax.broadcasted_iota(jnp.int32, s.shape, 1)
    return jnp.where(kpos <= qpos, s, -jnp.inf)


def _split_heads(q):
    lane = lax.broadcasted_iota(jnp.int32, q.shape, 1)
    zero = jnp.zeros_like(q)
    return jnp.where(lane < 64, q, zero), jnp.where(lane >= 64, q, zero)


def _tile_rows(i):
    return pl.ds(pl.multiple_of(i * ATT_TILE, ATT_TILE), ATT_TILE)


def _past_items(n_q):
    items = [(qi, t) for qi in range(1, n_q) for t in range(qi)]
    assert len(items) % 2 == 0
    items.append(items[-1])
    return (np.array([i[0] for i in items], np.int32), np.array([i[1] for i in items], np.int32))


def _run_attention(n_q, qtab_ref, ttab_ref, scores, update):
    def diag_pair(j, _):
        scores(2 * j, 2 * j, 0, True)
        scores(2 * j + 1, 2 * j + 1, 1, True)
        update(2 * j, 2 * j, 0, True)
        update(2 * j + 1, 2 * j + 1, 1, True)
        return 0

    lax.fori_loop(0, n_q // 2, diag_pair, 0)

    scores(qtab_ref[0], ttab_ref[0], 0, False)

    def step(j, _):
        for u in range(PIPE_UNROLL):
            t = PIPE_UNROLL * j + u
            scores(qtab_ref[t + 1], ttab_ref[t + 1], (u + 1) % 2, False)
            update(qtab_ref[t], ttab_ref[t], u % 2, False)
        return 0

    n_items = n_q * (n_q - 1) // 2
    assert n_items % PIPE_UNROLL == 0 and PIPE_UNROLL % 2 == 0
    lax.fori_loop(0, n_items // PIPE_UNROLL, step, 0)


def _diff_kernel(qtab_ref, ttab_ref, lq1_ref, lk1_ref, lq2_ref, lk2_ref, q_ref, k_ref, vt_ref,
                 dg_ref, sg_ref, o_ref, acc1_ref, acc2_ref, m1_ref, m2_ref, s1_ref, s2_ref, cm_ref):
    n_q = acc1_ref.shape[0]
    maps = ((s1_ref, acc1_ref, m1_ref), (s2_ref, acc2_ref, m2_ref))

    def scores(qi, tile, slot, diagonal):
        k = k_ref[_tile_rows(tile), :]
        for mi, qm in enumerate(_split_heads(q_ref[_tile_rows(qi), :])):
            s = lax.dot_general(k, qm, _NT, preferred_element_type=_F32)
            s = _causal_mask(s) if diagonal else s
            maps[mi][0][slot] = s
            cm_ref[slot, mi] = jnp.max(s, axis=0, keepdims=True)

    def update(qi, tile, slot, first):
        vt = vt_ref[0, tile]
        for mi, (s_ref, acc_ref, m_ref) in enumerate(maps):
            m_ref[qi] = _flash_update([(s_ref[slot], cm_ref[slot, mi], None)],
                                      None if first else m_ref[qi], acc_ref.at[qi], vt)

    _run_attention(n_q, qtab_ref, ttab_ref, scores, update)

    dv = 2 * DIFF_HEAD_DIM
    lam = (jnp.exp(jnp.sum(lq1_ref[...] * lk1_ref[...], axis=-1, keepdims=True))
           - jnp.exp(jnp.sum(lq2_ref[...] * lk2_ref[...], axis=-1, keepdims=True)) + LAM_INIT)

    def finalize(qi, _):
        ot = (acc1_ref[qi, :dv] / acc1_ref[qi, dv:dv + 1]
              - lam * (acc2_ref[qi, :dv] / acc2_ref[qi, dv:dv + 1]))
        ms = jnp.mean(ot * ot, axis=0, keepdims=True)
        ot = ot * lax.rsqrt(ms + SUBLN_EPS) * sg_ref[...] * (1.0 - LAM_INIT)
        o_ref[_tile_rows(qi), :] = (ot.T * dg_ref[_tile_rows(qi), :]).astype(_BF16)
        return 0

    lax.fori_loop(0, n_q, finalize, 0)


def _moba_kernel(qtab_ref, ttab_ref, q_ref, k_ref, vt_ref, km_ref, mg_ref, o_ref,
                 acca_ref, accb_ref, ma_ref, mb_ref, sa_ref, sb_ref, cm_ref, bias_ref, qs_ref):
    n_q = acca_ref.shape[0]
    nb = km_ref.shape[0]
    per_tile = ATT_TILE // MOBA_BLOCK
    heads = ((sa_ref, acca_ref, ma_ref), (sb_ref, accb_ref, mb_ref))
    km_heads = _split_heads(km_ref[...])
    blk = lax.broadcasted_iota(jnp.int32, (nb, ATT_TILE), 0)
    col = lax.broadcasted_iota(jnp.int32, (nb, ATT_TILE), 1)
    col_blk = lax.shift_right_logical(col, MOBA_BLOCK.bit_length() - 1)

    def gate(qi, _):
        q = q_ref[_tile_rows(qi), :]
        own = per_tile * qi + col_blk
        for hh, kmh in enumerate(km_heads):
            g = lax.dot_general(kmh, q, _NT, preferred_element_type=_F32,
                                precision=lax.Precision.HIGHEST)
            g = jnp.where(blk < own, g, -jnp.inf)
            sel = jnp.zeros(g.shape, jnp.bool_)
            for _ in range(MOBA_TOPK):
                mx = jnp.max(g, axis=0, keepdims=True)
                first_idx = jnp.min(jnp.where(g == mx, blk, nb), axis=0, keepdims=True)
                pick = (blk == first_idx) & (mx > -jnp.inf)
                sel = sel | pick
                g = jnp.where(pick, -jnp.inf, g)
            bias_ref[hh, qi] = jnp.where(sel | (blk == own), 0.0, -jnp.inf)
        for hh, qh in enumerate(_split_heads((q * _SCORE_SCALE).astype(_BF16))):
            qs_ref[hh, _tile_rows(qi), :] = qh
        return 0

    lax.fori_loop(0, n_q // 2, lambda j, c: gate(2 * j + 1, gate(2 * j, c)), 0)

    def scores(qi, tile, slot, diagonal):
        k = k_ref[_tile_rows(tile), :]
        for hh, (s_ref, _, _) in enumerate(heads):
            s = lax.dot_general(k, qs_ref[hh, _tile_rows(qi), :], _NT, preferred_element_type=_F32)
            s = _causal_mask(s) if diagonal else s
            s_ref[slot] = s
            for j in range(per_tile):
                cm_ref[slot, per_tile * hh + j] = jnp.max(s[MOBA_BLOCK * j:MOBA_BLOCK * (j + 1)],
                                                          axis=0, keepdims=True)

    def update(qi, tile, slot, first):
        for hh, (s_ref, acc_ref, m_ref) in enumerate(heads):
            vth = vt_ref[0, tile, MOBA_V_ROWS * hh:MOBA_V_ROWS * (hh + 1), :]
            segs = [(s_ref[slot, MOBA_BLOCK * j:MOBA_BLOCK * (j + 1), :],
                     cm_ref[slot, per_tile * hh + j],
                     bias_ref[hh, qi, pl.ds(per_tile * tile + j, 1), :]) for j in range(per_tile)]
            m_ref[qi] = _flash_update(segs, None if first else m_ref[qi], acc_ref.at[qi], vth)

    _run_attention(n_q, qtab_ref, ttab_ref, scores, update)

    dh = MOBA_HEAD_DIM

    def finalize(qi, _):
        ot = jnp.concatenate([acca_ref[qi, :dh] / acca_ref[qi, dh:dh + 1],
                              accb_ref[qi, :dh] / accb_ref[qi, dh:dh + 1]], axis=0)
        o_ref[_tile_rows(qi), :] = (ot.T * mg_ref[_tile_rows(qi), :]).astype(_BF16)
        return 0

    lax.fori_loop(0, n_q, finalize, 0)


def _out_kernel(oa_ref, ob_ref, gates_ref, x_ref, p_ref, wbd_ref, wbm_ref, wout_ref, wpg_ref,
                wple_ref, fg_ref, o_ref):
    ya = jnp.dot(oa_ref[...], wbd_ref[...], preferred_element_type=_F32)
    yb = jnp.dot(ob_ref[...], wbm_ref[...], preferred_element_type=_F32)
    merged = gates_ref[:, :D_MODEL] * ya + gates_ref[:, D_MODEL:] * yb
    x1 = x_ref[...] + jnp.dot(merged.astype(_BF16), wout_ref[...], preferred_element_type=_F32)
    t = jnp.dot(x1.astype(_BF16), wpg_ref[...], preferred_element_type=_F32)
    pe = jnp.dot(p_ref[...].astype(_BF16), wple_ref[...], preferred_element_type=_F32)
    x2 = x1 + jax.nn.sigmoid(t) * pe
    ms = jnp.mean(x2 * x2, axis=-1, keepdims=True)
    o_ref[...] = x2 * lax.rsqrt(ms + EPS) * fg_ref[...]


def _rope_lane_tables(seq):
    half = ROT_DIM // 2
    inv = ROPE_THETA ** (-jnp.arange(0, ROT_DIM, 2, dtype=_F32) / ROT_DIM)
    ang = jnp.arange(seq, dtype=_F32)[:, None] * inv[None, :]
    cos, sin = jnp.cos(ang), jnp.sin(ang)
    pad = MOBA_HEAD_DIM - ROT_DIM
    one, zero = jnp.ones((seq, pad), _F32), jnp.zeros((seq, pad), _F32)
    zh = jnp.zeros((seq, half), _F32)
    cos64 = jnp.concatenate([cos, cos, one], axis=1)
    slo64 = jnp.concatenate([-sin, zh, zero], axis=1)
    shi64 = jnp.concatenate([zh, sin, zero], axis=1)
    return tuple(jnp.tile(t, (1, LANES // MOBA_HEAD_DIM)) for t in (cos64, slo64, shi64))


def _resident(shape):
    return pl.BlockSpec(shape, lambda *_: (0,) * len(shape), pipeline_mode=pl.Buffered(1))


def _params(n_axes):
    return pltpu.CompilerParams(dimension_semantics=("arbitrary",) * n_axes,
                                vmem_limit_bytes=VMEM_LIMIT)


def kernel(x, p, norm_g, w_in, lambda_q1, lambda_k1, lambda_q2, lambda_k2, subln_g,
           w_branch_diff, w_branch_moba, w_out, w_ple, w_ple_gate, final_g):
    B, S, _ = x.shape
    assert w_in.shape == (1, D_MODEL, N_CHUNKS * CHUNK) and S % (2 * ATT_TILE) == 0
    rows = B * S
    n_row_tiles = rows // ROW_TILE
    n_kv = S // ATT_TILE
    n_blk = S // MOBA_BLOCK
    x2d = x.reshape(rows, D_MODEL)
    cosv, sin_lo, sin_hi = _rope_lane_tables(S)

    row_spec = lambda w: pl.BlockSpec((ROW_TILE, w), lambda i: (i, 0))
    tab_spec = pl.BlockSpec((ROW_TILE, LANES), lambda i: (i % n_kv, 0))
    dvt_rows, mvt_rows = DIFF_HEADS * DIFF_V_ROWS, MOBA_HEADS * MOBA_V_ROWS
    vt_spec = lambda r: pl.BlockSpec((1, 1, r, ATT_TILE), lambda i: (i // n_kv, i % n_kv, 0, 0))
    vt_shape = lambda r: jax.ShapeDtypeStruct((B, n_kv, r, ATT_TILE), _BF16)
    rows_bf16 = jax.ShapeDtypeStruct((rows, CHUNK), _BF16)
    rows_f32 = jax.ShapeDtypeStruct((rows, CHUNK), _F32)
    km_rows = ROW_TILE // MOBA_BLOCK
    (dq, dk, dvt, dgs, mq, mk, mvt, mgs, kmean, gates) = pl.pallas_call(
        _proj_kernel,
        grid=(n_row_tiles,),
        in_specs=[row_spec(D_MODEL), _resident((1, D_MODEL)), _resident((D_MODEL, N_CHUNKS * CHUNK)),
                  tab_spec, tab_spec, tab_spec],
        out_specs=[row_spec(CHUNK), row_spec(CHUNK), vt_spec(dvt_rows), row_spec(CHUNK), row_spec(CHUNK),
                   row_spec(CHUNK), vt_spec(mvt_rows), row_spec(CHUNK),
                   pl.BlockSpec((1, km_rows, CHUNK), lambda i: (i, 0, 0)),
                   row_spec(4 * CHUNK)],
        out_shape=[rows_bf16, rows_bf16, vt_shape(dvt_rows), rows_f32, rows_f32, rows_bf16,
                   vt_shape(mvt_rows), rows_f32,
                   jax.ShapeDtypeStruct((n_row_tiles, km_rows, CHUNK), _F32),
                   jax.ShapeDtypeStruct((rows, 4 * CHUNK), _F32)],
        compiler_params=_params(1),
        name="proj",
    )(x2d, norm_g[0].reshape(1, D_MODEL), w_in[0].astype(_BF16), cosv, sin_lo, sin_hi)
    kmean = kmean.reshape(B * n_blk, CHUNK)

    n_groups = CHUNK // LANES
    qtab, ttab = _past_items(n_kv)
    seq_spec = pl.BlockSpec((S, LANES), lambda b, g, *_: (b, g))
    v_spec = lambda r: pl.BlockSpec((1, n_kv, r, ATT_TILE), lambda b, g, *_: (b, 0, g, 0))
    lam_spec = pl.BlockSpec((1, DIFF_HEAD_DIM), lambda b, g, *_: (0, 0))
    acc = lambda r: pltpu.VMEM((n_kv, r, ATT_TILE), _F32)
    run_max = pltpu.VMEM((n_kv, 1, ATT_TILE), _F32)
    score_buf = pltpu.VMEM((2, ATT_TILE, ATT_TILE), _F32)
    col_max = lambda n: pltpu.VMEM((2, n, 1, ATT_TILE), _F32)

    o_a = pl.pallas_call(
        _diff_kernel,
        grid_spec=pltpu.PrefetchScalarGridSpec(
            num_scalar_prefetch=2,
            grid=(B, n_groups),
            in_specs=[lam_spec, lam_spec, lam_spec, lam_spec, seq_spec, seq_spec, v_spec(DIFF_V_ROWS),
                      seq_spec, pl.BlockSpec((2 * DIFF_HEAD_DIM, 1), lambda b, g, *_: (0, 0))],
            out_specs=seq_spec,
            scratch_shapes=[acc(DIFF_V_ROWS), acc(DIFF_V_ROWS), run_max, run_max, score_buf, score_buf,
                            col_max(2)],
        ),
        out_shape=rows_bf16,
        compiler_params=_params(2),
        name="diff_attn",
    )(qtab, ttab, lambda_q1[0:1], lambda_k1[0:1], lambda_q2[0:1], lambda_k2[0:1], dq, dk, dvt, dgs,
      subln_g[0].reshape(2 * DIFF_HEAD_DIM, 1))

    o_b = pl.pallas_call(
        _moba_kernel,
        grid_spec=pltpu.PrefetchScalarGridSpec(
            num_scalar_prefetch=2,
            grid=(B, n_groups),
            in_specs=[seq_spec, seq_spec, v_spec(2 * MOBA_V_ROWS),
                      pl.BlockSpec((n_blk, LANES), lambda b, g, *_: (b, g)), seq_spec],
            out_specs=seq_spec,
            scratch_shapes=[acc(MOBA_V_ROWS), acc(MOBA_V_ROWS), run_max, run_max, score_buf, score_buf,
                            col_max(2 * (ATT_TILE // MOBA_BLOCK)),
                            pltpu.VMEM((2, n_kv, n_blk, ATT_TILE), _F32),
                            pltpu.VMEM((2, S, LANES), _BF16)],
        ),
        out_shape=rows_bf16,
        compiler_params=_params(2),
        name="moba_attn",
    )(qtab, ttab, mq, mk, mvt, kmean, mgs)

    out = pl.pallas_call(
        _out_kernel,
        grid=(n_row_tiles,),
        in_specs=[row_spec(CHUNK), row_spec(CHUNK), row_spec(4 * CHUNK), row_spec(D_MODEL),
                  row_spec(PLE_DIM), _resident((DIFF_WIDTH, D_MODEL)), _resident((MOBA_WIDTH, D_MODEL)),
                  _resident((D_MODEL, D_MODEL)), _resident((D_MODEL, D_MODEL)),
                  _resident((PLE_DIM, D_MODEL)), _resident((1, D_MODEL))],
        out_specs=row_spec(D_MODEL),
        out_shape=jax.ShapeDtypeStruct((rows, D_MODEL), _F32),
        compiler_params=_params(1),
        name="out_proj",
    )(o_a, o_b, gates, x2d, p[0].reshape(rows, PLE_DIM), w_branch_diff[0].astype(_BF16),
      w_branch_moba[0].astype(_BF16), w_out[0].astype(_BF16), w_ple_gate[0].astype(_BF16),
      w_ple[0].astype(_BF16), final_g.reshape(1, D_MODEL))
    return out.reshape(B, S, D_MODEL)
```

```python
import collections
import functools
import math

import jax
import jax.numpy as jnp
import numpy as np
from jax import lax
from jax.experimental import pallas as pl
from jax.experimental.pallas import tpu as pltpu

D_MODEL = 1024
PLE_DIM = 256
DIFF_HEADS = 4
DIFF_HEAD_DIM = 64
DIFF_WIDTH = DIFF_HEADS * 2 * DIFF_HEAD_DIM
MOBA_HEADS = 8
MOBA_HEAD_DIM = 64
MOBA_WIDTH = MOBA_HEADS * MOBA_HEAD_DIM
MOBA_BLOCK = 256
MOBA_TOPK = 3
ROT_DIM = 16
ROPE_THETA = 500000.0
EPS = 1e-6
SUBLN_EPS = 1e-5
LAM_INIT = 0.8 - 0.6 * math.exp(-0.3 * 0)

LANES = 128
MXU_COLS = 256
BF16_ROWS = 16
CHUNK = 512
N_CHUNKS = 12
ROW_TILE = 512
ATT_TILE = 512
PIPE_UNROLL = 8
VMEM_LIMIT = 56 * 1024 * 1024
_SCORE_SCALE = DIFF_HEAD_DIM ** -0.5 * math.log2(math.e)
DIFF_V_ROWS = 2 * DIFF_HEAD_DIM + BF16_ROWS
MOBA_V_ROWS = MOBA_HEAD_DIM + BF16_ROWS
assert DIFF_HEAD_DIM == MOBA_HEAD_DIM and ATT_TILE % MOBA_BLOCK == 0 and ROW_TILE == ATT_TILE

_F32 = jnp.float32
_BF16 = jnp.bfloat16
_NT = (((1,), (1,)), ((), ()))


def _rope_rows(a, cosv, sin_lo, sin_hi):
    outs = []
    for j in range(CHUNK // LANES):
        xs = a[:, LANES * j:LANES * (j + 1)]
        up = pltpu.roll(xs, LANES - ROT_DIM // 2, 1)
        dn = pltpu.roll(xs, ROT_DIM // 2, 1)
        outs.append(xs * cosv + up * sin_lo + dn * sin_hi)
    return jnp.concatenate(outs, axis=1)


def _proj_kernel(x_ref, g_ref, w_ref, cos_ref, slo_ref, shi_ref,
                 dq_ref, dk_ref, dvt_ref, dg_ref, mq_ref, mk_ref, mvt_ref, mg_ref,
                 km_ref, gates_ref):
    x = x_ref[...]
    ms = jnp.mean(x * x, axis=-1, keepdims=True)
    h = (x * lax.rsqrt(ms + EPS) * g_ref[...]).astype(_BF16)
    cosv, sin_lo, sin_hi = cos_ref[...], slo_ref[...], shi_ref[...]

    def proj(c):
        return jnp.dot(h, w_ref[:, CHUNK * c:CHUNK * (c + 1)], preferred_element_type=_F32)

    def store_transposed(ref, a, head_rows):
        at = a.T.astype(_BF16)
        ones = jnp.ones((BF16_ROWS, ROW_TILE), _BF16)
        stride = head_rows + BF16_ROWS
        for g in range(CHUNK // head_rows):
            ref[0, 0, stride * g:stride * g + head_rows, :] = at[head_rows * g:head_rows * (g + 1)]
            ref[0, 0, stride * g + head_rows:stride * (g + 1), :] = ones

    dq_ref[...] = (_rope_rows(proj(0), cosv, sin_lo, sin_hi) * _SCORE_SCALE).astype(_BF16)
    dk_ref[...] = _rope_rows(proj(1), cosv, sin_lo, sin_hi).astype(_BF16)
    store_transposed(dvt_ref, proj(2), 2 * DIFF_HEAD_DIM)
    dg_ref[...] = jax.nn.silu(proj(3))
    mq_ref[...] = _rope_rows(proj(4), cosv, sin_lo, sin_hi)
    mk = _rope_rows(proj(5), cosv, sin_lo, sin_hi)
    mk_ref[...] = mk.astype(_BF16)
    for j in range(ROW_TILE // MOBA_BLOCK):
        km_ref[0, j:j + 1, :] = jnp.mean(mk[MOBA_BLOCK * j:MOBA_BLOCK * (j + 1)], axis=0, keepdims=True)
    store_transposed(mvt_ref, proj(6), MOBA_HEAD_DIM)
    mg_ref[...] = jax.nn.silu(proj(7))
    for c in range(8, N_CHUNKS):
        gates_ref[:, CHUNK * (c - 8):CHUNK * (c - 7)] = jax.nn.sigmoid(proj(c))


def _split_heads(q):
    lane = lax.broadcasted_iota(jnp.int32, q.shape, 1)
    zero = jnp.zeros_like(q)
    return jnp.where(lane < 64, q, zero), jnp.where(lane >= 64, q, zero)


def _tile_rows(i):
    return pl.ds(pl.multiple_of(i * ATT_TILE, ATT_TILE), ATT_TILE)


def _past_items(n_q):
    items = [(qi, t) for qi in range(1, n_q) for t in range(qi)]
    assert len(items) % PIPE_UNROLL == 0 and PIPE_UNROLL % 2 == 0
    items.append(items[-1])
    return (np.array([i[0] for i in items], np.int32), np.array([i[1] for i in items], np.int32))


_Stream = collections.namedtuple("_Stream", "s_ref acc_ref m_ref")


def _run_attention(qtab_ref, ttab_ref, streams, n_seg, k_ref, cm_ref, q_of, vt_of, bias_of):
    n_q = streams[0].acc_ref.shape[0]
    seg_rows = ATT_TILE // n_seg
    parts = ATT_TILE // MXU_COLS

    def softmax(st, qi, tile, slot, first):
        stream = streams[st]
        biases = [bias_of(st, qi, tile, j) for j in range(n_seg)]
        cands = [cm_ref[slot, st * n_seg + j] if b is None else cm_ref[slot, st * n_seg + j] + b
                 for j, b in enumerate(biases)]
        m_old = None if first else stream.m_ref[qi]
        m_new = functools.reduce(jnp.maximum, cands if first else [m_old] + cands)
        stream.m_ref[qi] = m_new
        p = [jnp.exp2(stream.s_ref[slot, seg_rows * j:seg_rows * (j + 1), :]
                      - (m_new if b is None else m_new - b)).astype(_BF16)
             for j, b in enumerate(biases)]
        p = p[0] if n_seg == 1 else jnp.concatenate(p, axis=0)
        return p, (None if first else jnp.exp2(m_old - m_new))

    def qk_part(st, k, qi, slot, part, diagonal):
        cols = slice(MXU_COLS * part, MXU_COLS * (part + 1))
        s = lax.dot_general(k, q_of(st, qi, part), _NT, preferred_element_type=_F32)
        if diagonal:
            kpos = lax.broadcasted_iota(jnp.int32, s.shape, 0)
            qpos = lax.broadcasted_iota(jnp.int32, s.shape, 1) + MXU_COLS * part
            s = jnp.where(kpos <= qpos, s, -jnp.inf)
        streams[st].s_ref[slot, :, cols] = s
        for j in range(n_seg):
            cm_ref[slot, st * n_seg + j, :, cols] = jnp.max(s[seg_rows * j:seg_rows * (j + 1)],
                                                           axis=0, keepdims=True)

    def pv_part(st, vt, p, alpha, qi, part):
        cols = slice(MXU_COLS * part, MXU_COLS * (part + 1))
        acc_ref = streams[st].acc_ref
        pv = jnp.dot(vt, p[:, cols], preferred_element_type=_F32)
        acc_ref[qi, :, cols] = pv if alpha is None else alpha[:, cols] * acc_ref[qi, :, cols] + pv

    def step(nxt, cur, first=False):
        k = None if nxt is None else k_ref[_tile_rows(nxt[1]), :]
        for st in range(len(streams)):
            if cur is not None:
                p, alpha = softmax(st, cur[0], cur[1], cur[2], first)
                vt = vt_of(st, cur[1])
            for part in range(parts):
                if nxt is not None:
                    qk_part(st, k, nxt[0], nxt[2], part, nxt[3])
                if cur is not None:
                    pv_part(st, vt, p, alpha, cur[0], part)

    step((0, 0, 0, True), None)

    def diag_items(j, _):
        for u in range(2):
            t = 2 * j + u
            step((t + 1, t + 1, (u + 1) % 2, True), (t, t, u), first=True)
        return 0

    lax.fori_loop(0, n_q // 2 - 1, diag_items, 0)
    step((n_q - 1, n_q - 1, 1, True), (n_q - 2, n_q - 2, 0), first=True)
    step((qtab_ref[0], ttab_ref[0], 0, False), (n_q - 1, n_q - 1, 1), first=True)

    def past_items(j, _):
        for u in range(PIPE_UNROLL):
            t = PIPE_UNROLL * j + u
            step((qtab_ref[t + 1], ttab_ref[t + 1], (u + 1) % 2, False),
                 (qtab_ref[t], ttab_ref[t], u % 2))
        return 0

    lax.fori_loop(0, n_q * (n_q - 1) // (2 * PIPE_UNROLL), past_items, 0)


def _q_part_rows(qi, part):
    return pl.ds(pl.multiple_of(qi * ATT_TILE + MXU_COLS * part, MXU_COLS), MXU_COLS)


def _diff_kernel(qtab_ref, ttab_ref, lq1_ref, lk1_ref, lq2_ref, lk2_ref, q_ref, k_ref, vt_ref,
                 dg_ref, sg_ref, o_ref, acc1_ref, acc2_ref, m1_ref, m2_ref, s1_ref, s2_ref, cm_ref):
    n_q = acc1_ref.shape[0]
    streams = (_Stream(s1_ref, acc1_ref, m1_ref), _Stream(s2_ref, acc2_ref, m2_ref))
    _run_attention(
        qtab_ref, ttab_ref, streams, 1, k_ref, cm_ref,
        q_of=lambda st, qi, part: _split_heads(q_ref[_q_part_rows(qi, part), :])[st],
        vt_of=lambda st, tile: vt_ref[0, tile],
        bias_of=lambda st, qi, tile, j: None)

    dv = 2 * DIFF_HEAD_DIM
    lam = (jnp.exp(jnp.sum(lq1_ref[...] * lk1_ref[...], axis=-1, keepdims=True))
           - jnp.exp(jnp.sum(lq2_ref[...] * lk2_ref[...], axis=-1, keepdims=True)) + LAM_INIT)

    def finalize(qi, _):
        ot = (acc1_ref[qi, :dv] / acc1_ref[qi, dv:dv + 1]
              - lam * (acc2_ref[qi, :dv] / acc2_ref[qi, dv:dv + 1]))
        ms = jnp.mean(ot * ot, axis=0, keepdims=True)
        ot = ot * lax.rsqrt(ms + SUBLN_EPS) * sg_ref[...] * (1.0 - LAM_INIT)
        o_ref[_tile_rows(qi), :] = (ot.T * dg_ref[_tile_rows(qi), :]).astype(_BF16)
        return 0

    lax.fori_loop(0, n_q, finalize, 0)


def _moba_kernel(qtab_ref, ttab_ref, q_ref, k_ref, vt_ref, km_ref, mg_ref, o_ref,
                 acca_ref, accb_ref, ma_ref, mb_ref, sa_ref, sb_ref, cm_ref, bias_ref, qs_ref):
    n_q = acca_ref.shape[0]
    nb = km_ref.shape[0]
    per_tile = ATT_TILE // MOBA_BLOCK
    km_heads = _split_heads(km_ref[...])
    blk = lax.broadcasted_iota(jnp.int32, (nb, ATT_TILE), 0)
    col = lax.broadcasted_iota(jnp.int32, (nb, ATT_TILE), 1)
    col_blk = lax.shift_right_logical(col, MOBA_BLOCK.bit_length() - 1)

    def gate(qi, _):
        q = q_ref[_tile_rows(qi), :]
        own = per_tile * qi + col_blk
        for hh, kmh in enumerate(km_heads):
            g = lax.dot_general(kmh, q, _NT, preferred_element_type=_F32,
                                precision=lax.Precision.HIGHEST)
            g = jnp.where(blk < own, g, -jnp.inf)
            sel = jnp.zeros(g.shape, jnp.bool_)
            for _ in range(MOBA_TOPK):
                mx = jnp.max(g, axis=0, keepdims=True)
                first_idx = jnp.min(jnp.where(g == mx, blk, nb), axis=0, keepdims=True)
                pick = (blk == first_idx) & (mx > -jnp.inf)
                sel = sel | pick
                g = jnp.where(pick, -jnp.inf, g)
            bias_ref[hh, qi] = jnp.where(sel | (blk == own), 0.0, -jnp.inf)
        for hh, qh in enumerate(_split_heads((q * _SCORE_SCALE).astype(_BF16))):
            qs_ref[hh, _tile_rows(qi), :] = qh
        return 0

    lax.fori_loop(0, n_q // 2, lambda j, c: gate(2 * j + 1, gate(2 * j, c)), 0)

    streams = (_Stream(sa_ref, acca_ref, ma_ref), _Stream(sb_ref, accb_ref, mb_ref))
    _run_attention(
        qtab_ref, ttab_ref, streams, per_tile, k_ref, cm_ref,
        q_of=lambda st, qi, part: qs_ref[st, _q_part_rows(qi, part), :],
        vt_of=lambda st, tile: vt_ref[0, tile, MOBA_V_ROWS * st:MOBA_V_ROWS * (st + 1), :],
        bias_of=lambda st, qi, tile, j: bias_ref[st, qi, pl.ds(per_tile * tile + j, 1), :])

    dh = MOBA_HEAD_DIM

    def finalize(qi, _):
        ot = jnp.concatenate([acca_ref[qi, :dh] / acca_ref[qi, dh:dh + 1],
                              accb_ref[qi, :dh] / accb_ref[qi, dh:dh + 1]], axis=0)
        o_ref[_tile_rows(qi), :] = (ot.T * mg_ref[_tile_rows(qi), :]).astype(_BF16)
        return 0

    lax.fori_loop(0, n_q, finalize, 0)


def _out_kernel(oa_ref, ob_ref, gates_ref, x_ref, p_ref, wbd_ref, wbm_ref, wout_ref, wpg_ref,
                wple_ref, fg_ref, o_ref):
    ya = jnp.dot(oa_ref[...], wbd_ref[...], preferred_element_type=_F32)
    yb = jnp.dot(ob_ref[...], wbm_ref[...], preferred_element_type=_F32)
    merged = gates_ref[:, :D_MODEL] * ya + gates_ref[:, D_MODEL:] * yb
    x1 = x_ref[...] + jnp.dot(merged.astype(_BF16), wout_ref[...], preferred_element_type=_F32)
    t = jnp.dot(x1.astype(_BF16), wpg_ref[...], preferred_element_type=_F32)
    pe = jnp.dot(p_ref[...].astype(_BF16), wple_ref[...], preferred_element_type=_F32)
    x2 = x1 + jax.nn.sigmoid(t) * pe
    ms = jnp.mean(x2 * x2, axis=-1, keepdims=True)
    o_ref[...] = x2 * lax.rsqrt(ms + EPS) * fg_ref[...]


def _rope_lane_tables(seq):
    half = ROT_DIM // 2
    inv = ROPE_THETA ** (-jnp.arange(0, ROT_DIM, 2, dtype=_F32) / ROT_DIM)
    ang = jnp.arange(seq, dtype=_F32)[:, None] * inv[None, :]
    cos, sin = jnp.cos(ang), jnp.sin(ang)
    pad = MOBA_HEAD_DIM - ROT_DIM
    one, zero = jnp.ones((seq, pad), _F32), jnp.zeros((seq, pad), _F32)
    zh = jnp.zeros((seq, half), _F32)
    cos64 = jnp.concatenate([cos, cos, one], axis=1)
    slo64 = jnp.concatenate([-sin, zh, zero], axis=1)
    shi64 = jnp.concatenate([zh, sin, zero], axis=1)
    return tuple(jnp.tile(t, (1, LANES // MOBA_HEAD_DIM)) for t in (cos64, slo64, shi64))


def _resident(shape):
    return pl.BlockSpec(shape, lambda *_: (0,) * len(shape), pipeline_mode=pl.Buffered(1))


def _params(n_axes):
    return pltpu.CompilerParams(dimension_semantics=("arbitrary",) * n_axes,
                                vmem_limit_bytes=VMEM_LIMIT)


def kernel(x, p, norm_g, w_in, lambda_q1, lambda_k1, lambda_q2, lambda_k2, subln_g,
           w_branch_diff, w_branch_moba, w_out, w_ple, w_ple_gate, final_g):
    B, S, _ = x.shape
    assert w_in.shape == (1, D_MODEL, N_CHUNKS * CHUNK) and S % (2 * ATT_TILE) == 0
    rows = B * S
    n_row_tiles = rows // ROW_TILE
    n_kv = S // ATT_TILE
    n_blk = S // MOBA_BLOCK
    x2d = x.reshape(rows, D_MODEL)
    cosv, sin_lo, sin_hi = _rope_lane_tables(S)

    row_spec = lambda w: pl.BlockSpec((ROW_TILE, w), lambda i: (i, 0))
    tab_spec = pl.BlockSpec((ROW_TILE, LANES), lambda i: (i % n_kv, 0))
    dvt_rows, mvt_rows = DIFF_HEADS * DIFF_V_ROWS, MOBA_HEADS * MOBA_V_ROWS
    vt_spec = lambda r: pl.BlockSpec((1, 1, r, ATT_TILE), lambda i: (i // n_kv, i % n_kv, 0, 0))
    vt_shape = lambda r: jax.ShapeDtypeStruct((B, n_kv, r, ATT_TILE), _BF16)
    rows_bf16 = jax.ShapeDtypeStruct((rows, CHUNK), _BF16)
    rows_f32 = jax.ShapeDtypeStruct((rows, CHUNK), _F32)
    km_rows = ROW_TILE // MOBA_BLOCK
    (dq, dk, dvt, dgs, mq, mk, mvt, mgs, kmean, gates) = pl.pallas_call(
        _proj_kernel,
        grid=(n_row_tiles,),
        in_specs=[row_spec(D_MODEL), _resident((1, D_MODEL)), _resident((D_MODEL, N_CHUNKS * CHUNK)),
                  tab_spec, tab_spec, tab_spec],
        out_specs=[row_spec(CHUNK), row_spec(CHUNK), vt_spec(dvt_rows), row_spec(CHUNK), row_spec(CHUNK),
                   row_spec(CHUNK), vt_spec(mvt_rows), row_spec(CHUNK),
                   pl.BlockSpec((1, km_rows, CHUNK), lambda i: (i, 0, 0)),
                   row_spec(4 * CHUNK)],
        out_shape=[rows_bf16, rows_bf16, vt_shape(dvt_rows), rows_f32, rows_f32, rows_bf16,
                   vt_shape(mvt_rows), rows_f32,
                   jax.ShapeDtypeStruct((n_row_tiles, km_rows, CHUNK), _F32),
                   jax.ShapeDtypeStruct((rows, 4 * CHUNK), _F32)],
        compiler_params=_params(1),
        name="proj",
    )(x2d, norm_g[0].reshape(1, D_MODEL), w_in[0].astype(_BF16), cosv, sin_lo, sin_hi)
    kmean = kmean.reshape(B * n_blk, CHUNK)

    n_groups = CHUNK // LANES
    qtab, ttab = _past_items(n_kv)
    seq_spec = pl.BlockSpec((S, LANES), lambda b, g, *_: (b, g))
    v_spec = lambda r: pl.BlockSpec((1, n_kv, r, ATT_TILE), lambda b, g, *_: (b, 0, g, 0))
    lam_spec = pl.BlockSpec((1, DIFF_HEAD_DIM), lambda b, g, *_: (0, 0))
    acc = lambda r: pltpu.VMEM((n_kv, r, ATT_TILE), _F32)
    run_max = pltpu.VMEM((n_kv, 1, ATT_TILE), _F32)
    score_buf = pltpu.VMEM((2, ATT_TILE, ATT_TILE), _F32)
    col_max = lambda n: pltpu.VMEM((2, n, 1, ATT_TILE), _F32)

    o_a = pl.pallas_call(
        _diff_kernel,
        grid_spec=pltpu.PrefetchScalarGridSpec(
            num_scalar_prefetch=2,
            grid=(B, n_groups),
            in_specs=[lam_spec, lam_spec, lam_spec, lam_spec, seq_spec, seq_spec, v_spec(DIFF_V_ROWS),
                      seq_spec, pl.BlockSpec((2 * DIFF_HEAD_DIM, 1), lambda b, g, *_: (0, 0))],
            out_specs=seq_spec,
            scratch_shapes=[acc(DIFF_V_ROWS), acc(DIFF_V_ROWS), run_max, run_max, score_buf, score_buf,
                            col_max(2)],
        ),
        out_shape=rows_bf16,
        compiler_params=_params(2),
        name="diff_attn",
    )(qtab, ttab, lambda_q1[0:1], lambda_k1[0:1], lambda_q2[0:1], lambda_k2[0:1], dq, dk, dvt, dgs,
      subln_g[0].reshape(2 * DIFF_HEAD_DIM, 1))

    o_b = pl.pallas_call(
        _moba_kernel,
        grid_spec=pltpu.PrefetchScalarGridSpec(
            num_scalar_prefetch=2,
            grid=(B, n_groups),
            in_specs=[seq_spec, seq_spec, v_spec(2 * MOBA_V_ROWS),
                      pl.BlockSpec((n_blk, LANES), lambda b, g, *_: (b, g)), seq_spec],
            out_specs=seq_spec,
            scratch_shapes=[acc(MOBA_V_ROWS), acc(MOBA_V_ROWS), run_max, run_max, score_buf, score_buf,
                            col_max(2 * (ATT_TILE // MOBA_BLOCK)),
                            pltpu.VMEM((2, n_kv, n_blk, ATT_TILE), _F32),
                            pltpu.VMEM((2, S, LANES), _BF16)],
        ),
        out_shape=rows_bf16,
        compiler_params=_params(2),
        name="moba_attn",
    )(qtab, ttab, mq, mk, mvt, kmean, mgs)

    out = pl.pallas_call(
        _out_kernel,
        grid=(n_row_tiles,),
        in_specs=[row_spec(CHUNK), row_spec(CHUNK), row_spec(4 * CHUNK), row_spec(D_MODEL),
                  row_spec(PLE_DIM), _resident((DIFF_WIDTH, D_MODEL)), _resident((MOBA_WIDTH, D_MODEL)),
                  _resident((D_MODEL, D_MODEL)), _resident((D_MODEL, D_MODEL)),
                  _resident((PLE_DIM, D_MODEL)), _resident((1, D_MODEL))],
        out_specs=row_spec(D_MODEL),
        out_shape=jax.ShapeDtypeStruct((rows, D_MODEL), _F32),
        compiler_params=_params(1),
        name="out_proj",
    )(o_a, o_b, gates, x2d, p[0].reshape(rows, PLE_DIM), w_branch_diff[0].astype(_BF16),
      w_branch_moba[0].astype(_BF16), w_out[0].astype(_BF16), w_ple_gate[0].astype(_BF16),
      w_ple[0].astype(_BF16), final_g.reshape(1, D_MODEL))
    return out.reshape(B, S, D_MODEL)
```

```python
import collections
import functools
import math

import jax
import jax.numpy as jnp
import numpy as np
from jax import lax
from jax.experimental import pallas as pl
from jax.experimental.pallas import tpu as pltpu

D_MODEL = 1024
PLE_DIM = 256
DIFF_HEADS = 4
DIFF_HEAD_DIM = 64
DIFF_WIDTH = DIFF_HEADS * 2 * DIFF_HEAD_DIM
MOBA_HEADS = 8
MOBA_HEAD_DIM = 64
MOBA_WIDTH = MOBA_HEADS * MOBA_HEAD_DIM
MOBA_BLOCK = 256
MOBA_TOPK = 3
ROT_DIM = 16
ROPE_THETA = 500000.0
EPS = 1e-6
SUBLN_EPS = 1e-5
LAM_INIT = 0.8 - 0.6 * math.exp(-0.3 * 0)

LANES = 128
MXU_COLS = 256
BF16_ROWS = 16
CHUNK = 512
N_CHUNKS = 12
ROW_TILE = 512
ATT_TILE = 512
PIPE_UNROLL = 8
VMEM_LIMIT = 56 * 1024 * 1024
_SCORE_SCALE = DIFF_HEAD_DIM ** -0.5 * math.log2(math.e)
DIFF_V_ROWS = 2 * DIFF_HEAD_DIM + BF16_ROWS
MOBA_V_ROWS = MOBA_HEAD_DIM + BF16_ROWS
assert DIFF_HEAD_DIM == MOBA_HEAD_DIM and ATT_TILE % MOBA_BLOCK == 0 and ROW_TILE == ATT_TILE

_F32 = jnp.float32
_BF16 = jnp.bfloat16
_NT = (((1,), (1,)), ((), ()))


def _rope_rows(a, cosv, sin_lo, sin_hi):
    outs = []
    for j in range(CHUNK // LANES):
        xs = a[:, LANES * j:LANES * (j + 1)]
        up = pltpu.roll(xs, LANES - ROT_DIM // 2, 1)
        dn = pltpu.roll(xs, ROT_DIM // 2, 1)
        outs.append(xs * cosv + up * sin_lo + dn * sin_hi)
    return jnp.concatenate(outs, axis=1)


def _proj_kernel(x_ref, g_ref, w_ref, cos_ref, slo_ref, shi_ref,
                 dq_ref, dk_ref, dvt_ref, dg_ref, mq_ref, mk_ref, mvt_ref, mg_ref,
                 km_ref, gates_ref):
    x = x_ref[...]
    ms = jnp.mean(x * x, axis=-1, keepdims=True)
    h = (x * lax.rsqrt(ms + EPS) * g_ref[...]).astype(_BF16)
    cosv, sin_lo, sin_hi = cos_ref[...], slo_ref[...], shi_ref[...]

    def proj(c):
        return jnp.dot(h, w_ref[:, CHUNK * c:CHUNK * (c + 1)], preferred_element_type=_F32)

    def store_transposed(ref, a, head_rows):
        at = a.T.astype(_BF16)
        ones = jnp.ones((BF16_ROWS, ROW_TILE), _BF16)
        stride = head_rows + BF16_ROWS
        for g in range(CHUNK // head_rows):
            ref[0, 0, stride * g:stride * g + head_rows, :] = at[head_rows * g:head_rows * (g + 1)]
            ref[0, 0, stride * g + head_rows:stride * (g + 1), :] = ones

    dq_ref[...] = (_rope_rows(proj(0), cosv, sin_lo, sin_hi) * _SCORE_SCALE).astype(_BF16)
    dk_ref[...] = _rope_rows(proj(1), cosv, sin_lo, sin_hi).astype(_BF16)
    store_transposed(dvt_ref, proj(2), 2 * DIFF_HEAD_DIM)
    dg_ref[...] = jax.nn.silu(proj(3))
    mq_ref[...] = _rope_rows(proj(4), cosv, sin_lo, sin_hi)
    mk = _rope_rows(proj(5), cosv, sin_lo, sin_hi)
    mk_ref[...] = mk.astype(_BF16)
    for j in range(ROW_TILE // MOBA_BLOCK):
        km_ref[0, j:j + 1, :] = jnp.mean(mk[MOBA_BLOCK * j:MOBA_BLOCK * (j + 1)], axis=0, keepdims=True)
    store_transposed(mvt_ref, proj(6), MOBA_HEAD_DIM)
    mg_ref[...] = jax.nn.silu(proj(7))
    for c in range(8, N_CHUNKS):
        gates_ref[:, CHUNK * (c - 8):CHUNK * (c - 7)] = jax.nn.sigmoid(proj(c))


def _split_heads(q):
    lane = lax.broadcasted_iota(jnp.int32, q.shape, 1)
    zero = jnp.zeros_like(q)
    return jnp.where(lane < 64, q, zero), jnp.where(lane >= 64, q, zero)


def _tile_rows(i):
    return pl.ds(pl.multiple_of(i * ATT_TILE, ATT_TILE), ATT_TILE)


def _past_items(n_q):
    items = [(qi, t) for qi in range(1, n_q) for t in range(qi)]
    assert len(items) % PIPE_UNROLL == 0 and PIPE_UNROLL % 2 == 0
    items.append(items[-1])
    return (np.array([i[0] for i in items], np.int32), np.array([i[1] for i in items], np.int32))


_Stream = collections.namedtuple("_Stream", "s_ref acc_ref m_ref")


def _run_attention(qtab_ref, ttab_ref, streams, n_seg, k_ref, cm_ref, q_of, vt_of, bias_of,
                   interleave):
    n_q = streams[0].acc_ref.shape[0]
    seg_rows = ATT_TILE // n_seg
    part_cols = MXU_COLS if interleave else ATT_TILE
    parts = ATT_TILE // part_cols

    def softmax(st, qi, tile, slot, first):
        stream = streams[st]
        biases = [bias_of(st, qi, tile, j) for j in range(n_seg)]
        cands = [cm_ref[slot, st * n_seg + j] if b is None else cm_ref[slot, st * n_seg + j] + b
                 for j, b in enumerate(biases)]
        m_old = None if first else stream.m_ref[qi]
        m_new = functools.reduce(jnp.maximum, cands if first else [m_old] + cands)
        stream.m_ref[qi] = m_new
        p = [jnp.exp2(stream.s_ref[slot, seg_rows * j:seg_rows * (j + 1), :]
                      - (m_new if b is None else m_new - b)).astype(_BF16)
             for j, b in enumerate(biases)]
        p = p[0] if n_seg == 1 else jnp.concatenate(p, axis=0)
        return p, (None if first else jnp.exp2(m_old - m_new))

    def qk_part(st, k, qi, slot, part, diagonal):
        cols = slice(part_cols * part, part_cols * (part + 1))
        rows = pl.ds(pl.multiple_of(qi * ATT_TILE + part_cols * part, part_cols), part_cols)
        s = lax.dot_general(k, q_of(st, rows), _NT, preferred_element_type=_F32)
        if diagonal:
            kpos = lax.broadcasted_iota(jnp.int32, s.shape, 0)
            qpos = lax.broadcasted_iota(jnp.int32, s.shape, 1) + part_cols * part
            s = jnp.where(kpos <= qpos, s, -jnp.inf)
        streams[st].s_ref[slot, :, cols] = s
        for j in range(n_seg):
            cm_ref[slot, st * n_seg + j, :, cols] = jnp.max(s[seg_rows * j:seg_rows * (j + 1)],
                                                           axis=0, keepdims=True)

    def pv_part(st, vt, p, alpha, qi, part):
        cols = slice(part_cols * part, part_cols * (part + 1))
        acc_ref = streams[st].acc_ref
        pv = jnp.dot(vt, p[:, cols], preferred_element_type=_F32)
        acc_ref[qi, :, cols] = pv if alpha is None else alpha[:, cols] * acc_ref[qi, :, cols] + pv

    def step(nxt, cur, first=False):
        k = None if nxt is None else k_ref[_tile_rows(nxt[1]), :]
        if not interleave and nxt is not None:
            for st in range(len(streams)):
                for part in range(parts):
                    qk_part(st, k, nxt[0], nxt[2], part, nxt[3])
        for st in range(len(streams)):
            if cur is not None:
                p, alpha = softmax(st, cur[0], cur[1], cur[2], first)
                vt = vt_of(st, cur[1])
            for part in range(parts):
                if interleave and nxt is not None:
                    qk_part(st, k, nxt[0], nxt[2], part, nxt[3])
                if cur is not None:
                    pv_part(st, vt, p, alpha, cur[0], part)

    step((0, 0, 0, True), None)

    def diag_items(j, _):
        for u in range(2):
            t = 2 * j + u
            step((t + 1, t + 1, (u + 1) % 2, True), (t, t, u), first=True)
        return 0

    lax.fori_loop(0, n_q // 2 - 1, diag_items, 0)
    step((n_q - 1, n_q - 1, 1, True), (n_q - 2, n_q - 2, 0), first=True)
    step((qtab_ref[0], ttab_ref[0], 0, False), (n_q - 1, n_q - 1, 1), first=True)

    def past_items(j, _):
        for u in range(PIPE_UNROLL):
            t = PIPE_UNROLL * j + u
            step((qtab_ref[t + 1], ttab_ref[t + 1], (u + 1) % 2, False),
                 (qtab_ref[t], ttab_ref[t], u % 2))
        return 0

    lax.fori_loop(0, n_q * (n_q - 1) // (2 * PIPE_UNROLL), past_items, 0)


def _diff_kernel(qtab_ref, ttab_ref, lq1_ref, lk1_ref, lq2_ref, lk2_ref, q_ref, k_ref, vt_ref,
                 dg_ref, sg_ref, o_ref, acc1_ref, acc2_ref, m1_ref, m2_ref, s1_ref, s2_ref, cm_ref):
    n_q = acc1_ref.shape[0]
    streams = (_Stream(s1_ref, acc1_ref, m1_ref), _Stream(s2_ref, acc2_ref, m2_ref))
    _run_attention(
        qtab_ref, ttab_ref, streams, 1, k_ref, cm_ref,
        q_of=lambda st, rows: _split_heads(q_ref[rows, :])[st],
        vt_of=lambda st, tile: vt_ref[0, tile],
        bias_of=lambda st, qi, tile, j: None,
        interleave=False)

    dv = 2 * DIFF_HEAD_DIM
    lam = (jnp.exp(jnp.sum(lq1_ref[...] * lk1_ref[...], axis=-1, keepdims=True))
           - jnp.exp(jnp.sum(lq2_ref[...] * lk2_ref[...], axis=-1, keepdims=True)) + LAM_INIT)

    def finalize(qi, _):
        ot = (acc1_ref[qi, :dv] / acc1_ref[qi, dv:dv + 1]
              - lam * (acc2_ref[qi, :dv] / acc2_ref[qi, dv:dv + 1]))
        ms = jnp.mean(ot * ot, axis=0, keepdims=True)
        ot = ot * lax.rsqrt(ms + SUBLN_EPS) * sg_ref[...] * (1.0 - LAM_INIT)
        o_ref[_tile_rows(qi), :] = (ot.T * dg_ref[_tile_rows(qi), :]).astype(_BF16)
        return 0

    lax.fori_loop(0, n_q, finalize, 0)


def _moba_kernel(qtab_ref, ttab_ref, q_ref, k_ref, vt_ref, km_ref, mg_ref, o_ref,
                 acca_ref, accb_ref, ma_ref, mb_ref, sa_ref, sb_ref, cm_ref, bias_ref, qs_ref):
    n_q = acca_ref.shape[0]
    nb = km_ref.shape[0]
    per_tile = ATT_TILE // MOBA_BLOCK
    km_heads = _split_heads(km_ref[...])
    blk = lax.broadcasted_iota(jnp.int32, (nb, ATT_TILE), 0)
    col = lax.broadcasted_iota(jnp.int32, (nb, ATT_TILE), 1)
    col_blk = lax.shift_right_logical(col, MOBA_BLOCK.bit_length() - 1)

    def gate(qi, _):
        q = q_ref[_tile_rows(qi), :]
        own = per_tile * qi + col_blk
        for hh, kmh in enumerate(km_heads):
            g = lax.dot_general(kmh, q, _NT, preferred_element_type=_F32,
                                precision=lax.Precision.HIGHEST)
            g = jnp.where(blk < own, g, -jnp.inf)
            sel = jnp.zeros(g.shape, jnp.bool_)
            for _ in range(MOBA_TOPK):
                mx = jnp.max(g, axis=0, keepdims=True)
                first_idx = jnp.min(jnp.where(g == mx, blk, nb), axis=0, keepdims=True)
                pick = (blk == first_idx) & (mx > -jnp.inf)
                sel = sel | pick
                g = jnp.where(pick, -jnp.inf, g)
            bias_ref[hh, qi] = jnp.where(sel | (blk == own), 0.0, -jnp.inf)
        for hh, qh in enumerate(_split_heads((q * _SCORE_SCALE).astype(_BF16))):
            qs_ref[hh, _tile_rows(qi), :] = qh
        return 0

    lax.fori_loop(0, n_q // 2, lambda j, c: gate(2 * j + 1, gate(2 * j, c)), 0)

    streams = (_Stream(sa_ref, acca_ref, ma_ref), _Stream(sb_ref, accb_ref, mb_ref))
    _run_attention(
        qtab_ref, ttab_ref, streams, per_tile, k_ref, cm_ref,
        q_of=lambda st, rows: qs_ref[st, rows, :],
        vt_of=lambda st, tile: vt_ref[0, tile, MOBA_V_ROWS * st:MOBA_V_ROWS * (st + 1), :],
        bias_of=lambda st, qi, tile, j: bias_ref[st, qi, pl.ds(per_tile * tile + j, 1), :],
        interleave=True)

    dh = MOBA_HEAD_DIM

    def finalize(qi, _):
        ot = jnp.concatenate([acca_ref[qi, :dh] / acca_ref[qi, dh:dh + 1],
                              accb_ref[qi, :dh] / accb_ref[qi, dh:dh + 1]], axis=0)
        o_ref[_tile_rows(qi), :] = (ot.T * mg_ref[_tile_rows(qi), :]).astype(_BF16)
        return 0

    lax.fori_loop(0, n_q, finalize, 0)


def _out_kernel(oa_ref, ob_ref, gates_ref, x_ref, p_ref, wbd_ref, wbm_ref, wout_ref, wpg_ref,
                wple_ref, fg_ref, o_ref, wbd_bf, wbm_bf, wout_bf, wpg_bf, wple_bf):
    @pl.when(pl.program_id(0) == 0)
    def _():
        for src, dst in ((wbd_ref, wbd_bf), (wbm_ref, wbm_bf), (wout_ref, wout_bf),
                         (wpg_ref, wpg_bf), (wple_ref, wple_bf)):
            dst[...] = src[...].astype(_BF16)

    ya = jnp.dot(oa_ref[...], wbd_bf[...], preferred_element_type=_F32)
    yb = jnp.dot(ob_ref[...], wbm_bf[...], preferred_element_type=_F32)
    merged = gates_ref[:, :D_MODEL] * ya + gates_ref[:, D_MODEL:] * yb
    x1 = x_ref[...] + jnp.dot(merged.astype(_BF16), wout_bf[...], preferred_element_type=_F32)
    t = jnp.dot(x1.astype(_BF16), wpg_bf[...], preferred_element_type=_F32)
    pe = jnp.dot(p_ref[...].astype(_BF16), wple_bf[...], preferred_element_type=_F32)
    x2 = x1 + jax.nn.sigmoid(t) * pe
    ms = jnp.mean(x2 * x2, axis=-1, keepdims=True)
    o_ref[...] = x2 * lax.rsqrt(ms + EPS) * fg_ref[...]


def _rope_lane_tables(seq):
    half = ROT_DIM // 2
    inv = ROPE_THETA ** (-np.arange(0, ROT_DIM, 2, dtype=np.float64) / ROT_DIM)
    ang = np.arange(seq, dtype=np.float64)[:, None] * inv[None, :]
    cos, sin = np.cos(ang), np.sin(ang)
    pad = MOBA_HEAD_DIM - ROT_DIM
    one, zero = np.ones((seq, pad)), np.zeros((seq, pad))
    zh = np.zeros((seq, half))
    cos64 = np.concatenate([cos, cos, one], axis=1)
    slo64 = np.concatenate([-sin, zh, zero], axis=1)
    shi64 = np.concatenate([zh, sin, zero], axis=1)
    return tuple(jnp.asarray(np.tile(t, (1, LANES // MOBA_HEAD_DIM)), _F32)
                 for t in (cos64, slo64, shi64))


def _resident(shape):
    return pl.BlockSpec(shape, lambda *_: (0,) * len(shape), pipeline_mode=pl.Buffered(1))


def _params(n_axes):
    return pltpu.CompilerParams(dimension_semantics=("arbitrary",) * n_axes,
                                vmem_limit_bytes=VMEM_LIMIT)


def kernel(x, p, norm_g, w_in, lambda_q1, lambda_k1, lambda_q2, lambda_k2, subln_g,
           w_branch_diff, w_branch_moba, w_out, w_ple, w_ple_gate, final_g):
    B, S, _ = x.shape
    assert w_in.shape == (1, D_MODEL, N_CHUNKS * CHUNK) and S % (2 * ATT_TILE) == 0
    rows = B * S
    n_row_tiles = rows // ROW_TILE
    n_kv = S // ATT_TILE
    n_blk = S // MOBA_BLOCK
    x2d = x.reshape(rows, D_MODEL)
    cosv, sin_lo, sin_hi = _rope_lane_tables(S)

    row_spec = lambda w: pl.BlockSpec((ROW_TILE, w), lambda i: (i, 0))
    tab_spec = pl.BlockSpec((ROW_TILE, LANES), lambda i: (i % n_kv, 0))
    dvt_rows, mvt_rows = DIFF_HEADS * DIFF_V_ROWS, MOBA_HEADS * MOBA_V_ROWS
    vt_spec = lambda r: pl.BlockSpec((1, 1, r, ATT_TILE), lambda i: (i // n_kv, i % n_kv, 0, 0))
    vt_shape = lambda r: jax.ShapeDtypeStruct((B, n_kv, r, ATT_TILE), _BF16)
    rows_bf16 = jax.ShapeDtypeStruct((rows, CHUNK), _BF16)
    rows_f32 = jax.ShapeDtypeStruct((rows, CHUNK), _F32)
    km_rows = ROW_TILE // MOBA_BLOCK
    (dq, dk, dvt, dgs, mq, mk, mvt, mgs, kmean, gates) = pl.pallas_call(
        _proj_kernel,
        grid=(n_row_tiles,),
        in_specs=[row_spec(D_MODEL), _resident((1, D_MODEL)), _resident((D_MODEL, N_CHUNKS * CHUNK)),
                  tab_spec, tab_spec, tab_spec],
        out_specs=[row_spec(CHUNK), row_spec(CHUNK), vt_spec(dvt_rows), row_spec(CHUNK), row_spec(CHUNK),
                   row_spec(CHUNK), vt_spec(mvt_rows), row_spec(CHUNK),
                   pl.BlockSpec((1, km_rows, CHUNK), lambda i: (i, 0, 0)),
                   row_spec(4 * CHUNK)],
        out_shape=[rows_bf16, rows_bf16, vt_shape(dvt_rows), rows_f32, rows_f32, rows_bf16,
                   vt_shape(mvt_rows), rows_f32,
                   jax.ShapeDtypeStruct((n_row_tiles, km_rows, CHUNK), _F32),
                   jax.ShapeDtypeStruct((rows, 4 * CHUNK), _F32)],
        compiler_params=_params(1),
        name="proj",
    )(x2d, norm_g[0].reshape(1, D_MODEL), w_in[0].astype(_BF16), cosv, sin_lo, sin_hi)
    kmean = kmean.reshape(B * n_blk, CHUNK)

    n_groups = CHUNK // LANES
    qtab, ttab = _past_items(n_kv)
    seq_spec = pl.BlockSpec((S, LANES), lambda b, g, *_: (b, g))
    v_spec = lambda r: pl.BlockSpec((1, n_kv, r, ATT_TILE), lambda b, g, *_: (b, 0, g, 0))
    lam_spec = pl.BlockSpec((1, DIFF_HEAD_DIM), lambda b, g, *_: (0, 0))
    acc = lambda r: pltpu.VMEM((n_kv, r, ATT_TILE), _F32)
    run_max = pltpu.VMEM((n_kv, 1, ATT_TILE), _F32)
    score_buf = pltpu.VMEM((2, ATT_TILE, ATT_TILE), _F32)
    col_max = lambda n: pltpu.VMEM((2, n, 1, ATT_TILE), _F32)

    o_a = pl.pallas_call(
        _diff_kernel,
        grid_spec=pltpu.PrefetchScalarGridSpec(
            num_scalar_prefetch=2,
            grid=(B, n_groups),
            in_specs=[lam_spec, lam_spec, lam_spec, lam_spec, seq_spec, seq_spec, v_spec(DIFF_V_ROWS),
                      seq_spec, pl.BlockSpec((2 * DIFF_HEAD_DIM, 1), lambda b, g, *_: (0, 0))],
            out_specs=seq_spec,
            scratch_shapes=[acc(DIFF_V_ROWS), acc(DIFF_V_ROWS), run_max, run_max, score_buf, score_buf,
                            col_max(2)],
        ),
        out_shape=rows_bf16,
        compiler_params=_params(2),
        name="diff_attn",
    )(qtab, ttab, lambda_q1[0:1], lambda_k1[0:1], lambda_q2[0:1], lambda_k2[0:1], dq, dk, dvt, dgs,
      subln_g[0].reshape(2 * DIFF_HEAD_DIM, 1))

    o_b = pl.pallas_call(
        _moba_kernel,
        grid_spec=pltpu.PrefetchScalarGridSpec(
            num_scalar_prefetch=2,
            grid=(B, n_groups),
            in_specs=[seq_spec, seq_spec, v_spec(2 * MOBA_V_ROWS),
                      pl.BlockSpec((n_blk, LANES), lambda b, g, *_: (b, g)), seq_spec],
            out_specs=seq_spec,
            scratch_shapes=[acc(MOBA_V_ROWS), acc(MOBA_V_ROWS), run_max, run_max, score_buf, score_buf,
                            col_max(2 * (ATT_TILE // MOBA_BLOCK)),
                            pltpu.VMEM((2, n_kv, n_blk, ATT_TILE), _F32),
                            pltpu.VMEM((2, S, LANES), _BF16)],
        ),
        out_shape=rows_bf16,
        compiler_params=_params(2),
        name="moba_attn",
    )(qtab, ttab, mq, mk, mvt, kmean, mgs)

    out = pl.pallas_call(
        _out_kernel,
        grid=(n_row_tiles,),
        in_specs=[row_spec(CHUNK), row_spec(CHUNK), row_spec(4 * CHUNK), row_spec(D_MODEL),
                  row_spec(PLE_DIM), _resident((DIFF_WIDTH, D_MODEL)), _resident((MOBA_WIDTH, D_MODEL)),
                  _resident((D_MODEL, D_MODEL)), _resident((D_MODEL, D_MODEL)),
                  _resident((PLE_DIM, D_MODEL)), _resident((1, D_MODEL))],
        out_specs=row_spec(D_MODEL),
        out_shape=jax.ShapeDtypeStruct((rows, D_MODEL), _F32),
        scratch_shapes=[pltpu.VMEM(s, _BF16) for s in ((DIFF_WIDTH, D_MODEL), (MOBA_WIDTH, D_MODEL),
                                                      (D_MODEL, D_MODEL), (D_MODEL, D_MODEL),
                                                      (PLE_DIM, D_MODEL))],
        compiler_params=_params(1),
        name="out_proj",
    )(o_a, o_b, gates, x2d, p[0].reshape(rows, PLE_DIM), w_branch_diff[0], w_branch_moba[0],
      w_out[0], w_ple_gate[0], w_ple[0], final_g.reshape(1, D_MODEL))
    return out.reshape(B, S, D_MODEL)
```

```python
import collections
import functools
import math

import jax
import jax.numpy as jnp
import numpy as np
from jax import lax
from jax.experimental import pallas as pl
from jax.experimental.pallas import tpu as pltpu

D_MODEL = 1024
PLE_DIM = 256
DIFF_HEADS = 4
DIFF_HEAD_DIM = 64
DIFF_WIDTH = DIFF_HEADS * 2 * DIFF_HEAD_DIM
MOBA_HEADS = 8
MOBA_HEAD_DIM = 64
MOBA_WIDTH = MOBA_HEADS * MOBA_HEAD_DIM
MOBA_BLOCK = 256
MOBA_TOPK = 3
ROT_DIM = 16
ROPE_THETA = 500000.0
EPS = 1e-6
SUBLN_EPS = 1e-5
LAM_INIT = 0.8 - 0.6 * math.exp(-0.3 * 0)

LANES = 128
MXU_COLS = 256
BF16_ROWS = 16
CHUNK = 512
N_CHUNKS = 12
ROW_TILE = 512
ATT_TILE = 512
PIPE_UNROLL = 8
VMEM_LIMIT = 56 * 1024 * 1024
_SCORE_SCALE = DIFF_HEAD_DIM ** -0.5 * math.log2(math.e)
DIFF_V_ROWS = 2 * DIFF_HEAD_DIM + BF16_ROWS
MOBA_V_ROWS = MOBA_HEAD_DIM + BF16_ROWS
assert DIFF_HEAD_DIM == MOBA_HEAD_DIM and ATT_TILE % MOBA_BLOCK == 0 and ROW_TILE == ATT_TILE

_F32 = jnp.float32
_BF16 = jnp.bfloat16
_NT = (((1,), (1,)), ((), ()))


def _rope_rows(a, cosv, sin_lo, sin_hi):
    outs = []
    for j in range(CHUNK // LANES):
        xs = a[:, LANES * j:LANES * (j + 1)]
        up = pltpu.roll(xs, LANES - ROT_DIM // 2, 1)
        dn = pltpu.roll(xs, ROT_DIM // 2, 1)
        outs.append(xs * cosv + up * sin_lo + dn * sin_hi)
    return jnp.concatenate(outs, axis=1)


def _proj_kernel(x_ref, g_ref, w_ref, cos_ref, slo_ref, shi_ref,
                 dq_ref, dk_ref, dvt_ref, dg_ref, mq_ref, mk_ref, mvt_ref, mg_ref,
                 km_ref, gates_ref):
    x = x_ref[...]
    ms = jnp.mean(x * x, axis=-1, keepdims=True)
    h = (x * lax.rsqrt(ms + EPS) * g_ref[...]).astype(_BF16)
    cosv, sin_lo, sin_hi = cos_ref[...], slo_ref[...], shi_ref[...]

    def proj(c):
        return jnp.dot(h, w_ref[:, CHUNK * c:CHUNK * (c + 1)], preferred_element_type=_F32)

    def store_transposed(ref, a, head_rows):
        at = a.T.astype(_BF16)
        ones = jnp.ones((BF16_ROWS, ROW_TILE), _BF16)
        stride = head_rows + BF16_ROWS
        for g in range(CHUNK // head_rows):
            ref[0, 0, stride * g:stride * g + head_rows, :] = at[head_rows * g:head_rows * (g + 1)]
            ref[0, 0, stride * g + head_rows:stride * (g + 1), :] = ones

    dq_ref[...] = (_rope_rows(proj(0), cosv, sin_lo, sin_hi) * _SCORE_SCALE).astype(_BF16)
    dk_ref[...] = _rope_rows(proj(1), cosv, sin_lo, sin_hi).astype(_BF16)
    store_transposed(dvt_ref, proj(2), 2 * DIFF_HEAD_DIM)
    dg_ref[...] = jax.nn.silu(proj(3))
    mq_ref[...] = _rope_rows(proj(4), cosv, sin_lo, sin_hi)
    mk = _rope_rows(proj(5), cosv, sin_lo, sin_hi)
    mk_ref[...] = mk.astype(_BF16)
    for j in range(ROW_TILE // MOBA_BLOCK):
        km_ref[0, j:j + 1, :] = jnp.mean(mk[MOBA_BLOCK * j:MOBA_BLOCK * (j + 1)], axis=0, keepdims=True)
    store_transposed(mvt_ref, proj(6), MOBA_HEAD_DIM)
    mg_ref[...] = jax.nn.silu(proj(7))
    for c in range(8, N_CHUNKS):
        gates_ref[:, CHUNK * (c - 8):CHUNK * (c - 7)] = jax.nn.sigmoid(proj(c))


def _split_heads(q):
    lane = lax.broadcasted_iota(jnp.int32, q.shape, 1)
    zero = jnp.zeros_like(q)
    return jnp.where(lane < 64, q, zero), jnp.where(lane >= 64, q, zero)


def _tile_rows(i):
    return pl.ds(pl.multiple_of(i * ATT_TILE, ATT_TILE), ATT_TILE)


def _past_items(n_q):
    items = [(qi, t) for qi in range(1, n_q) for t in range(qi)]
    assert len(items) % PIPE_UNROLL == 0 and PIPE_UNROLL % 2 == 0
    items.append(items[-1])
    return (np.array([i[0] for i in items], np.int32), np.array([i[1] for i in items], np.int32))


_Stream = collections.namedtuple("_Stream", "s_ref acc_ref m_ref")


def _run_attention(qtab_ref, ttab_ref, streams, n_seg, k_ref, cm_ref, q_of, vt_of, bias_of,
                   interleave):
    n_q = streams[0].acc_ref.shape[0]
    seg_rows = ATT_TILE // n_seg
    part_cols = MXU_COLS if interleave == "part" else ATT_TILE
    parts = ATT_TILE // part_cols

    def softmax(st, qi, tile, slot, first):
        stream = streams[st]
        biases = [bias_of(st, qi, tile, j) for j in range(n_seg)]
        cands = [cm_ref[slot, st * n_seg + j] if b is None else cm_ref[slot, st * n_seg + j] + b
                 for j, b in enumerate(biases)]
        m_old = None if first else stream.m_ref[qi]
        m_new = functools.reduce(jnp.maximum, cands if first else [m_old] + cands)
        stream.m_ref[qi] = m_new
        p = [jnp.exp2(stream.s_ref[slot, seg_rows * j:seg_rows * (j + 1), :]
                      - (m_new if b is None else m_new - b)).astype(_BF16)
             for j, b in enumerate(biases)]
        p = p[0] if n_seg == 1 else jnp.concatenate(p, axis=0)
        return p, (None if first else jnp.exp2(m_old - m_new))

    def qk_part(st, k, qi, slot, part, diagonal):
        cols = slice(part_cols * part, part_cols * (part + 1))
        rows = pl.ds(pl.multiple_of(qi * ATT_TILE + part_cols * part, part_cols), part_cols)
        s = lax.dot_general(k, q_of(st, rows), _NT, preferred_element_type=_F32)
        if diagonal:
            kpos = lax.broadcasted_iota(jnp.int32, s.shape, 0)
            qpos = lax.broadcasted_iota(jnp.int32, s.shape, 1) + part_cols * part
            s = jnp.where(kpos <= qpos, s, -jnp.inf)
        streams[st].s_ref[slot, :, cols] = s
        for j in range(n_seg):
            cm_ref[slot, st * n_seg + j, :, cols] = jnp.max(s[seg_rows * j:seg_rows * (j + 1)],
                                                           axis=0, keepdims=True)

    def pv_part(st, vt, p, alpha, qi, part):
        cols = slice(part_cols * part, part_cols * (part + 1))
        acc_ref = streams[st].acc_ref
        pv = jnp.dot(vt, p[:, cols], preferred_element_type=_F32)
        acc_ref[qi, :, cols] = pv if alpha is None else alpha[:, cols] * acc_ref[qi, :, cols] + pv

    def step(nxt, cur, first=False):
        k = None if nxt is None else k_ref[_tile_rows(nxt[1]), :]
        if not interleave and nxt is not None:
            for st in range(len(streams)):
                for part in range(parts):
                    qk_part(st, k, nxt[0], nxt[2], part, nxt[3])
        for st in range(len(streams)):
            if cur is not None:
                p, alpha = softmax(st, cur[0], cur[1], cur[2], first)
                vt = vt_of(st, cur[1])
            for part in range(parts):
                if interleave and nxt is not None:
                    qk_part(st, k, nxt[0], nxt[2], part, nxt[3])
                if cur is not None:
                    pv_part(st, vt, p, alpha, cur[0], part)

    step((0, 0, 0, True), None)

    def diag_items(j, _):
        for u in range(2):
            t = 2 * j + u
            step((t + 1, t + 1, (u + 1) % 2, True), (t, t, u), first=True)
        return 0

    lax.fori_loop(0, n_q // 2 - 1, diag_items, 0)
    step((n_q - 1, n_q - 1, 1, True), (n_q - 2, n_q - 2, 0), first=True)
    step((qtab_ref[0], ttab_ref[0], 0, False), (n_q - 1, n_q - 1, 1), first=True)

    def past_items(j, _):
        for u in range(PIPE_UNROLL):
            t = PIPE_UNROLL * j + u
            step((qtab_ref[t + 1], ttab_ref[t + 1], (u + 1) % 2, False),
                 (qtab_ref[t], ttab_ref[t], u % 2))
        return 0

    lax.fori_loop(0, n_q * (n_q - 1) // (2 * PIPE_UNROLL), past_items, 0)


def _diff_kernel(qtab_ref, ttab_ref, lq1_ref, lk1_ref, lq2_ref, lk2_ref, q_ref, k_ref, vt_ref,
                 dg_ref, sg_ref, o_ref, acc1_ref, acc2_ref, m1_ref, m2_ref, s1_ref, s2_ref, cm_ref):
    n_q = acc1_ref.shape[0]
    streams = (_Stream(s1_ref, acc1_ref, m1_ref), _Stream(s2_ref, acc2_ref, m2_ref))
    _run_attention(
        qtab_ref, ttab_ref, streams, 1, k_ref, cm_ref,
        q_of=lambda st, rows: _split_heads(q_ref[rows, :])[st],
        vt_of=lambda st, tile: vt_ref[0, tile],
        bias_of=lambda st, qi, tile, j: None,
        interleave="stream")

    dv = 2 * DIFF_HEAD_DIM
    lam = (jnp.exp(jnp.sum(lq1_ref[...] * lk1_ref[...], axis=-1, keepdims=True))
           - jnp.exp(jnp.sum(lq2_ref[...] * lk2_ref[...], axis=-1, keepdims=True)) + LAM_INIT)

    def finalize(qi, _):
        ot = (acc1_ref[qi, :dv] / acc1_ref[qi, dv:dv + 1]
              - lam * (acc2_ref[qi, :dv] / acc2_ref[qi, dv:dv + 1]))
        ms = jnp.mean(ot * ot, axis=0, keepdims=True)
        ot = ot * lax.rsqrt(ms + SUBLN_EPS) * sg_ref[...] * (1.0 - LAM_INIT)
        o_ref[_tile_rows(qi), :] = (ot.T * dg_ref[_tile_rows(qi), :]).astype(_BF16)
        return 0

    lax.fori_loop(0, n_q, finalize, 0)


def _moba_kernel(qtab_ref, ttab_ref, q_ref, k_ref, vt_ref, km_ref, mg_ref, o_ref,
                 acca_ref, accb_ref, ma_ref, mb_ref, sa_ref, sb_ref, cm_ref, bias_ref, qs_ref):
    n_q = acca_ref.shape[0]
    nb = km_ref.shape[0]
    per_tile = ATT_TILE // MOBA_BLOCK
    km_both = jnp.concatenate(_split_heads(km_ref[...]), axis=0)
    blk = lax.broadcasted_iota(jnp.int32, (nb, ATT_TILE), 0)
    col = lax.broadcasted_iota(jnp.int32, (nb, ATT_TILE), 1)
    col_blk = lax.shift_right_logical(col, MOBA_BLOCK.bit_length() - 1)

    def gate(qi, _):
        q = q_ref[_tile_rows(qi), :]
        own = per_tile * qi + col_blk
        g_both = lax.dot_general(km_both, q, _NT, preferred_element_type=_F32,
                                 precision=lax.Precision.HIGHEST)
        for hh in range(2):
            g = jnp.where(blk < own, g_both[nb * hh:nb * (hh + 1)], -jnp.inf)
            sel = jnp.zeros(g.shape, jnp.bool_)
            for _ in range(MOBA_TOPK):
                mx = jnp.max(g, axis=0, keepdims=True)
                first_idx = jnp.min(jnp.where(g == mx, blk, nb), axis=0, keepdims=True)
                pick = (blk == first_idx) & (mx > -jnp.inf)
                sel = sel | pick
                g = jnp.where(pick, -jnp.inf, g)
            bias_ref[hh, qi] = jnp.where(sel | (blk == own), 0.0, -jnp.inf)
        for hh, qh in enumerate(_split_heads((q * _SCORE_SCALE).astype(_BF16))):
            qs_ref[hh, _tile_rows(qi), :] = qh
        return 0

    lax.fori_loop(0, n_q // 2, lambda j, c: gate(2 * j + 1, gate(2 * j, c)), 0)

    streams = (_Stream(sa_ref, acca_ref, ma_ref), _Stream(sb_ref, accb_ref, mb_ref))
    _run_attention(
        qtab_ref, ttab_ref, streams, per_tile, k_ref, cm_ref,
        q_of=lambda st, rows: qs_ref[st, rows, :],
        vt_of=lambda st, tile: vt_ref[0, tile, MOBA_V_ROWS * st:MOBA_V_ROWS * (st + 1), :],
        bias_of=lambda st, qi, tile, j: bias_ref[st, qi, pl.ds(per_tile * tile + j, 1), :],
        interleave="part")

    dh = MOBA_HEAD_DIM

    def finalize(qi, _):
        ot = jnp.concatenate([acca_ref[qi, :dh] / acca_ref[qi, dh:dh + 1],
                              accb_ref[qi, :dh] / accb_ref[qi, dh:dh + 1]], axis=0)
        o_ref[_tile_rows(qi), :] = (ot.T * mg_ref[_tile_rows(qi), :]).astype(_BF16)
        return 0

    lax.fori_loop(0, n_q, finalize, 0)


def _out_kernel(oa_ref, ob_ref, gates_ref, x_ref, p_ref, wbd_ref, wbm_ref, wout_ref, wpg_ref,
                wple_ref, fg_ref, o_ref, wbd_bf, wbm_bf, wout_bf, wpg_bf, wple_bf):
    @pl.when(pl.program_id(0) == 0)
    def _():
        for src, dst in ((wbd_ref, wbd_bf), (wbm_ref, wbm_bf), (wout_ref, wout_bf),
                         (wpg_ref, wpg_bf), (wple_ref, wple_bf)):
            dst[...] = src[...].astype(_BF16)

    ya = jnp.dot(oa_ref[...], wbd_bf[...], preferred_element_type=_F32)
    yb = jnp.dot(ob_ref[...], wbm_bf[...], preferred_element_type=_F32)
    merged = gates_ref[:, :D_MODEL] * ya + gates_ref[:, D_MODEL:] * yb
    x1 = x_ref[...] + jnp.dot(merged.astype(_BF16), wout_bf[...], preferred_element_type=_F32)
    t = jnp.dot(x1.astype(_BF16), wpg_bf[...], preferred_element_type=_F32)
    pe = jnp.dot(p_ref[...].astype(_BF16), wple_bf[...], preferred_element_type=_F32)
    x2 = x1 + jax.nn.sigmoid(t) * pe
    ms = jnp.mean(x2 * x2, axis=-1, keepdims=True)
    o_ref[...] = x2 * lax.rsqrt(ms + EPS) * fg_ref[...]


def _rope_lane_tables(seq):
    half = ROT_DIM // 2
    inv = ROPE_THETA ** (-np.arange(0, ROT_DIM, 2, dtype=np.float64) / ROT_DIM)
    ang = np.arange(seq, dtype=np.float64)[:, None] * inv[None, :]
    cos, sin = np.cos(ang), np.sin(ang)
    pad = MOBA_HEAD_DIM - ROT_DIM
    one, zero = np.ones((seq, pad)), np.zeros((seq, pad))
    zh = np.zeros((seq, half))
    cos64 = np.concatenate([cos, cos, one], axis=1)
    slo64 = np.concatenate([-sin, zh, zero], axis=1)
    shi64 = np.concatenate([zh, sin, zero], axis=1)
    return tuple(jnp.asarray(np.tile(t, (1, LANES // MOBA_HEAD_DIM)), _F32)
                 for t in (cos64, slo64, shi64))


def _resident(shape):
    return pl.BlockSpec(shape, lambda *_: (0,) * len(shape), pipeline_mode=pl.Buffered(1))


def _params(n_axes):
    return pltpu.CompilerParams(dimension_semantics=("arbitrary",) * n_axes,
                                vmem_limit_bytes=VMEM_LIMIT)


def kernel(x, p, norm_g, w_in, lambda_q1, lambda_k1, lambda_q2, lambda_k2, subln_g,
           w_branch_diff, w_branch_moba, w_out, w_ple, w_ple_gate, final_g):
    B, S, _ = x.shape
    assert w_in.shape == (1, D_MODEL, N_CHUNKS * CHUNK) and S % (2 * ATT_TILE) == 0
    rows = B * S
    n_row_tiles = rows // ROW_TILE
    n_kv = S // ATT_TILE
    n_blk = S // MOBA_BLOCK
    x2d = x.reshape(rows, D_MODEL)
    cosv, sin_lo, sin_hi = _rope_lane_tables(S)

    row_spec = lambda w: pl.BlockSpec((ROW_TILE, w), lambda i: (i, 0))
    tab_spec = pl.BlockSpec((ROW_TILE, LANES), lambda i: (i % n_kv, 0))
    dvt_rows, mvt_rows = DIFF_HEADS * DIFF_V_ROWS, MOBA_HEADS * MOBA_V_ROWS
    vt_spec = lambda r: pl.BlockSpec((1, 1, r, ATT_TILE), lambda i: (i // n_kv, i % n_kv, 0, 0))
    vt_shape = lambda r: jax.ShapeDtypeStruct((B, n_kv, r, ATT_TILE), _BF16)
    rows_bf16 = jax.ShapeDtypeStruct((rows, CHUNK), _BF16)
    rows_f32 = jax.ShapeDtypeStruct((rows, CHUNK), _F32)
    km_rows = ROW_TILE // MOBA_BLOCK
    (dq, dk, dvt, dgs, mq, mk, mvt, mgs, kmean, gates) = pl.pallas_call(
        _proj_kernel,
        grid=(n_row_tiles,),
        in_specs=[row_spec(D_MODEL), _resident((1, D_MODEL)), _resident((D_MODEL, N_CHUNKS * CHUNK)),
                  tab_spec, tab_spec, tab_spec],
        out_specs=[row_spec(CHUNK), row_spec(CHUNK), vt_spec(dvt_rows), row_spec(CHUNK), row_spec(CHUNK),
                   row_spec(CHUNK), vt_spec(mvt_rows), row_spec(CHUNK),
                   pl.BlockSpec((1, km_rows, CHUNK), lambda i: (i, 0, 0)),
                   row_spec(4 * CHUNK)],
        out_shape=[rows_bf16, rows_bf16, vt_shape(dvt_rows), rows_f32, rows_f32, rows_bf16,
                   vt_shape(mvt_rows), rows_f32,
                   jax.ShapeDtypeStruct((n_row_tiles, km_rows, CHUNK), _F32),
                   jax.ShapeDtypeStruct((rows, 4 * CHUNK), _F32)],
        compiler_params=_params(1),
        name="proj",
    )(x2d, norm_g[0].reshape(1, D_MODEL), w_in[0].astype(_BF16), cosv, sin_lo, sin_hi)
    kmean = kmean.reshape(B * n_blk, CHUNK)

    n_groups = CHUNK // LANES
    qtab, ttab = _past_items(n_kv)
    seq_spec = pl.BlockSpec((S, LANES), lambda b, g, *_: (b, g))
    v_spec = lambda r: pl.BlockSpec((1, n_kv, r, ATT_TILE), lambda b, g, *_: (b, 0, g, 0))
    lam_spec = pl.BlockSpec((1, DIFF_HEAD_DIM), lambda b, g, *_: (0, 0))
    acc = lambda r: pltpu.VMEM((n_kv, r, ATT_TILE), _F32)
    run_max = pltpu.VMEM((n_kv, 1, ATT_TILE), _F32)
    score_buf = pltpu.VMEM((2, ATT_TILE, ATT_TILE), _F32)
    col_max = lambda n: pltpu.VMEM((2, n, 1, ATT_TILE), _F32)

    o_a = pl.pallas_call(
        _diff_kernel,
        grid_spec=pltpu.PrefetchScalarGridSpec(
            num_scalar_prefetch=2,
            grid=(B, n_groups),
            in_specs=[lam_spec, lam_spec, lam_spec, lam_spec, seq_spec, seq_spec, v_spec(DIFF_V_ROWS),
                      seq_spec, pl.BlockSpec((2 * DIFF_HEAD_DIM, 1), lambda b, g, *_: (0, 0))],
            out_specs=seq_spec,
            scratch_shapes=[acc(DIFF_V_ROWS), acc(DIFF_V_ROWS), run_max, run_max, score_buf, score_buf,
                            col_max(2)],
        ),
        out_shape=rows_bf16,
        compiler_params=_params(2),
        name="diff_attn",
    )(qtab, ttab, lambda_q1[0:1], lambda_k1[0:1], lambda_q2[0:1], lambda_k2[0:1], dq, dk, dvt, dgs,
      subln_g[0].reshape(2 * DIFF_HEAD_DIM, 1))

    o_b = pl.pallas_call(
        _moba_kernel,
        grid_spec=pltpu.PrefetchScalarGridSpec(
            num_scalar_prefetch=2,
            grid=(B, n_groups),
            in_specs=[seq_spec, seq_spec, v_spec(2 * MOBA_V_ROWS),
                      pl.BlockSpec((n_blk, LANES), lambda b, g, *_: (b, g)), seq_spec],
            out_specs=seq_spec,
            scratch_shapes=[acc(MOBA_V_ROWS), acc(MOBA_V_ROWS), run_max, run_max, score_buf, score_buf,
                            col_max(2 * (ATT_TILE // MOBA_BLOCK)),
                            pltpu.VMEM((2, n_kv, n_blk, ATT_TILE), _F32),
                            pltpu.VMEM((2, S, LANES), _BF16)],
        ),
        out_shape=rows_bf16,
        compiler_params=_params(2),
        name="moba_attn",
    )(qtab, ttab, mq, mk, mvt, kmean, mgs)

    out = pl.pallas_call(
        _out_kernel,
        grid=(n_row_tiles,),
        in_specs=[row_spec(CHUNK), row_spec(CHUNK), row_spec(4 * CHUNK), row_spec(D_MODEL),
                  row_spec(PLE_DIM), _resident((DIFF_WIDTH, D_MODEL)), _resident((MOBA_WIDTH, D_MODEL)),
                  _resident((D_MODEL, D_MODEL)), _resident((D_MODEL, D_MODEL)),
                  _resident((PLE_DIM, D_MODEL)), _resident((1, D_MODEL))],
        out_specs=row_spec(D_MODEL),
        out_shape=jax.ShapeDtypeStruct((rows, D_MODEL), _F32),
        scratch_shapes=[pltpu.VMEM(s, _BF16) for s in ((DIFF_WIDTH, D_MODEL), (MOBA_WIDTH, D_MODEL),
                                                      (D_MODEL, D_MODEL), (D_MODEL, D_MODEL),
                                                      (PLE_DIM, D_MODEL))],
        compiler_params=_params(1),
        name="out_proj",
    )(o_a, o_b, gates, x2d, p[0].reshape(rows, PLE_DIM), w_branch_diff[0], w_branch_moba[0],
      w_out[0], w_ple_gate[0], w_ple[0], final_g.reshape(1, D_MODEL))
    return out.reshape(B, S, D_MODEL)
```

```python
import collections
import functools
import math

import jax
import jax.numpy as jnp
import numpy as np
from jax import lax
from jax.experimental import pallas as pl
from jax.experimental.pallas import tpu as pltpu

D_MODEL = 1024
PLE_DIM = 256
DIFF_HEADS = 4
DIFF_HEAD_DIM = 64
DIFF_WIDTH = DIFF_HEADS * 2 * DIFF_HEAD_DIM
MOBA_HEADS = 8
MOBA_HEAD_DIM = 64
MOBA_WIDTH = MOBA_HEADS * MOBA_HEAD_DIM
MOBA_BLOCK = 256
MOBA_TOPK = 3
ROT_DIM = 16
ROPE_THETA = 500000.0
EPS = 1e-6
SUBLN_EPS = 1e-5
LAM_INIT = 0.8 - 0.6 * math.exp(-0.3 * 0)

LANES = 128
MXU_COLS = 256
BF16_ROWS = 16
CHUNK = 512
N_CHUNKS = 12
ROW_TILE = 512
ATT_TILE = 512
PIPE_UNROLL = 8
VMEM_LIMIT = 56 * 1024 * 1024
_SCORE_SCALE = DIFF_HEAD_DIM ** -0.5 * math.log2(math.e)
DIFF_V_ROWS = 2 * DIFF_HEAD_DIM + BF16_ROWS
MOBA_V_ROWS = MOBA_HEAD_DIM + BF16_ROWS
assert DIFF_HEAD_DIM == MOBA_HEAD_DIM and ATT_TILE % MOBA_BLOCK == 0 and ROW_TILE == ATT_TILE

_F32 = jnp.float32
_BF16 = jnp.bfloat16
_NT = (((1,), (1,)), ((), ()))


def _rope_rows(a, cosv, sin_lo, sin_hi):
    outs = []
    for j in range(CHUNK // LANES):
        xs = a[:, LANES * j:LANES * (j + 1)]
        up = pltpu.roll(xs, LANES - ROT_DIM // 2, 1)
        dn = pltpu.roll(xs, ROT_DIM // 2, 1)
        outs.append(xs * cosv + up * sin_lo + dn * sin_hi)
    return jnp.concatenate(outs, axis=1)


def _proj_kernel(x_ref, g_ref, w_ref, cos_ref, slo_ref, shi_ref,
                 dq_ref, dk_ref, dvt_ref, dg_ref, mq_ref, mk_ref, mvt_ref, mg_ref,
                 km_ref, gates_ref):
    x = x_ref[...]
    ms = jnp.mean(x * x, axis=-1, keepdims=True)
    h = (x * lax.rsqrt(ms + EPS) * g_ref[...]).astype(_BF16)
    cosv, sin_lo, sin_hi = cos_ref[...], slo_ref[...], shi_ref[...]

    def proj(c):
        return jnp.dot(h, w_ref[:, CHUNK * c:CHUNK * (c + 1)], preferred_element_type=_F32)

    def store_transposed(ref, a, head_rows):
        at = a.T.astype(_BF16)
        ones = jnp.ones((BF16_ROWS, ROW_TILE), _BF16)
        stride = head_rows + BF16_ROWS
        for g in range(CHUNK // head_rows):
            ref[0, 0, stride * g:stride * g + head_rows, :] = at[head_rows * g:head_rows * (g + 1)]
            ref[0, 0, stride * g + head_rows:stride * (g + 1), :] = ones

    dq_ref[...] = (_rope_rows(proj(0), cosv, sin_lo, sin_hi) * _SCORE_SCALE).astype(_BF16)
    dk_ref[...] = _rope_rows(proj(1), cosv, sin_lo, sin_hi).astype(_BF16)
    store_transposed(dvt_ref, proj(2), 2 * DIFF_HEAD_DIM)
    dg_ref[...] = jax.nn.silu(proj(3))
    mq_ref[...] = _rope_rows(proj(4), cosv, sin_lo, sin_hi)
    mk = _rope_rows(proj(5), cosv, sin_lo, sin_hi)
    mk_ref[...] = mk.astype(_BF16)
    for j in range(ROW_TILE // MOBA_BLOCK):
        km_ref[0, j:j + 1, :] = jnp.mean(mk[MOBA_BLOCK * j:MOBA_BLOCK * (j + 1)], axis=0, keepdims=True)
    store_transposed(mvt_ref, proj(6), MOBA_HEAD_DIM)
    mg_ref[...] = jax.nn.silu(proj(7))
    for c in range(8, N_CHUNKS):
        gates_ref[:, CHUNK * (c - 8):CHUNK * (c - 7)] = jax.nn.sigmoid(proj(c))


def _split_heads(q):
    lane = lax.broadcasted_iota(jnp.int32, q.shape, 1)
    zero = jnp.zeros_like(q)
    return jnp.where(lane < 64, q, zero), jnp.where(lane >= 64, q, zero)


def _tile_rows(i):
    return pl.ds(pl.multiple_of(i * ATT_TILE, ATT_TILE), ATT_TILE)


def _past_items(n_q):
    items = [(qi, t) for qi in range(1, n_q) for t in range(qi)]
    assert len(items) % PIPE_UNROLL == 0 and PIPE_UNROLL % 2 == 0
    items.append(items[-1])
    return (np.array([i[0] for i in items], np.int32), np.array([i[1] for i in items], np.int32))


_Stream = collections.namedtuple("_Stream", "s_ref acc_ref m_ref")


def _run_attention(qtab_ref, ttab_ref, streams, n_seg, k_ref, cm_ref, q_of, vt_of, bias_of,
                   interleave):
    n_q = streams[0].acc_ref.shape[0]
    seg_rows = ATT_TILE // n_seg
    assert interleave in ("part", "stream") and seg_rows % MXU_COLS == 0

    def softmax(st, qi, tile, slot, first):
        stream = streams[st]
        biases = [bias_of(st, qi, tile, j) for j in range(n_seg)]
        cands = [cm_ref[slot, st * n_seg + j] if b is None else cm_ref[slot, st * n_seg + j] + b
                 for j, b in enumerate(biases)]
        m_old = None if first else stream.m_ref[qi]
        m_new = functools.reduce(jnp.maximum, cands if first else [m_old] + cands)
        stream.m_ref[qi] = m_new
        p = [jnp.exp2(stream.s_ref[slot, seg_rows * j:seg_rows * (j + 1), :]
                      - (m_new if b is None else m_new - b)).astype(_BF16)
             for j, b in enumerate(biases)]
        p = p[0] if n_seg == 1 else jnp.concatenate(p, axis=0)
        return p, (None if first else jnp.exp2(m_old - m_new))

    def pieces(diagonal):
        if diagonal:
            return [(c, MXU_COLS, c + MXU_COLS) for c in range(0, ATT_TILE, MXU_COLS)]
        width = MXU_COLS if interleave == "part" else ATT_TILE
        return [(c, width, ATT_TILE) for c in range(0, ATT_TILE, width)]

    def qk_piece(st, k, qi, slot, diagonal, c0, width, keys):
        cols = slice(c0, c0 + width)
        rows = pl.ds(pl.multiple_of(qi * ATT_TILE + c0, width), width)
        s = lax.dot_general(k[:keys], q_of(st, rows), _NT, preferred_element_type=_F32)
        if diagonal:
            kpos = lax.broadcasted_iota(jnp.int32, s.shape, 0)
            qpos = lax.broadcasted_iota(jnp.int32, s.shape, 1) + c0
            s = jnp.where(kpos <= qpos, s, -jnp.inf)
        s_ref = streams[st].s_ref
        s_ref[slot, :keys, cols] = s
        if keys < ATT_TILE:
            s_ref[slot, keys:, cols] = jnp.full((ATT_TILE - keys, width), -jnp.inf, _F32)
        for j in range(n_seg):
            seg = s[seg_rows * j:seg_rows * (j + 1)]
            cm_ref[slot, st * n_seg + j, :, cols] = (
                jnp.max(seg, axis=0, keepdims=True) if seg.shape[0]
                else jnp.full((1, width), -jnp.inf, _F32))

    def pv_piece(st, vt, p, alpha, qi, c0, width, keys):
        cols = slice(c0, c0 + width)
        acc_ref = streams[st].acc_ref
        pv = jnp.dot(vt[:, :keys], p[:keys, cols], preferred_element_type=_F32)
        acc_ref[qi, :, cols] = pv if alpha is None else alpha[:, cols] * acc_ref[qi, :, cols] + pv

    def step(nxt, cur, first=False):
        k = None if nxt is None else k_ref[_tile_rows(nxt[1]), :]
        for st in range(len(streams)):
            qk_ops, pv_ops = [], []
            if nxt is not None:
                qk_ops = [functools.partial(qk_piece, st, k, nxt[0], nxt[2], nxt[3], *pc)
                          for pc in pieces(nxt[3])]
            if cur is not None:
                p, alpha = softmax(st, cur[0], cur[1], cur[2], first)
                vt = vt_of(st, cur[1])
                pv_ops = [functools.partial(pv_piece, st, vt, p, alpha, cur[0], *pc)
                          for pc in pieces(first)]
            if interleave == "part":
                ops = [op for pair in zip(qk_ops, pv_ops) for op in pair]
                ops += qk_ops[len(pv_ops):] + pv_ops[len(qk_ops):]
            else:
                ops = qk_ops + pv_ops
            for op in ops:
                op()

    step((0, 0, 0, True), None)

    def diag_items(j, _):
        for u in range(2):
            t = 2 * j + u
            step((t + 1, t + 1, (u + 1) % 2, True), (t, t, u), first=True)
        return 0

    lax.fori_loop(0, n_q // 2 - 1, diag_items, 0)
    step((n_q - 1, n_q - 1, 1, True), (n_q - 2, n_q - 2, 0), first=True)
    step((qtab_ref[0], ttab_ref[0], 0, False), (n_q - 1, n_q - 1, 1), first=True)

    def past_items(j, _):
        for u in range(PIPE_UNROLL):
            t = PIPE_UNROLL * j + u
            step((qtab_ref[t + 1], ttab_ref[t + 1], (u + 1) % 2, False),
                 (qtab_ref[t], ttab_ref[t], u % 2))
        return 0

    lax.fori_loop(0, n_q * (n_q - 1) // (2 * PIPE_UNROLL), past_items, 0)


def _diff_kernel(qtab_ref, ttab_ref, lq1_ref, lk1_ref, lq2_ref, lk2_ref, q_ref, k_ref, vt_ref,
                 dg_ref, sg_ref, o_ref, acc1_ref, acc2_ref, m1_ref, m2_ref, s1_ref, s2_ref, cm_ref):
    n_q = acc1_ref.shape[0]
    streams = (_Stream(s1_ref, acc1_ref, m1_ref), _Stream(s2_ref, acc2_ref, m2_ref))
    _run_attention(
        qtab_ref, ttab_ref, streams, 1, k_ref, cm_ref,
        q_of=lambda st, rows: _split_heads(q_ref[rows, :])[st],
        vt_of=lambda st, tile: vt_ref[0, tile],
        bias_of=lambda st, qi, tile, j: None,
        interleave="stream")

    dv = 2 * DIFF_HEAD_DIM
    lam = (jnp.exp(jnp.sum(lq1_ref[...] * lk1_ref[...], axis=-1, keepdims=True))
           - jnp.exp(jnp.sum(lq2_ref[...] * lk2_ref[...], axis=-1, keepdims=True)) + LAM_INIT)

    def finalize(qi, _):
        ot = (acc1_ref[qi, :dv] / acc1_ref[qi, dv:dv + 1]
              - lam * (acc2_ref[qi, :dv] / acc2_ref[qi, dv:dv + 1]))
        ms = jnp.mean(ot * ot, axis=0, keepdims=True)
        ot = ot * lax.rsqrt(ms + SUBLN_EPS) * sg_ref[...] * (1.0 - LAM_INIT)
        o_ref[_tile_rows(qi), :] = (ot.T * dg_ref[_tile_rows(qi), :]).astype(_BF16)
        return 0

    lax.fori_loop(0, n_q // 2, lambda j, c: finalize(2 * j + 1, finalize(2 * j, c)), 0)


def _moba_kernel(qtab_ref, ttab_ref, q_ref, k_ref, vt_ref, km_ref, mg_ref, o_ref,
                 acca_ref, accb_ref, ma_ref, mb_ref, sa_ref, sb_ref, cm_ref, bias_ref, qs_ref):
    n_q = acca_ref.shape[0]
    nb = km_ref.shape[0]
    per_tile = ATT_TILE // MOBA_BLOCK
    km_both = jnp.concatenate(_split_heads(km_ref[...]), axis=0)
    blk = lax.broadcasted_iota(jnp.int32, (nb, ATT_TILE), 0)
    col = lax.broadcasted_iota(jnp.int32, (nb, ATT_TILE), 1)
    col_blk = lax.shift_right_logical(col, MOBA_BLOCK.bit_length() - 1)

    def gate(qi, _):
        q = q_ref[_tile_rows(qi), :]
        own = per_tile * qi + col_blk
        g_both = lax.dot_general(km_both, q, _NT, preferred_element_type=_F32,
                                 precision=lax.Precision.HIGHEST)
        for hh in range(2):
            g = jnp.where(blk < own, g_both[nb * hh:nb * (hh + 1)], -jnp.inf)
            sel = jnp.zeros(g.shape, jnp.bool_)
            for _ in range(MOBA_TOPK):
                mx = jnp.max(g, axis=0, keepdims=True)
                first_idx = jnp.min(jnp.where(g == mx, blk, nb), axis=0, keepdims=True)
                pick = (blk == first_idx) & (mx > -jnp.inf)
                sel = sel | pick
                g = jnp.where(pick, -jnp.inf, g)
            bias_ref[hh, qi] = jnp.where(sel | (blk == own), 0.0, -jnp.inf)
        for hh, qh in enumerate(_split_heads((q * _SCORE_SCALE).astype(_BF16))):
            qs_ref[hh, _tile_rows(qi), :] = qh
        return 0

    lax.fori_loop(0, n_q // 2, lambda j, c: gate(2 * j + 1, gate(2 * j, c)), 0)

    streams = (_Stream(sa_ref, acca_ref, ma_ref), _Stream(sb_ref, accb_ref, mb_ref))
    _run_attention(
        qtab_ref, ttab_ref, streams, per_tile, k_ref, cm_ref,
        q_of=lambda st, rows: qs_ref[st, rows, :],
        vt_of=lambda st, tile: vt_ref[0, tile, MOBA_V_ROWS * st:MOBA_V_ROWS * (st + 1), :],
        bias_of=lambda st, qi, tile, j: bias_ref[st, qi, pl.ds(per_tile * tile + j, 1), :],
        interleave="part")

    dh = MOBA_HEAD_DIM

    def finalize(qi, _):
        ot = jnp.concatenate([acca_ref[qi, :dh] / acca_ref[qi, dh:dh + 1],
                              accb_ref[qi, :dh] / accb_ref[qi, dh:dh + 1]], axis=0)
        o_ref[_tile_rows(qi), :] = (ot.T * mg_ref[_tile_rows(qi), :]).astype(_BF16)
        return 0

    lax.fori_loop(0, n_q // 2, lambda j, c: finalize(2 * j + 1, finalize(2 * j, c)), 0)


def _out_kernel(oa_ref, ob_ref, gates_ref, x_ref, p_ref, wbd_ref, wbm_ref, wout_ref, wpg_ref,
                wple_ref, fg_ref, o_ref, wbd_bf, wbm_bf, wout_bf, wpg_bf, wple_bf):
    @pl.when(pl.program_id(0) == 0)
    def _():
        for src, dst in ((wbd_ref, wbd_bf), (wbm_ref, wbm_bf), (wout_ref, wout_bf),
                         (wpg_ref, wpg_bf), (wple_ref, wple_bf)):
            dst[...] = src[...].astype(_BF16)

    ya = jnp.dot(oa_ref[...], wbd_bf[...], preferred_element_type=_F32)
    yb = jnp.dot(ob_ref[...], wbm_bf[...], preferred_element_type=_F32)
    merged = gates_ref[:, :D_MODEL] * ya + gates_ref[:, D_MODEL:] * yb
    x1 = x_ref[...] + jnp.dot(merged.astype(_BF16), wout_bf[...], preferred_element_type=_F32)
    t = jnp.dot(x1.astype(_BF16), wpg_bf[...], preferred_element_type=_F32)
    pe = jnp.dot(p_ref[...].astype(_BF16), wple_bf[...], preferred_element_type=_F32)
    x2 = x1 + jax.nn.sigmoid(t) * pe
    ms = jnp.mean(x2 * x2, axis=-1, keepdims=True)
    o_ref[...] = x2 * lax.rsqrt(ms + EPS) * fg_ref[...]


def _rope_lane_tables(seq):
    half = ROT_DIM // 2
    inv = ROPE_THETA ** (-np.arange(0, ROT_DIM, 2, dtype=np.float64) / ROT_DIM)
    ang = np.arange(seq, dtype=np.float64)[:, None] * inv[None, :]
    cos, sin = np.cos(ang), np.sin(ang)
    pad = MOBA_HEAD_DIM - ROT_DIM
    one, zero = np.ones((seq, pad)), np.zeros((seq, pad))
    zh = np.zeros((seq, half))
    cos64 = np.concatenate([cos, cos, one], axis=1)
    slo64 = np.concatenate([-sin, zh, zero], axis=1)
    shi64 = np.concatenate([zh, sin, zero], axis=1)
    return tuple(jnp.asarray(np.tile(t, (1, LANES // MOBA_HEAD_DIM)), _F32)
                 for t in (cos64, slo64, shi64))


def _resident(shape):
    return pl.BlockSpec(shape, lambda *_: (0,) * len(shape), pipeline_mode=pl.Buffered(1))


def _params(n_axes):
    return pltpu.CompilerParams(dimension_semantics=("arbitrary",) * n_axes,
                                vmem_limit_bytes=VMEM_LIMIT)


def kernel(x, p, norm_g, w_in, lambda_q1, lambda_k1, lambda_q2, lambda_k2, subln_g,
           w_branch_diff, w_branch_moba, w_out, w_ple, w_ple_gate, final_g):
    B, S, _ = x.shape
    assert w_in.shape == (1, D_MODEL, N_CHUNKS * CHUNK) and S % (2 * ATT_TILE) == 0
    rows = B * S
    n_row_tiles = rows // ROW_TILE
    n_kv = S // ATT_TILE
    n_blk = S // MOBA_BLOCK
    x2d = x.reshape(rows, D_MODEL)
    cosv, sin_lo, sin_hi = _rope_lane_tables(S)

    row_spec = lambda w: pl.BlockSpec((ROW_TILE, w), lambda i: (i, 0))
    tab_spec = pl.BlockSpec((ROW_TILE, LANES), lambda i: (i % n_kv, 0))
    dvt_rows, mvt_rows = DIFF_HEADS * DIFF_V_ROWS, MOBA_HEADS * MOBA_V_ROWS
    vt_spec = lambda r: pl.BlockSpec((1, 1, r, ATT_TILE), lambda i: (i // n_kv, i % n_kv, 0, 0))
    vt_shape = lambda r: jax.ShapeDtypeStruct((B, n_kv, r, ATT_TILE), _BF16)
    rows_bf16 = jax.ShapeDtypeStruct((rows, CHUNK), _BF16)
    rows_f32 = jax.ShapeDtypeStruct((rows, CHUNK), _F32)
    km_rows = ROW_TILE // MOBA_BLOCK
    (dq, dk, dvt, dgs, mq, mk, mvt, mgs, kmean, gates) = pl.pallas_call(
        _proj_kernel,
        grid=(n_row_tiles,),
        in_specs=[row_spec(D_MODEL), _resident((1, D_MODEL)), _resident((D_MODEL, N_CHUNKS * CHUNK)),
                  tab_spec, tab_spec, tab_spec],
        out_specs=[row_spec(CHUNK), row_spec(CHUNK), vt_spec(dvt_rows), row_spec(CHUNK), row_spec(CHUNK),
                   row_spec(CHUNK), vt_spec(mvt_rows), row_spec(CHUNK),
                   pl.BlockSpec((1, km_rows, CHUNK), lambda i: (i, 0, 0)),
                   row_spec(4 * CHUNK)],
        out_shape=[rows_bf16, rows_bf16, vt_shape(dvt_rows), rows_f32, rows_f32, rows_bf16,
                   vt_shape(mvt_rows), rows_f32,
                   jax.ShapeDtypeStruct((n_row_tiles, km_rows, CHUNK), _F32),
                   jax.ShapeDtypeStruct((rows, 4 * CHUNK), _F32)],
        compiler_params=_params(1),
        name="proj",
    )(x2d, norm_g[0].reshape(1, D_MODEL), w_in[0].astype(_BF16), cosv, sin_lo, sin_hi)
    kmean = kmean.reshape(B * n_blk, CHUNK)

    n_groups = CHUNK // LANES
    qtab, ttab = _past_items(n_kv)
    seq_spec = pl.BlockSpec((S, LANES), lambda b, g, *_: (b, g))
    v_spec = lambda r: pl.BlockSpec((1, n_kv, r, ATT_TILE), lambda b, g, *_: (b, 0, g, 0))
    lam_spec = pl.BlockSpec((1, DIFF_HEAD_DIM), lambda b, g, *_: (0, 0))
    acc = lambda r: pltpu.VMEM((n_kv, r, ATT_TILE), _F32)
    run_max = pltpu.VMEM((n_kv, 1, ATT_TILE), _F32)
    score_buf = pltpu.VMEM((2, ATT_TILE, ATT_TILE), _F32)
    col_max = lambda n: pltpu.VMEM((2, n, 1, ATT_TILE), _F32)

    o_a = pl.pallas_call(
        _diff_kernel,
        grid_spec=pltpu.PrefetchScalarGridSpec(
            num_scalar_prefetch=2,
            grid=(B, n_groups),
            in_specs=[lam_spec, lam_spec, lam_spec, lam_spec, seq_spec, seq_spec, v_spec(DIFF_V_ROWS),
                      seq_spec, pl.BlockSpec((2 * DIFF_HEAD_DIM, 1), lambda b, g, *_: (0, 0))],
            out_specs=seq_spec,
            scratch_shapes=[acc(DIFF_V_ROWS), acc(DIFF_V_ROWS), run_max, run_max, score_buf, score_buf,
                            col_max(2)],
        ),
        out_shape=rows_bf16,
        compiler_params=_params(2),
        name="diff_attn",
    )(qtab, ttab, lambda_q1[0:1], lambda_k1[0:1], lambda_q2[0:1], lambda_k2[0:1], dq, dk, dvt, dgs,
      subln_g[0].reshape(2 * DIFF_HEAD_DIM, 1))

    o_b = pl.pallas_call(
        _moba_kernel,
        grid_spec=pltpu.PrefetchScalarGridSpec(
            num_scalar_prefetch=2,
            grid=(B, n_groups),
            in_specs=[seq_spec, seq_spec, v_spec(2 * MOBA_V_ROWS),
                      pl.BlockSpec((n_blk, LANES), lambda b, g, *_: (b, g)), seq_spec],
            out_specs=seq_spec,
            scratch_shapes=[acc(MOBA_V_ROWS), acc(MOBA_V_ROWS), run_max, run_max, score_buf, score_buf,
                            col_max(2 * (ATT_TILE // MOBA_BLOCK)),
                            pltpu.VMEM((2, n_kv, n_blk, ATT_TILE), _F32),
                            pltpu.VMEM((2, S, LANES), _BF16)],
        ),
        out_shape=rows_bf16,
        compiler_params=_params(2),
        name="moba_attn",
    )(qtab, ttab, mq, mk, mvt, kmean, mgs)

    out = pl.pallas_call(
        _out_kernel,
        grid=(n_row_tiles,),
        in_specs=[row_spec(CHUNK), row_spec(CHUNK), row_spec(4 * CHUNK), row_spec(D_MODEL),
                  row_spec(PLE_DIM), _resident((DIFF_WIDTH, D_MODEL)), _resident((MOBA_WIDTH, D_MODEL)),
                  _resident((D_MODEL, D_MODEL)), _resident((D_MODEL, D_MODEL)),
                  _resident((PLE_DIM, D_MODEL)), _resident((1, D_MODEL))],
        out_specs=row_spec(D_MODEL),
        out_shape=jax.ShapeDtypeStruct((rows, D_MODEL), _F32),
        scratch_shapes=[pltpu.VMEM(s, _BF16) for s in ((DIFF_WIDTH, D_MODEL), (MOBA_WIDTH, D_MODEL),
                                                      (D_MODEL, D_MODEL), (D_MODEL, D_MODEL),
                                                      (PLE_DIM, D_MODEL))],
        compiler_params=_params(1),
        name="out_proj",
    )(o_a, o_b, gates, x2d, p[0].reshape(rows, PLE_DIM), w_branch_diff[0], w_branch_moba[0],
      w_out[0], w_ple_gate[0], w_ple[0], final_g.reshape(1, D_MODEL))
    return out.reshape(B, S, D_MODEL)
```

```python
import collections
import functools
import math

import jax
import jax.numpy as jnp
import numpy as np
from jax import lax
from jax.experimental import pallas as pl
from jax.experimental.pallas import tpu as pltpu

D_MODEL = 1024
PLE_DIM = 256
DIFF_HEADS = 4
DIFF_HEAD_DIM = 64
DIFF_WIDTH = DIFF_HEADS * 2 * DIFF_HEAD_DIM
MOBA_HEADS = 8
MOBA_HEAD_DIM = 64
MOBA_WIDTH = MOBA_HEADS * MOBA_HEAD_DIM
MOBA_BLOCK = 256
MOBA_TOPK = 3
ROT_DIM = 16
ROPE_THETA = 500000.0
EPS = 1e-6
SUBLN_EPS = 1e-5
LAM_INIT = 0.8 - 0.6 * math.exp(-0.3 * 0)

LANES = 128
MXU_COLS = 256
BF16_ROWS = 16
CHUNK = 512
N_CHUNKS = 12
ROW_TILE = 512
ATT_TILE = 512
PIPE_UNROLL = 12
VMEM_LIMIT = 56 * 1024 * 1024
_SCORE_SCALE = DIFF_HEAD_DIM ** -0.5 * math.log2(math.e)
DIFF_V_ROWS = 2 * DIFF_HEAD_DIM + BF16_ROWS
MOBA_V_ROWS = MOBA_HEAD_DIM + BF16_ROWS
assert DIFF_HEAD_DIM == MOBA_HEAD_DIM and ATT_TILE % MOBA_BLOCK == 0 and ROW_TILE == ATT_TILE

_F32 = jnp.float32
_BF16 = jnp.bfloat16
_NT = (((1,), (1,)), ((), ()))


def _rope_rows(a, cosv, sin_lo, sin_hi):
    outs = []
    for j in range(CHUNK // LANES):
        xs = a[:, LANES * j:LANES * (j + 1)]
        up = pltpu.roll(xs, LANES - ROT_DIM // 2, 1)
        dn = pltpu.roll(xs, ROT_DIM // 2, 1)
        outs.append(xs * cosv + up * sin_lo + dn * sin_hi)
    return jnp.concatenate(outs, axis=1)


def _proj_kernel(x_ref, g_ref, w_ref, cos_ref, slo_ref, shi_ref,
                 dq_ref, dk_ref, dvt_ref, dg_ref, mq_ref, mk_ref, mvt_ref, mg_ref,
                 km_ref, gates_ref):
    x = x_ref[...]
    ms = jnp.mean(x * x, axis=-1, keepdims=True)
    h = (x * lax.rsqrt(ms + EPS) * g_ref[...]).astype(_BF16)
    cosv, sin_lo, sin_hi = cos_ref[...], slo_ref[...], shi_ref[...]

    def proj(c):
        return jnp.dot(h, w_ref[:, CHUNK * c:CHUNK * (c + 1)], preferred_element_type=_F32)

    def store_transposed(ref, a, head_rows):
        at = a.T.astype(_BF16)
        ones = jnp.ones((BF16_ROWS, ROW_TILE), _BF16)
        stride = head_rows + BF16_ROWS
        for g in range(CHUNK // head_rows):
            ref[0, 0, stride * g:stride * g + head_rows, :] = at[head_rows * g:head_rows * (g + 1)]
            ref[0, 0, stride * g + head_rows:stride * (g + 1), :] = ones

    dq_ref[...] = (_rope_rows(proj(0), cosv, sin_lo, sin_hi) * _SCORE_SCALE).astype(_BF16)
    dk_ref[...] = _rope_rows(proj(1), cosv, sin_lo, sin_hi).astype(_BF16)
    store_transposed(dvt_ref, proj(2), 2 * DIFF_HEAD_DIM)
    dg_ref[...] = jax.nn.silu(proj(3))
    mq_ref[...] = _rope_rows(proj(4), cosv, sin_lo, sin_hi)
    mk = _rope_rows(proj(5), cosv, sin_lo, sin_hi)
    mk_ref[...] = mk.astype(_BF16)
    for j in range(ROW_TILE // MOBA_BLOCK):
        km_ref[0, j:j + 1, :] = jnp.mean(mk[MOBA_BLOCK * j:MOBA_BLOCK * (j + 1)], axis=0, keepdims=True)
    store_transposed(mvt_ref, proj(6), MOBA_HEAD_DIM)
    mg_ref[...] = jax.nn.silu(proj(7))
    for c in range(8, N_CHUNKS):
        gates_ref[:, CHUNK * (c - 8):CHUNK * (c - 7)] = jax.nn.sigmoid(proj(c))


def _split_heads(q):
    lane = lax.broadcasted_iota(jnp.int32, q.shape, 1)
    zero = jnp.zeros_like(q)
    return jnp.where(lane < 64, q, zero), jnp.where(lane >= 64, q, zero)


def _tile_rows(i):
    return pl.ds(pl.multiple_of(i * ATT_TILE, ATT_TILE), ATT_TILE)


def _past_items(n_q):
    items = [(qi, t) for qi in range(1, n_q) for t in range(qi)]
    assert len(items) % PIPE_UNROLL == 0 and PIPE_UNROLL % 2 == 0
    items.append(items[-1])
    return (np.array([i[0] for i in items], np.int32), np.array([i[1] for i in items], np.int32))


_Stream = collections.namedtuple("_Stream", "s_ref acc_ref m_ref")


def _run_attention(qtab_ref, ttab_ref, streams, n_seg, k_ref, cm_ref, q_of, vt_of, bias_of,
                   interleave):
    n_q = streams[0].acc_ref.shape[0]
    seg_rows = ATT_TILE // n_seg
    assert interleave in ("part", "stream") and seg_rows % MXU_COLS == 0

    def softmax(st, qi, tile, slot, first):
        stream = streams[st]
        biases = [bias_of(st, qi, tile, j) for j in range(n_seg)]
        cands = [cm_ref[slot, st * n_seg + j] if b is None else cm_ref[slot, st * n_seg + j] + b
                 for j, b in enumerate(biases)]
        m_old = None if first else stream.m_ref[qi]
        m_new = functools.reduce(jnp.maximum, cands if first else [m_old] + cands)
        stream.m_ref[qi] = m_new
        p = [jnp.exp2(stream.s_ref[slot, seg_rows * j:seg_rows * (j + 1), :]
                      - (m_new if b is None else m_new - b)).astype(_BF16)
             for j, b in enumerate(biases)]
        p = p[0] if n_seg == 1 else jnp.concatenate(p, axis=0)
        return p, (None if first else jnp.exp2(m_old - m_new))

    def pieces(diagonal):
        if diagonal:
            return [(c, MXU_COLS, c + MXU_COLS) for c in range(0, ATT_TILE, MXU_COLS)]
        width = MXU_COLS if interleave == "part" else ATT_TILE
        return [(c, width, ATT_TILE) for c in range(0, ATT_TILE, width)]

    def qk_piece(st, k, qi, slot, diagonal, c0, width, keys):
        cols = slice(c0, c0 + width)
        rows = pl.ds(pl.multiple_of(qi * ATT_TILE + c0, width), width)
        s = lax.dot_general(k[:keys], q_of(st, rows), _NT, preferred_element_type=_F32)
        if diagonal:
            kpos = lax.broadcasted_iota(jnp.int32, s.shape, 0)
            qpos = lax.broadcasted_iota(jnp.int32, s.shape, 1) + c0
            s = jnp.where(kpos <= qpos, s, -jnp.inf)
        s_ref = streams[st].s_ref
        s_ref[slot, :keys, cols] = s
        if keys < ATT_TILE:
            s_ref[slot, keys:, cols] = jnp.full((ATT_TILE - keys, width), -jnp.inf, _F32)
        for j in range(n_seg):
            seg = s[seg_rows * j:seg_rows * (j + 1)]
            cm_ref[slot, st * n_seg + j, :, cols] = (
                jnp.max(seg, axis=0, keepdims=True) if seg.shape[0]
                else jnp.full((1, width), -jnp.inf, _F32))

    def pv_piece(st, vt, p, alpha, qi, c0, width, keys):
        cols = slice(c0, c0 + width)
        acc_ref = streams[st].acc_ref
        pv = jnp.dot(vt[:, :keys], p[:keys, cols], preferred_element_type=_F32)
        acc_ref[qi, :, cols] = pv if alpha is None else alpha[:, cols] * acc_ref[qi, :, cols] + pv

    def step(nxt, cur, first=False):
        k = None if nxt is None else k_ref[_tile_rows(nxt[1]), :]
        for st in range(len(streams)):
            qk_ops, pv_ops = [], []
            if nxt is not None:
                qk_ops = [functools.partial(qk_piece, st, k, nxt[0], nxt[2], nxt[3], *pc)
                          for pc in pieces(nxt[3])]
            if cur is not None:
                p, alpha = softmax(st, cur[0], cur[1], cur[2], first)
                vt = vt_of(st, cur[1])
                pv_ops = [functools.partial(pv_piece, st, vt, p, alpha, cur[0], *pc)
                          for pc in pieces(first)]
            if interleave == "part":
                ops = [op for pair in zip(qk_ops, pv_ops) for op in pair]
                ops += qk_ops[len(pv_ops):] + pv_ops[len(qk_ops):]
            else:
                ops = qk_ops + pv_ops
            for op in ops:
                op()

    step((0, 0, 0, True), None)

    def diag_items(j, _):
        for u in range(2):
            t = 2 * j + u
            step((t + 1, t + 1, (u + 1) % 2, True), (t, t, u), first=True)
        return 0

    lax.fori_loop(0, n_q // 2 - 1, diag_items, 0)
    step((n_q - 1, n_q - 1, 1, True), (n_q - 2, n_q - 2, 0), first=True)
    step((qtab_ref[0], ttab_ref[0], 0, False), (n_q - 1, n_q - 1, 1), first=True)

    def past_items(j, _):
        for u in range(PIPE_UNROLL):
            t = PIPE_UNROLL * j + u
            step((qtab_ref[t + 1], ttab_ref[t + 1], (u + 1) % 2, False),
                 (qtab_ref[t], ttab_ref[t], u % 2))
        return 0

    lax.fori_loop(0, n_q * (n_q - 1) // (2 * PIPE_UNROLL), past_items, 0)


def _diff_kernel(qtab_ref, ttab_ref, lq1_ref, lk1_ref, lq2_ref, lk2_ref, q_ref, k_ref, vt_ref,
                 dg_ref, sg_ref, o_ref, acc1_ref, acc2_ref, m1_ref, m2_ref, s1_ref, s2_ref, cm_ref):
    n_q = acc1_ref.shape[0]
    streams = (_Stream(s1_ref, acc1_ref, m1_ref), _Stream(s2_ref, acc2_ref, m2_ref))
    _run_attention(
        qtab_ref, ttab_ref, streams, 1, k_ref, cm_ref,
        q_of=lambda st, rows: _split_heads(q_ref[rows, :])[st],
        vt_of=lambda st, tile: vt_ref[0, tile],
        bias_of=lambda st, qi, tile, j: None,
        interleave="stream")

    dv = 2 * DIFF_HEAD_DIM
    lam = (jnp.exp(jnp.sum(lq1_ref[...] * lk1_ref[...], axis=-1, keepdims=True))
           - jnp.exp(jnp.sum(lq2_ref[...] * lk2_ref[...], axis=-1, keepdims=True)) + LAM_INIT)

    def finalize(qi, _):
        ot = (acc1_ref[qi, :dv] / acc1_ref[qi, dv:dv + 1]
              - lam * (acc2_ref[qi, :dv] / acc2_ref[qi, dv:dv + 1]))
        ms = jnp.mean(ot * ot, axis=0, keepdims=True)
        ot = ot * lax.rsqrt(ms + SUBLN_EPS) * sg_ref[...] * (1.0 - LAM_INIT)
        o_ref[_tile_rows(qi), :] = (ot.T * dg_ref[_tile_rows(qi), :]).astype(_BF16)
        return 0

    lax.fori_loop(0, n_q // 2, lambda j, c: finalize(2 * j + 1, finalize(2 * j, c)), 0)


def _moba_kernel(qtab_ref, ttab_ref, q_ref, k_ref, vt_ref, km_ref, mg_ref, o_ref,
                 acca_ref, accb_ref, ma_ref, mb_ref, sa_ref, sb_ref, cm_ref, bias_ref, qs_ref):
    n_q = acca_ref.shape[0]
    nb = km_ref.shape[0]
    per_tile = ATT_TILE // MOBA_BLOCK
    km_both = jnp.concatenate(_split_heads(km_ref[...]), axis=0)
    blk = lax.broadcasted_iota(jnp.int32, (nb, ATT_TILE), 0)
    col = lax.broadcasted_iota(jnp.int32, (nb, ATT_TILE), 1)
    col_blk = lax.shift_right_logical(col, MOBA_BLOCK.bit_length() - 1)

    def gate(qi, _):
        q = q_ref[_tile_rows(qi), :]
        own = per_tile * qi + col_blk
        g_both = lax.dot_general(km_both, q, _NT, preferred_element_type=_F32,
                                 precision=lax.Precision.HIGHEST)
        for hh in range(2):
            g = jnp.where(blk < own, g_both[nb * hh:nb * (hh + 1)], -jnp.inf)
            sel = jnp.zeros(g.shape, jnp.bool_)
            for _ in range(MOBA_TOPK):
                mx = jnp.max(g, axis=0, keepdims=True)
                first_idx = jnp.min(jnp.where(g == mx, blk, nb), axis=0, keepdims=True)
                pick = (blk == first_idx) & (mx > -jnp.inf)
                sel = sel | pick
                g = jnp.where(pick, -jnp.inf, g)
            bias_ref[hh, qi] = jnp.where(sel | (blk == own), 0.0, -jnp.inf)
        for hh, qh in enumerate(_split_heads((q * _SCORE_SCALE).astype(_BF16))):
            qs_ref[hh, _tile_rows(qi), :] = qh
        return 0

    lax.fori_loop(0, n_q // 2, lambda j, c: gate(2 * j + 1, gate(2 * j, c)), 0)

    streams = (_Stream(sa_ref, acca_ref, ma_ref), _Stream(sb_ref, accb_ref, mb_ref))
    _run_attention(
        qtab_ref, ttab_ref, streams, per_tile, k_ref, cm_ref,
        q_of=lambda st, rows: qs_ref[st, rows, :],
        vt_of=lambda st, tile: vt_ref[0, tile, MOBA_V_ROWS * st:MOBA_V_ROWS * (st + 1), :],
        bias_of=lambda st, qi, tile, j: bias_ref[st, qi, pl.ds(per_tile * tile + j, 1), :],
        interleave="part")

    dh = MOBA_HEAD_DIM

    def finalize(qi, _):
        ot = jnp.concatenate([acca_ref[qi, :dh] / acca_ref[qi, dh:dh + 1],
                              accb_ref[qi, :dh] / accb_ref[qi, dh:dh + 1]], axis=0)
        o_ref[_tile_rows(qi), :] = (ot.T * mg_ref[_tile_rows(qi), :]).astype(_BF16)
        return 0

    lax.fori_loop(0, n_q // 2, lambda j, c: finalize(2 * j + 1, finalize(2 * j, c)), 0)


def _out_kernel(oa_ref, ob_ref, gates_ref, x_ref, p_ref, wbd_ref, wbm_ref, wout_ref, wpg_ref,
                wple_ref, fg_ref, o_ref, wbd_bf, wbm_bf, wout_bf, wpg_bf, wple_bf):
    @pl.when(pl.program_id(0) == 0)
    def _():
        for src, dst in ((wbd_ref, wbd_bf), (wbm_ref, wbm_bf), (wout_ref, wout_bf),
                         (wpg_ref, wpg_bf), (wple_ref, wple_bf)):
            dst[...] = src[...].astype(_BF16)

    ya = jnp.dot(oa_ref[...], wbd_bf[...], preferred_element_type=_F32)
    yb = jnp.dot(ob_ref[...], wbm_bf[...], preferred_element_type=_F32)
    merged = gates_ref[:, :D_MODEL] * ya + gates_ref[:, D_MODEL:] * yb
    x1 = x_ref[...] + jnp.dot(merged.astype(_BF16), wout_bf[...], preferred_element_type=_F32)
    t = jnp.dot(x1.astype(_BF16), wpg_bf[...], preferred_element_type=_F32)
    pe = jnp.dot(p_ref[...].astype(_BF16), wple_bf[...], preferred_element_type=_F32)
    x2 = x1 + jax.nn.sigmoid(t) * pe
    ms = jnp.mean(x2 * x2, axis=-1, keepdims=True)
    o_ref[...] = x2 * lax.rsqrt(ms + EPS) * fg_ref[...]


def _rope_lane_tables(seq):
    half = ROT_DIM // 2
    inv = ROPE_THETA ** (-np.arange(0, ROT_DIM, 2, dtype=np.float64) / ROT_DIM)
    ang = np.arange(seq, dtype=np.float64)[:, None] * inv[None, :]
    cos, sin = np.cos(ang), np.sin(ang)
    pad = MOBA_HEAD_DIM - ROT_DIM
    one, zero = np.ones((seq, pad)), np.zeros((seq, pad))
    zh = np.zeros((seq, half))
    cos64 = np.concatenate([cos, cos, one], axis=1)
    slo64 = np.concatenate([-sin, zh, zero], axis=1)
    shi64 = np.concatenate([zh, sin, zero], axis=1)
    return tuple(jnp.asarray(np.tile(t, (1, LANES // MOBA_HEAD_DIM)), _F32)
                 for t in (cos64, slo64, shi64))


def _resident(shape):
    return pl.BlockSpec(shape, lambda *_: (0,) * len(shape), pipeline_mode=pl.Buffered(1))


def _params(n_axes):
    return pltpu.CompilerParams(dimension_semantics=("arbitrary",) * n_axes,
                                vmem_limit_bytes=VMEM_LIMIT)


def kernel(x, p, norm_g, w_in, lambda_q1, lambda_k1, lambda_q2, lambda_k2, subln_g,
           w_branch_diff, w_branch_moba, w_out, w_ple, w_ple_gate, final_g):
    B, S, _ = x.shape
    assert w_in.shape == (1, D_MODEL, N_CHUNKS * CHUNK) and S % (2 * ATT_TILE) == 0
    rows = B * S
    n_row_tiles = rows // ROW_TILE
    n_kv = S // ATT_TILE
    n_blk = S // MOBA_BLOCK
    x2d = x.reshape(rows, D_MODEL)
    cosv, sin_lo, sin_hi = _rope_lane_tables(S)

    row_spec = lambda w: pl.BlockSpec((ROW_TILE, w), lambda i: (i, 0))
    tab_spec = pl.BlockSpec((ROW_TILE, LANES), lambda i: (i % n_kv, 0))
    dvt_rows, mvt_rows = DIFF_HEADS * DIFF_V_ROWS, MOBA_HEADS * MOBA_V_ROWS
    vt_spec = lambda r: pl.BlockSpec((1, 1, r, ATT_TILE), lambda i: (i // n_kv, i % n_kv, 0, 0))
    vt_shape = lambda r: jax.ShapeDtypeStruct((B, n_kv, r, ATT_TILE), _BF16)
    rows_bf16 = jax.ShapeDtypeStruct((rows, CHUNK), _BF16)
    rows_f32 = jax.ShapeDtypeStruct((rows, CHUNK), _F32)
    km_rows = ROW_TILE // MOBA_BLOCK
    (dq, dk, dvt, dgs, mq, mk, mvt, mgs, kmean, gates) = pl.pallas_call(
        _proj_kernel,
        grid=(n_row_tiles,),
        in_specs=[row_spec(D_MODEL), _resident((1, D_MODEL)), _resident((D_MODEL, N_CHUNKS * CHUNK)),
                  tab_spec, tab_spec, tab_spec],
        out_specs=[row_spec(CHUNK), row_spec(CHUNK), vt_spec(dvt_rows), row_spec(CHUNK), row_spec(CHUNK),
                   row_spec(CHUNK), vt_spec(mvt_rows), row_spec(CHUNK),
                   pl.BlockSpec((1, km_rows, CHUNK), lambda i: (i, 0, 0)),
                   row_spec(4 * CHUNK)],
        out_shape=[rows_bf16, rows_bf16, vt_shape(dvt_rows), rows_f32, rows_f32, rows_bf16,
                   vt_shape(mvt_rows), rows_f32,
                   jax.ShapeDtypeStruct((n_row_tiles, km_rows, CHUNK), _F32),
                   jax.ShapeDtypeStruct((rows, 4 * CHUNK), _F32)],
        compiler_params=_params(1),
        name="proj",
    )(x2d, norm_g[0].reshape(1, D_MODEL), w_in[0].astype(_BF16), cosv, sin_lo, sin_hi)
    kmean = kmean.reshape(B * n_blk, CHUNK)

    n_groups = CHUNK // LANES
    qtab, ttab = _past_items(n_kv)
    seq_spec = pl.BlockSpec((S, LANES), lambda b, g, *_: (b, g))
    v_spec = lambda r: pl.BlockSpec((1, n_kv, r, ATT_TILE), lambda b, g, *_: (b, 0, g, 0))
    lam_spec = pl.BlockSpec((1, DIFF_HEAD_DIM), lambda b, g, *_: (0, 0))
    acc = lambda r: pltpu.VMEM((n_kv, r, ATT_TILE), _F32)
    run_max = pltpu.VMEM((n_kv, 1, ATT_TILE), _F32)
    score_buf = pltpu.VMEM((2, ATT_TILE, ATT_TILE), _F32)
    col_max = lambda n: pltpu.VMEM((2, n, 1, ATT_TILE), _F32)

    o_a = pl.pallas_call(
        _diff_kernel,
        grid_spec=pltpu.PrefetchScalarGridSpec(
            num_scalar_prefetch=2,
            grid=(B, n_groups),
            in_specs=[lam_spec, lam_spec, lam_spec, lam_spec, seq_spec, seq_spec, v_spec(DIFF_V_ROWS),
                      seq_spec, pl.BlockSpec((2 * DIFF_HEAD_DIM, 1), lambda b, g, *_: (0, 0))],
            out_specs=seq_spec,
            scratch_shapes=[acc(DIFF_V_ROWS), acc(DIFF_V_ROWS), run_max, run_max, score_buf, score_buf,
                            col_max(2)],
        ),
        out_shape=rows_bf16,
        compiler_params=_params(2),
        name="diff_attn",
    )(qtab, ttab, lambda_q1[0:1], lambda_k1[0:1], lambda_q2[0:1], lambda_k2[0:1], dq, dk, dvt, dgs,
      subln_g[0].reshape(2 * DIFF_HEAD_DIM, 1))

    o_b = pl.pallas_call(
        _moba_kernel,
        grid_spec=pltpu.PrefetchScalarGridSpec(
            num_scalar_prefetch=2,
            grid=(B, n_groups),
            in_specs=[seq_spec, seq_spec, v_spec(2 * MOBA_V_ROWS),
                      pl.BlockSpec((n_blk, LANES), lambda b, g, *_: (b, g)), seq_spec],
            out_specs=seq_spec,
            scratch_shapes=[acc(MOBA_V_ROWS), acc(MOBA_V_ROWS), run_max, run_max, score_buf, score_buf,
                            col_max(2 * (ATT_TILE // MOBA_BLOCK)),
                            pltpu.VMEM((2, n_kv, n_blk, ATT_TILE), _F32),
                            pltpu.VMEM((2, S, LANES), _BF16)],
        ),
        out_shape=rows_bf16,
        compiler_params=_params(2),
        name="moba_attn",
    )(qtab, ttab, mq, mk, mvt, kmean, mgs)

    out = pl.pallas_call(
        _out_kernel,
        grid=(n_row_tiles,),
        in_specs=[row_spec(CHUNK), row_spec(CHUNK), row_spec(4 * CHUNK), row_spec(D_MODEL),
                  row_spec(PLE_DIM), _resident((DIFF_WIDTH, D_MODEL)), _resident((MOBA_WIDTH, D_MODEL)),
                  _resident((D_MODEL, D_MODEL)), _resident((D_MODEL, D_MODEL)),
                  _resident((PLE_DIM, D_MODEL)), _resident((1, D_MODEL))],
        out_specs=row_spec(D_MODEL),
        out_shape=jax.ShapeDtypeStruct((rows, D_MODEL), _F32),
        scratch_shapes=[pltpu.VMEM(s, _BF16) for s in ((DIFF_WIDTH, D_MODEL), (MOBA_WIDTH, D_MODEL),
                                                      (D_MODEL, D_MODEL), (D_MODEL, D_MODEL),
                                                      (PLE_DIM, D_MODEL))],
        compiler_params=_params(1),
        name="out_proj",
    )(o_a, o_b, gates, x2d, p[0].reshape(rows, PLE_DIM), w_branch_diff[0], w_branch_moba[0],
      w_out[0], w_ple_gate[0], w_ple[0], final_g.reshape(1, D_MODEL))
    return out.reshape(B, S, D_MODEL)
```

```python
import collections
import functools
import math

import jax
import jax.numpy as jnp
import numpy as np
from jax import lax
from jax.experimental import pallas as pl
from jax.experimental.pallas import tpu as pltpu

D_MODEL = 1024
PLE_DIM = 256
DIFF_HEADS = 4
DIFF_HEAD_DIM = 64
DIFF_WIDTH = DIFF_HEADS * 2 * DIFF_HEAD_DIM
MOBA_HEADS = 8
MOBA_HEAD_DIM = 64
MOBA_WIDTH = MOBA_HEADS * MOBA_HEAD_DIM
MOBA_BLOCK = 256
MOBA_TOPK = 3
ROT_DIM = 16
ROPE_THETA = 500000.0
EPS = 1e-6
SUBLN_EPS = 1e-5
LAM_INIT = 0.8 - 0.6 * math.exp(-0.3 * 0)

LANES = 128
MXU_COLS = 256
BF16_ROWS = 16
CHUNK = 512
N_CHUNKS = 12
ROW_TILE = 512
ATT_TILE = 512
PIPE_UNROLL = 24
VMEM_LIMIT = 56 * 1024 * 1024
_SCORE_SCALE = DIFF_HEAD_DIM ** -0.5 * math.log2(math.e)
DIFF_V_ROWS = 2 * DIFF_HEAD_DIM + BF16_ROWS
MOBA_V_ROWS = MOBA_HEAD_DIM + BF16_ROWS
assert DIFF_HEAD_DIM == MOBA_HEAD_DIM and ATT_TILE % MOBA_BLOCK == 0 and ROW_TILE == ATT_TILE

_F32 = jnp.float32
_BF16 = jnp.bfloat16
_NT = (((1,), (1,)), ((), ()))


def _rope_rows(a, cosv, sin_lo, sin_hi):
    outs = []
    for j in range(CHUNK // LANES):
        xs = a[:, LANES * j:LANES * (j + 1)]
        up = pltpu.roll(xs, LANES - ROT_DIM // 2, 1)
        dn = pltpu.roll(xs, ROT_DIM // 2, 1)
        outs.append(xs * cosv + up * sin_lo + dn * sin_hi)
    return jnp.concatenate(outs, axis=1)


def _proj_kernel(x_ref, g_ref, w_ref, cos_ref, slo_ref, shi_ref,
                 dq_ref, dk_ref, dvt_ref, dg_ref, mq_ref, mk_ref, mvt_ref, mg_ref,
                 km_ref, gates_ref):
    x = x_ref[...]
    ms = jnp.mean(x * x, axis=-1, keepdims=True)
    h = (x * lax.rsqrt(ms + EPS) * g_ref[...]).astype(_BF16)
    cosv, sin_lo, sin_hi = cos_ref[...], slo_ref[...], shi_ref[...]

    def proj(c):
        return jnp.dot(h, w_ref[:, CHUNK * c:CHUNK * (c + 1)], preferred_element_type=_F32)

    def store_transposed(ref, a, head_rows):
        at = a.T.astype(_BF16)
        ones = jnp.ones((BF16_ROWS, ROW_TILE), _BF16)
        stride = head_rows + BF16_ROWS
        for g in range(CHUNK // head_rows):
            ref[0, 0, stride * g:stride * g + head_rows, :] = at[head_rows * g:head_rows * (g + 1)]
            ref[0, 0, stride * g + head_rows:stride * (g + 1), :] = ones

    dq_ref[...] = (_rope_rows(proj(0), cosv, sin_lo, sin_hi) * _SCORE_SCALE).astype(_BF16)
    dk_ref[...] = _rope_rows(proj(1), cosv, sin_lo, sin_hi).astype(_BF16)
    store_transposed(dvt_ref, proj(2), 2 * DIFF_HEAD_DIM)
    dg_ref[...] = jax.nn.silu(proj(3))
    mq_ref[...] = _rope_rows(proj(4), cosv, sin_lo, sin_hi)
    mk = _rope_rows(proj(5), cosv, sin_lo, sin_hi)
    mk_ref[...] = mk.astype(_BF16)
    for j in range(ROW_TILE // MOBA_BLOCK):
        km_ref[0, j:j + 1, :] = jnp.mean(mk[MOBA_BLOCK * j:MOBA_BLOCK * (j + 1)], axis=0, keepdims=True)
    store_transposed(mvt_ref, proj(6), MOBA_HEAD_DIM)
    mg_ref[...] = jax.nn.silu(proj(7))
    for c in range(8, N_CHUNKS):
        gates_ref[:, CHUNK * (c - 8):CHUNK * (c - 7)] = jax.nn.sigmoid(proj(c))


def _split_heads(q):
    lane = lax.broadcasted_iota(jnp.int32, q.shape, 1)
    zero = jnp.zeros_like(q)
    return jnp.where(lane < 64, q, zero), jnp.where(lane >= 64, q, zero)


def _tile_rows(i):
    return pl.ds(pl.multiple_of(i * ATT_TILE, ATT_TILE), ATT_TILE)


def _past_items(n_q):
    items = [(qi, t) for qi in range(1, n_q) for t in range(qi)]
    assert len(items) % PIPE_UNROLL == 0 and PIPE_UNROLL % 2 == 0
    items.append(items[-1])
    return (np.array([i[0] for i in items], np.int32), np.array([i[1] for i in items], np.int32))


_Stream = collections.namedtuple("_Stream", "s_ref acc_ref m_ref")


def _run_attention(qtab_ref, ttab_ref, streams, n_seg, k_ref, cm_ref, q_of, vt_of, bias_of,
                   interleave):
    n_q = streams[0].acc_ref.shape[0]
    seg_rows = ATT_TILE // n_seg
    assert interleave in ("part", "stream") and seg_rows % MXU_COLS == 0

    def softmax(st, qi, tile, slot, first):
        stream = streams[st]
        biases = [bias_of(st, qi, tile, j) for j in range(n_seg)]
        cands = [cm_ref[slot, st * n_seg + j] if b is None else cm_ref[slot, st * n_seg + j] + b
                 for j, b in enumerate(biases)]
        m_old = None if first else stream.m_ref[qi]
        m_new = functools.reduce(jnp.maximum, cands if first else [m_old] + cands)
        stream.m_ref[qi] = m_new
        p = [jnp.exp2(stream.s_ref[slot, seg_rows * j:seg_rows * (j + 1), :]
                      - (m_new if b is None else m_new - b)).astype(_BF16)
             for j, b in enumerate(biases)]
        p = p[0] if n_seg == 1 else jnp.concatenate(p, axis=0)
        return p, (None if first else jnp.exp2(m_old - m_new))

    def pieces(diagonal):
        if diagonal:
            return [(c, MXU_COLS, c + MXU_COLS) for c in range(0, ATT_TILE, MXU_COLS)]
        width = MXU_COLS if interleave == "part" else ATT_TILE
        return [(c, width, ATT_TILE) for c in range(0, ATT_TILE, width)]

    def qk_piece(st, k, qi, slot, diagonal, c0, width, keys):
        cols = slice(c0, c0 + width)
        rows = pl.ds(pl.multiple_of(qi * ATT_TILE + c0, width), width)
        s = lax.dot_general(k[:keys], q_of(st, rows), _NT, preferred_element_type=_F32)
        if diagonal:
            kpos = lax.broadcasted_iota(jnp.int32, s.shape, 0)
            qpos = lax.broadcasted_iota(jnp.int32, s.shape, 1) + c0
            s = jnp.where(kpos <= qpos, s, -jnp.inf)
        s_ref = streams[st].s_ref
        s_ref[slot, :keys, cols] = s
        if keys < ATT_TILE:
            s_ref[slot, keys:, cols] = jnp.full((ATT_TILE - keys, width), -jnp.inf, _F32)
        for j in range(n_seg):
            seg = s[seg_rows * j:seg_rows * (j + 1)]
            cm_ref[slot, st * n_seg + j, :, cols] = (
                jnp.max(seg, axis=0, keepdims=True) if seg.shape[0]
                else jnp.full((1, width), -jnp.inf, _F32))

    def pv_piece(st, vt, p, alpha, qi, c0, width, keys):
        cols = slice(c0, c0 + width)
        acc_ref = streams[st].acc_ref
        pv = jnp.dot(vt[:, :keys], p[:keys, cols], preferred_element_type=_F32)
        acc_ref[qi, :, cols] = pv if alpha is None else alpha[:, cols] * acc_ref[qi, :, cols] + pv

    def step(nxt, cur, first=False):
        k = None if nxt is None else k_ref[_tile_rows(nxt[1]), :]
        for st in range(len(streams)):
            qk_ops, pv_ops = [], []
            if nxt is not None:
                qk_ops = [functools.partial(qk_piece, st, k, nxt[0], nxt[2], nxt[3], *pc)
                          for pc in pieces(nxt[3])]
            if cur is not None:
                p, alpha = softmax(st, cur[0], cur[1], cur[2], first)
                vt = vt_of(st, cur[1])
                pv_ops = [functools.partial(pv_piece, st, vt, p, alpha, cur[0], *pc)
                          for pc in pieces(first)]
            if interleave == "part":
                ops = [op for pair in zip(qk_ops, pv_ops) for op in pair]
                ops += qk_ops[len(pv_ops):] + pv_ops[len(qk_ops):]
            else:
                ops = qk_ops + pv_ops
            for op in ops:
                op()

    step((0, 0, 0, True), None)

    def diag_items(j, _):
        for u in range(2):
            t = 2 * j + u
            step((t + 1, t + 1, (u + 1) % 2, True), (t, t, u), first=True)
        return 0

    lax.fori_loop(0, n_q // 2 - 1, diag_items, 0)
    step((n_q - 1, n_q - 1, 1, True), (n_q - 2, n_q - 2, 0), first=True)
    step((qtab_ref[0], ttab_ref[0], 0, False), (n_q - 1, n_q - 1, 1), first=True)

    def past_items(j, _):
        for u in range(PIPE_UNROLL):
            t = PIPE_UNROLL * j + u
            step((qtab_ref[t + 1], ttab_ref[t + 1], (u + 1) % 2, False),
                 (qtab_ref[t], ttab_ref[t], u % 2))
        return 0

    lax.fori_loop(0, n_q * (n_q - 1) // (2 * PIPE_UNROLL), past_items, 0)


def _diff_kernel(qtab_ref, ttab_ref, lq1_ref, lk1_ref, lq2_ref, lk2_ref, q_ref, k_ref, vt_ref,
                 dg_ref, sg_ref, o_ref, acc1_ref, acc2_ref, m1_ref, m2_ref, s1_ref, s2_ref, cm_ref):
    n_q = acc1_ref.shape[0]
    streams = (_Stream(s1_ref, acc1_ref, m1_ref), _Stream(s2_ref, acc2_ref, m2_ref))
    _run_attention(
        qtab_ref, ttab_ref, streams, 1, k_ref, cm_ref,
        q_of=lambda st, rows: _split_heads(q_ref[rows, :])[st],
        vt_of=lambda st, tile: vt_ref[0, tile],
        bias_of=lambda st, qi, tile, j: None,
        interleave="stream")

    dv = 2 * DIFF_HEAD_DIM
    lam = (jnp.exp(jnp.sum(lq1_ref[...] * lk1_ref[...], axis=-1, keepdims=True))
           - jnp.exp(jnp.sum(lq2_ref[...] * lk2_ref[...], axis=-1, keepdims=True)) + LAM_INIT)

    def finalize(qi, _):
        ot = (acc1_ref[qi, :dv] / acc1_ref[qi, dv:dv + 1]
              - lam * (acc2_ref[qi, :dv] / acc2_ref[qi, dv:dv + 1]))
        ms = jnp.mean(ot * ot, axis=0, keepdims=True)
        ot = ot * lax.rsqrt(ms + SUBLN_EPS) * sg_ref[...] * (1.0 - LAM_INIT)
        o_ref[_tile_rows(qi), :] = (ot.T * dg_ref[_tile_rows(qi), :]).astype(_BF16)
        return 0

    lax.fori_loop(0, n_q // 2, lambda j, c: finalize(2 * j + 1, finalize(2 * j, c)), 0)


def _moba_kernel(qtab_ref, ttab_ref, q_ref, k_ref, vt_ref, km_ref, mg_ref, o_ref,
                 acca_ref, accb_ref, ma_ref, mb_ref, sa_ref, sb_ref, cm_ref, bias_ref, qs_ref):
    n_q = acca_ref.shape[0]
    nb = km_ref.shape[0]
    per_tile = ATT_TILE // MOBA_BLOCK
    km_both = jnp.concatenate(_split_heads(km_ref[...]), axis=0)
    blk = lax.broadcasted_iota(jnp.int32, (nb, ATT_TILE), 0)
    col = lax.broadcasted_iota(jnp.int32, (nb, ATT_TILE), 1)
    col_blk = lax.shift_right_logical(col, MOBA_BLOCK.bit_length() - 1)

    def gate(qi, _):
        q = q_ref[_tile_rows(qi), :]
        own = per_tile * qi + col_blk
        g_both = lax.dot_general(km_both, q, _NT, preferred_element_type=_F32,
                                 precision=lax.Precision.HIGHEST)
        for hh in range(2):
            g = jnp.where(blk < own, g_both[nb * hh:nb * (hh + 1)], -jnp.inf)
            sel = jnp.zeros(g.shape, jnp.bool_)
            for _ in range(MOBA_TOPK):
                mx = jnp.max(g, axis=0, keepdims=True)
                first_idx = jnp.min(jnp.where(g == mx, blk, nb), axis=0, keepdims=True)
                pick = (blk == first_idx) & (mx > -jnp.inf)
                sel = sel | pick
                g = jnp.where(pick, -jnp.inf, g)
            bias_ref[hh, qi] = jnp.where(sel | (blk == own), 0.0, -jnp.inf)
        for hh, qh in enumerate(_split_heads((q * _SCORE_SCALE).astype(_BF16))):
            qs_ref[hh, _tile_rows(qi), :] = qh
        return 0

    lax.fori_loop(0, n_q // 2, lambda j, c: gate(2 * j + 1, gate(2 * j, c)), 0)

    streams = (_Stream(sa_ref, acca_ref, ma_ref), _Stream(sb_ref, accb_ref, mb_ref))
    _run_attention(
        qtab_ref, ttab_ref, streams, per_tile, k_ref, cm_ref,
        q_of=lambda st, rows: qs_ref[st, rows, :],
        vt_of=lambda st, tile: vt_ref[0, tile, MOBA_V_ROWS * st:MOBA_V_ROWS * (st + 1), :],
        bias_of=lambda st, qi, tile, j: bias_ref[st, qi, pl.ds(per_tile * tile + j, 1), :],
        interleave="part")

    dh = MOBA_HEAD_DIM

    def finalize(qi, _):
        ot = jnp.concatenate([acca_ref[qi, :dh] / acca_ref[qi, dh:dh + 1],
                              accb_ref[qi, :dh] / accb_ref[qi, dh:dh + 1]], axis=0)
        o_ref[_tile_rows(qi), :] = (ot.T * mg_ref[_tile_rows(qi), :]).astype(_BF16)
        return 0

    lax.fori_loop(0, n_q // 2, lambda j, c: finalize(2 * j + 1, finalize(2 * j, c)), 0)


def _out_kernel(oa_ref, ob_ref, gates_ref, x_ref, p_ref, wbd_ref, wbm_ref, wout_ref, wpg_ref,
                wple_ref, fg_ref, o_ref, wbd_bf, wbm_bf, wout_bf, wpg_bf, wple_bf):
    @pl.when(pl.program_id(0) == 0)
    def _():
        for src, dst in ((wbd_ref, wbd_bf), (wbm_ref, wbm_bf), (wout_ref, wout_bf),
                         (wpg_ref, wpg_bf), (wple_ref, wple_bf)):
            dst[...] = src[...].astype(_BF16)

    ya = jnp.dot(oa_ref[...], wbd_bf[...], preferred_element_type=_F32)
    yb = jnp.dot(ob_ref[...], wbm_bf[...], preferred_element_type=_F32)
    merged = gates_ref[:, :D_MODEL] * ya + gates_ref[:, D_MODEL:] * yb
    x1 = x_ref[...] + jnp.dot(merged.astype(_BF16), wout_bf[...], preferred_element_type=_F32)
    t = jnp.dot(x1.astype(_BF16), wpg_bf[...], preferred_element_type=_F32)
    pe = jnp.dot(p_ref[...].astype(_BF16), wple_bf[...], preferred_element_type=_F32)
    x2 = x1 + jax.nn.sigmoid(t) * pe
    ms = jnp.mean(x2 * x2, axis=-1, keepdims=True)
    o_ref[...] = x2 * lax.rsqrt(ms + EPS) * fg_ref[...]


def _rope_lane_tables(seq):
    half = ROT_DIM // 2
    inv = ROPE_THETA ** (-np.arange(0, ROT_DIM, 2, dtype=np.float64) / ROT_DIM)
    ang = np.arange(seq, dtype=np.float64)[:, None] * inv[None, :]
    cos, sin = np.cos(ang), np.sin(ang)
    pad = MOBA_HEAD_DIM - ROT_DIM
    one, zero = np.ones((seq, pad)), np.zeros((seq, pad))
    zh = np.zeros((seq, half))
    cos64 = np.concatenate([cos, cos, one], axis=1)
    slo64 = np.concatenate([-sin, zh, zero], axis=1)
    shi64 = np.concatenate([zh, sin, zero], axis=1)
    return tuple(jnp.asarray(np.tile(t, (1, LANES // MOBA_HEAD_DIM)), _F32)
                 for t in (cos64, slo64, shi64))


def _resident(shape):
    return pl.BlockSpec(shape, lambda *_: (0,) * len(shape), pipeline_mode=pl.Buffered(1))


def _params(n_axes):
    return pltpu.CompilerParams(dimension_semantics=("arbitrary",) * n_axes,
                                vmem_limit_bytes=VMEM_LIMIT)


def kernel(x, p, norm_g, w_in, lambda_q1, lambda_k1, lambda_q2, lambda_k2, subln_g,
           w_branch_diff, w_branch_moba, w_out, w_ple, w_ple_gate, final_g):
    B, S, _ = x.shape
    assert w_in.shape == (1, D_MODEL, N_CHUNKS * CHUNK) and S % (2 * ATT_TILE) == 0
    rows = B * S
    n_row_tiles = rows // ROW_TILE
    n_kv = S // ATT_TILE
    n_blk = S // MOBA_BLOCK
    x2d = x.reshape(rows, D_MODEL)
    cosv, sin_lo, sin_hi = _rope_lane_tables(S)

    row_spec = lambda w: pl.BlockSpec((ROW_TILE, w), lambda i: (i, 0))
    tab_spec = pl.BlockSpec((ROW_TILE, LANES), lambda i: (i % n_kv, 0))
    dvt_rows, mvt_rows = DIFF_HEADS * DIFF_V_ROWS, MOBA_HEADS * MOBA_V_ROWS
    vt_spec = lambda r: pl.BlockSpec((1, 1, r, ATT_TILE), lambda i: (i // n_kv, i % n_kv, 0, 0))
    vt_shape = lambda r: jax.ShapeDtypeStruct((B, n_kv, r, ATT_TILE), _BF16)
    rows_bf16 = jax.ShapeDtypeStruct((rows, CHUNK), _BF16)
    rows_f32 = jax.ShapeDtypeStruct((rows, CHUNK), _F32)
    km_rows = ROW_TILE // MOBA_BLOCK
    (dq, dk, dvt, dgs, mq, mk, mvt, mgs, kmean, gates) = pl.pallas_call(
        _proj_kernel,
        grid=(n_row_tiles,),
        in_specs=[row_spec(D_MODEL), _resident((1, D_MODEL)), _resident((D_MODEL, N_CHUNKS * CHUNK)),
                  tab_spec, tab_spec, tab_spec],
        out_specs=[row_spec(CHUNK), row_spec(CHUNK), vt_spec(dvt_rows), row_spec(CHUNK), row_spec(CHUNK),
                   row_spec(CHUNK), vt_spec(mvt_rows), row_spec(CHUNK),
                   pl.BlockSpec((1, km_rows, CHUNK), lambda i: (i, 0, 0)),
                   row_spec(4 * CHUNK)],
        out_shape=[rows_bf16, rows_bf16, vt_shape(dvt_rows), rows_f32, rows_f32, rows_bf16,
                   vt_shape(mvt_rows), rows_f32,
                   jax.ShapeDtypeStruct((n_row_tiles, km_rows, CHUNK), _F32),
                   jax.ShapeDtypeStruct((rows, 4 * CHUNK), _F32)],
        compiler_params=_params(1),
        name="proj",
    )(x2d, norm_g[0].reshape(1, D_MODEL), w_in[0].astype(_BF16), cosv, sin_lo, sin_hi)
    kmean = kmean.reshape(B * n_blk, CHUNK)

    n_groups = CHUNK // LANES
    qtab, ttab = _past_items(n_kv)
    seq_spec = pl.BlockSpec((S, LANES), lambda b, g, *_: (b, g))
    v_spec = lambda r: pl.BlockSpec((1, n_kv, r, ATT_TILE), lambda b, g, *_: (b, 0, g, 0))
    lam_spec = pl.BlockSpec((1, DIFF_HEAD_DIM), lambda b, g, *_: (0, 0))
    acc = lambda r: pltpu.VMEM((n_kv, r, ATT_TILE), _F32)
    run_max = pltpu.VMEM((n_kv, 1, ATT_TILE), _F32)
    score_buf = pltpu.VMEM((2, ATT_TILE, ATT_TILE), _F32)
    col_max = lambda n: pltpu.VMEM((2, n, 1, ATT_TILE), _F32)

    o_a = pl.pallas_call(
        _diff_kernel,
        grid_spec=pltpu.PrefetchScalarGridSpec(
            num_scalar_prefetch=2,
            grid=(B, n_groups),
            in_specs=[lam_spec, lam_spec, lam_spec, lam_spec, seq_spec, seq_spec, v_spec(DIFF_V_ROWS),
                      seq_spec, pl.BlockSpec((2 * DIFF_HEAD_DIM, 1), lambda b, g, *_: (0, 0))],
            out_specs=seq_spec,
            scratch_shapes=[acc(DIFF_V_ROWS), acc(DIFF_V_ROWS), run_max, run_max, score_buf, score_buf,
                            col_max(2)],
        ),
        out_shape=rows_bf16,
        compiler_params=_params(2),
        name="diff_attn",
    )(qtab, ttab, lambda_q1[0:1], lambda_k1[0:1], lambda_q2[0:1], lambda_k2[0:1], dq, dk, dvt, dgs,
      subln_g[0].reshape(2 * DIFF_HEAD_DIM, 1))

    o_b = pl.pallas_call(
        _moba_kernel,
        grid_spec=pltpu.PrefetchScalarGridSpec(
            num_scalar_prefetch=2,
            grid=(B, n_groups),
            in_specs=[seq_spec, seq_spec, v_spec(2 * MOBA_V_ROWS),
                      pl.BlockSpec((n_blk, LANES), lambda b, g, *_: (b, g)), seq_spec],
            out_specs=seq_spec,
            scratch_shapes=[acc(MOBA_V_ROWS), acc(MOBA_V_ROWS), run_max, run_max, score_buf, score_buf,
                            col_max(2 * (ATT_TILE // MOBA_BLOCK)),
                            pltpu.VMEM((2, n_kv, n_blk, ATT_TILE), _F32),
                            pltpu.VMEM((2, S, LANES), _BF16)],
        ),
        out_shape=rows_bf16,
        compiler_params=_params(2),
        name="moba_attn",
    )(qtab, ttab, mq, mk, mvt, kmean, mgs)

    out = pl.pallas_call(
        _out_kernel,
        grid=(n_row_tiles,),
        in_specs=[row_spec(CHUNK), row_spec(CHUNK), row_spec(4 * CHUNK), row_spec(D_MODEL),
                  row_spec(PLE_DIM), _resident((DIFF_WIDTH, D_MODEL)), _resident((MOBA_WIDTH, D_MODEL)),
                  _resident((D_MODEL, D_MODEL)), _resident((D_MODEL, D_MODEL)),
                  _resident((PLE_DIM, D_MODEL)), _resident((1, D_MODEL))],
        out_specs=row_spec(D_MODEL),
        out_shape=jax.ShapeDtypeStruct((rows, D_MODEL), _F32),
        scratch_shapes=[pltpu.VMEM(s, _BF16) for s in ((DIFF_WIDTH, D_MODEL), (MOBA_WIDTH, D_MODEL),
                                                      (D_MODEL, D_MODEL), (D_MODEL, D_MODEL),
                                                      (PLE_DIM, D_MODEL))],
        compiler_params=_params(1),
        name="out_proj",
    )(o_a, o_b, gates, x2d, p[0].reshape(rows, PLE_DIM), w_branch_diff[0], w_branch_moba[0],
      w_out[0], w_ple_gate[0], w_ple[0], final_g.reshape(1, D_MODEL))
    return out.reshape(B, S, D_MODEL)
```

```python
import collections
import functools
import math

import jax
import jax.numpy as jnp
import numpy as np
from jax import lax
from jax.experimental import pallas as pl
from jax.experimental.pallas import tpu as pltpu

D_MODEL = 1024
PLE_DIM = 256
DIFF_HEADS = 4
DIFF_HEAD_DIM = 64
DIFF_WIDTH = DIFF_HEADS * 2 * DIFF_HEAD_DIM
MOBA_HEADS = 8
MOBA_HEAD_DIM = 64
MOBA_WIDTH = MOBA_HEADS * MOBA_HEAD_DIM
MOBA_BLOCK = 256
MOBA_TOPK = 3
ROT_DIM = 16
ROPE_THETA = 500000.0
EPS = 1e-6
SUBLN_EPS = 1e-5
LAM_INIT = 0.8 - 0.6 * math.exp(-0.3 * 0)

LANES = 128
MXU_COLS = 256
BF16_ROWS = 16
CHUNK = 512
N_CHUNKS = 12
ROW_TILE = 512
ATT_TILE = 512
PIPE_UNROLL = 24
VMEM_LIMIT = 56 * 1024 * 1024
_SCORE_SCALE = DIFF_HEAD_DIM ** -0.5 * math.log2(math.e)
DIFF_V_ROWS = 2 * DIFF_HEAD_DIM + BF16_ROWS
MOBA_V_ROWS = MOBA_HEAD_DIM + BF16_ROWS
assert DIFF_HEAD_DIM == MOBA_HEAD_DIM and ATT_TILE % MOBA_BLOCK == 0 and ROW_TILE == ATT_TILE

_F32 = jnp.float32
_BF16 = jnp.bfloat16
_NT = (((1,), (1,)), ((), ()))


def _rope_rows(a, cosv, sin_lo, sin_hi):
    outs = []
    for j in range(CHUNK // LANES):
        xs = a[:, LANES * j:LANES * (j + 1)]
        up = pltpu.roll(xs, LANES - ROT_DIM // 2, 1)
        dn = pltpu.roll(xs, ROT_DIM // 2, 1)
        outs.append(xs * cosv + up * sin_lo + dn * sin_hi)
    return jnp.concatenate(outs, axis=1)


def _proj_kernel(x_ref, g_ref, w_ref, cos_ref, slo_ref, shi_ref,
                 dq_ref, dk_ref, dvt_ref, dg_ref, mq_ref, mk_ref, mvt_ref, mg_ref,
                 km_ref, gates_ref):
    for j in range(ROW_TILE // MOBA_BLOCK):
        r = slice(MOBA_BLOCK * j, MOBA_BLOCK * (j + 1))
        x = x_ref[r]
        ms = jnp.mean(x * x, axis=-1, keepdims=True)
        h = (x * lax.rsqrt(ms + EPS) * g_ref[...]).astype(_BF16)
        rope = functools.partial(_rope_rows, cosv=cos_ref[r], sin_lo=slo_ref[r], sin_hi=shi_ref[r])

        def proj(c):
            return jnp.dot(h, w_ref[:, CHUNK * c:CHUNK * (c + 1)], preferred_element_type=_F32)

        def store_transposed(ref, a, head_rows):
            at = a.T.astype(_BF16)
            ones = jnp.ones((BF16_ROWS, MOBA_BLOCK), _BF16)
            stride = head_rows + BF16_ROWS
            for g in range(CHUNK // head_rows):
                ref[0, 0, stride * g:stride * g + head_rows, r] = at[head_rows * g:head_rows * (g + 1)]
                ref[0, 0, stride * g + head_rows:stride * (g + 1), r] = ones

        dq_ref[r] = (rope(proj(0)) * _SCORE_SCALE).astype(_BF16)
        dk_ref[r] = rope(proj(1)).astype(_BF16)
        store_transposed(dvt_ref, proj(2), 2 * DIFF_HEAD_DIM)
        dg_ref[r] = jax.nn.silu(proj(3))
        mq_ref[r] = rope(proj(4))
        mk = rope(proj(5))
        mk_ref[r] = mk.astype(_BF16)
        km_ref[0, j:j + 1, :] = jnp.mean(mk, axis=0, keepdims=True)
        store_transposed(mvt_ref, proj(6), MOBA_HEAD_DIM)
        mg_ref[r] = jax.nn.silu(proj(7))
        for c in range(8, N_CHUNKS):
            gates_ref[r, CHUNK * (c - 8):CHUNK * (c - 7)] = jax.nn.sigmoid(proj(c))


def _split_heads(q):
    lane = lax.broadcasted_iota(jnp.int32, q.shape, 1)
    zero = jnp.zeros_like(q)
    return jnp.where(lane < 64, q, zero), jnp.where(lane >= 64, q, zero)


def _tile_rows(i):
    return pl.ds(pl.multiple_of(i * ATT_TILE, ATT_TILE), ATT_TILE)


def _past_items(n_q):
    items = [(qi, t) for qi in range(1, n_q) for t in range(qi)]
    assert len(items) % PIPE_UNROLL == 0 and PIPE_UNROLL % 2 == 0
    items.append(items[-1])
    return (np.array([i[0] for i in items], np.int32), np.array([i[1] for i in items], np.int32))


_Stream = collections.namedtuple("_Stream", "s_ref acc_ref m_ref")


def _run_attention(qtab_ref, ttab_ref, streams, n_seg, k_ref, cm_ref, q_of, vt_of, bias_of,
                   interleave):
    n_q = streams[0].acc_ref.shape[0]
    seg_rows = ATT_TILE // n_seg
    assert interleave in ("part", "stream") and seg_rows % MXU_COLS == 0

    def softmax(st, qi, tile, slot, first):
        stream = streams[st]
        biases = [bias_of(st, qi, tile, j) for j in range(n_seg)]
        cands = [cm_ref[slot, st * n_seg + j] if b is None else cm_ref[slot, st * n_seg + j] + b
                 for j, b in enumerate(biases)]
        m_old = None if first else stream.m_ref[qi]
        m_new = functools.reduce(jnp.maximum, cands if first else [m_old] + cands)
        stream.m_ref[qi] = m_new
        p = [jnp.exp2(stream.s_ref[slot, seg_rows * j:seg_rows * (j + 1), :]
                      - (m_new if b is None else m_new - b)).astype(_BF16)
             for j, b in enumerate(biases)]
        p = p[0] if n_seg == 1 else jnp.concatenate(p, axis=0)
        return p, (None if first else jnp.exp2(m_old - m_new))

    def pieces(diagonal):
        if diagonal:
            return [(c, MXU_COLS, c + MXU_COLS) for c in range(0, ATT_TILE, MXU_COLS)]
        width = MXU_COLS if interleave == "part" else ATT_TILE
        return [(c, width, ATT_TILE) for c in range(0, ATT_TILE, width)]

    def qk_piece(st, k, qi, slot, diagonal, c0, width, keys):
        cols = slice(c0, c0 + width)
        rows = pl.ds(pl.multiple_of(qi * ATT_TILE + c0, width), width)
        s = lax.dot_general(k[:keys], q_of(st, rows), _NT, preferred_element_type=_F32)
        if diagonal:
            kpos = lax.broadcasted_iota(jnp.int32, s.shape, 0)
            qpos = lax.broadcasted_iota(jnp.int32, s.shape, 1) + c0
            s = jnp.where(kpos <= qpos, s, -jnp.inf)
        s_ref = streams[st].s_ref
        s_ref[slot, :keys, cols] = s
        if keys < ATT_TILE:
            s_ref[slot, keys:, cols] = jnp.full((ATT_TILE - keys, width), -jnp.inf, _F32)
        for j in range(n_seg):
            seg = s[seg_rows * j:seg_rows * (j + 1)]
            cm_ref[slot, st * n_seg + j, :, cols] = (
                jnp.max(seg, axis=0, keepdims=True) if seg.shape[0]
                else jnp.full((1, width), -jnp.inf, _F32))

    def pv_piece(st, vt, p, alpha, qi, c0, width, keys):
        cols = slice(c0, c0 + width)
        acc_ref = streams[st].acc_ref
        pv = jnp.dot(vt[:, :keys], p[:keys, cols], preferred_element_type=_F32)
        acc_ref[qi, :, cols] = pv if alpha is None else alpha[:, cols] * acc_ref[qi, :, cols] + pv

    def step(nxt, cur, first=False):
        k = None if nxt is None else k_ref[_tile_rows(nxt[1]), :]
        for st in range(len(streams)):
            qk_ops, pv_ops = [], []
            if nxt is not None:
                qk_ops = [functools.partial(qk_piece, st, k, nxt[0], nxt[2], nxt[3], *pc)
                          for pc in pieces(nxt[3])]
            if cur is not None:
                p, alpha = softmax(st, cur[0], cur[1], cur[2], first)
                vt = vt_of(st, cur[1])
                pv_ops = [functools.partial(pv_piece, st, vt, p, alpha, cur[0], *pc)
                          for pc in pieces(first)]
            if interleave == "part":
                ops = [op for pair in zip(qk_ops, pv_ops) for op in pair]
                ops += qk_ops[len(pv_ops):] + pv_ops[len(qk_ops):]
            else:
                ops = qk_ops + pv_ops
            for op in ops:
                op()

    step((0, 0, 0, True), None)

    def diag_items(j, _):
        for u in range(2):
            t = 2 * j + u
            step((t + 1, t + 1, (u + 1) % 2, True), (t, t, u), first=True)
        return 0

    lax.fori_loop(0, n_q // 2 - 1, diag_items, 0)
    step((n_q - 1, n_q - 1, 1, True), (n_q - 2, n_q - 2, 0), first=True)
    step((qtab_ref[0], ttab_ref[0], 0, False), (n_q - 1, n_q - 1, 1), first=True)

    def past_items(j, _):
        for u in range(PIPE_UNROLL):
            t = PIPE_UNROLL * j + u
            step((qtab_ref[t + 1], ttab_ref[t + 1], (u + 1) % 2, False),
                 (qtab_ref[t], ttab_ref[t], u % 2))
        return 0

    lax.fori_loop(0, n_q * (n_q - 1) // (2 * PIPE_UNROLL), past_items, 0)


def _diff_kernel(qtab_ref, ttab_ref, lq1_ref, lk1_ref, lq2_ref, lk2_ref, q_ref, k_ref, vt_ref,
                 dg_ref, sg_ref, o_ref, acc1_ref, acc2_ref, m1_ref, m2_ref, s1_ref, s2_ref, cm_ref):
    n_q = acc1_ref.shape[0]
    streams = (_Stream(s1_ref, acc1_ref, m1_ref), _Stream(s2_ref, acc2_ref, m2_ref))
    _run_attention(
        qtab_ref, ttab_ref, streams, 1, k_ref, cm_ref,
        q_of=lambda st, rows: _split_heads(q_ref[rows, :])[st],
        vt_of=lambda st, tile: vt_ref[0, tile],
        bias_of=lambda st, qi, tile, j: None,
        interleave="stream")

    dv = 2 * DIFF_HEAD_DIM
    lam = (jnp.exp(jnp.sum(lq1_ref[...] * lk1_ref[...], axis=-1, keepdims=True))
           - jnp.exp(jnp.sum(lq2_ref[...] * lk2_ref[...], axis=-1, keepdims=True)) + LAM_INIT)

    def finalize(qi, _):
        ot = (acc1_ref[qi, :dv] / acc1_ref[qi, dv:dv + 1]
              - lam * (acc2_ref[qi, :dv] / acc2_ref[qi, dv:dv + 1]))
        ms = jnp.mean(ot * ot, axis=0, keepdims=True)
        ot = ot * lax.rsqrt(ms + SUBLN_EPS) * sg_ref[...] * (1.0 - LAM_INIT)
        o_ref[_tile_rows(qi), :] = (ot.T * dg_ref[_tile_rows(qi), :]).astype(_BF16)
        return 0

    lax.fori_loop(0, n_q // 2, lambda j, c: finalize(2 * j + 1, finalize(2 * j, c)), 0)


def _moba_kernel(qtab_ref, ttab_ref, q_ref, k_ref, vt_ref, km_ref, mg_ref, o_ref,
                 acca_ref, accb_ref, ma_ref, mb_ref, sa_ref, sb_ref, cm_ref, bias_ref, qs_ref):
    n_q = acca_ref.shape[0]
    nb = km_ref.shape[0]
    per_tile = ATT_TILE // MOBA_BLOCK
    km_both = jnp.concatenate(_split_heads(km_ref[...]), axis=0)
    blk = lax.broadcasted_iota(jnp.int32, (nb, ATT_TILE), 0)
    col = lax.broadcasted_iota(jnp.int32, (nb, ATT_TILE), 1)
    col_blk = lax.shift_right_logical(col, MOBA_BLOCK.bit_length() - 1)

    def gate(qi, _):
        q = q_ref[_tile_rows(qi), :]
        own = per_tile * qi + col_blk
        g_both = lax.dot_general(km_both, q, _NT, preferred_element_type=_F32,
                                 precision=lax.Precision.HIGHEST)
        for hh in range(2):
            g = jnp.where(blk < own, g_both[nb * hh:nb * (hh + 1)], -jnp.inf)
            sel = jnp.zeros(g.shape, jnp.bool_)
            for _ in range(MOBA_TOPK):
                mx = jnp.max(g, axis=0, keepdims=True)
                first_idx = jnp.min(jnp.where(g == mx, blk, nb), axis=0, keepdims=True)
                pick = (blk == first_idx) & (mx > -jnp.inf)
                sel = sel | pick
                g = jnp.where(pick, -jnp.inf, g)
            bias_ref[hh, qi] = jnp.where(sel | (blk == own), 0.0, -jnp.inf)
        for hh, qh in enumerate(_split_heads((q * _SCORE_SCALE).astype(_BF16))):
            qs_ref[hh, _tile_rows(qi), :] = qh
        return 0

    lax.fori_loop(0, n_q // 2, lambda j, c: gate(2 * j + 1, gate(2 * j, c)), 0)

    streams = (_Stream(sa_ref, acca_ref, ma_ref), _Stream(sb_ref, accb_ref, mb_ref))
    _run_attention(
        qtab_ref, ttab_ref, streams, per_tile, k_ref, cm_ref,
        q_of=lambda st, rows: qs_ref[st, rows, :],
        vt_of=lambda st, tile: vt_ref[0, tile, MOBA_V_ROWS * st:MOBA_V_ROWS * (st + 1), :],
        bias_of=lambda st, qi, tile, j: bias_ref[st, qi, pl.ds(per_tile * tile + j, 1), :],
        interleave="part")

    dh = MOBA_HEAD_DIM

    def finalize(qi, _):
        ot = jnp.concatenate([acca_ref[qi, :dh] / acca_ref[qi, dh:dh + 1],
                              accb_ref[qi, :dh] / accb_ref[qi, dh:dh + 1]], axis=0)
        o_ref[_tile_rows(qi), :] = (ot.T * mg_ref[_tile_rows(qi), :]).astype(_BF16)
        return 0

    lax.fori_loop(0, n_q // 2, lambda j, c: finalize(2 * j + 1, finalize(2 * j, c)), 0)


def _out_kernel(oa_ref, ob_ref, gates_ref, x_ref, p_ref, wbd_ref, wbm_ref, wout_ref, wpg_ref,
                wple_ref, fg_ref, o_ref, wbd_bf, wbm_bf, wout_bf, wpg_bf, wple_bf):
    @pl.when(pl.program_id(0) == 0)
    def _():
        for src, dst in ((wbd_ref, wbd_bf), (wbm_ref, wbm_bf), (wout_ref, wout_bf),
                         (wpg_ref, wpg_bf), (wple_ref, wple_bf)):
            dst[...] = src[...].astype(_BF16)

    half = ROW_TILE // 2
    for r in (slice(0, half), slice(half, ROW_TILE)):
        ya = jnp.dot(oa_ref[r], wbd_bf[...], preferred_element_type=_F32)
        yb = jnp.dot(ob_ref[r], wbm_bf[...], preferred_element_type=_F32)
        merged = gates_ref[r, :D_MODEL] * ya + gates_ref[r, D_MODEL:] * yb
        x1 = x_ref[r] + jnp.dot(merged.astype(_BF16), wout_bf[...], preferred_element_type=_F32)
        t = jnp.dot(x1.astype(_BF16), wpg_bf[...], preferred_element_type=_F32)
        pe = jnp.dot(p_ref[r].astype(_BF16), wple_bf[...], preferred_element_type=_F32)
        x2 = x1 + jax.nn.sigmoid(t) * pe
        ms = jnp.mean(x2 * x2, axis=-1, keepdims=True)
        o_ref[r] = x2 * lax.rsqrt(ms + EPS) * fg_ref[...]


def _rope_lane_tables(seq):
    half = ROT_DIM // 2
    inv = ROPE_THETA ** (-np.arange(0, ROT_DIM, 2, dtype=np.float64) / ROT_DIM)
    ang = np.arange(seq, dtype=np.float64)[:, None] * inv[None, :]
    cos, sin = np.cos(ang), np.sin(ang)
    pad = MOBA_HEAD_DIM - ROT_DIM
    one, zero = np.ones((seq, pad)), np.zeros((seq, pad))
    zh = np.zeros((seq, half))
    cos64 = np.concatenate([cos, cos, one], axis=1)
    slo64 = np.concatenate([-sin, zh, zero], axis=1)
    shi64 = np.concatenate([zh, sin, zero], axis=1)
    return tuple(jnp.asarray(np.tile(t, (1, LANES // MOBA_HEAD_DIM)), _F32)
                 for t in (cos64, slo64, shi64))


def _resident(shape):
    return pl.BlockSpec(shape, lambda *_: (0,) * len(shape), pipeline_mode=pl.Buffered(1))


def _params(n_axes):
    return pltpu.CompilerParams(dimension_semantics=("arbitrary",) * n_axes,
                                vmem_limit_bytes=VMEM_LIMIT)


def kernel(x, p, norm_g, w_in, lambda_q1, lambda_k1, lambda_q2, lambda_k2, subln_g,
           w_branch_diff, w_branch_moba, w_out, w_ple, w_ple_gate, final_g):
    B, S, _ = x.shape
    assert w_in.shape == (1, D_MODEL, N_CHUNKS * CHUNK) and S % (2 * ATT_TILE) == 0
    rows = B * S
    n_row_tiles = rows // ROW_TILE
    n_kv = S // ATT_TILE
    n_blk = S // MOBA_BLOCK
    x2d = x.reshape(rows, D_MODEL)
    cosv, sin_lo, sin_hi = _rope_lane_tables(S)

    row_spec = lambda w: pl.BlockSpec((ROW_TILE, w), lambda i: (i, 0))
    tab_spec = pl.BlockSpec((ROW_TILE, LANES), lambda i: (i % n_kv, 0))
    dvt_rows, mvt_rows = DIFF_HEADS * DIFF_V_ROWS, MOBA_HEADS * MOBA_V_ROWS
    vt_spec = lambda r: pl.BlockSpec((1, 1, r, ATT_TILE), lambda i: (i // n_kv, i % n_kv, 0, 0))
    vt_shape = lambda r: jax.ShapeDtypeStruct((B, n_kv, r, ATT_TILE), _BF16)
    rows_bf16 = jax.ShapeDtypeStruct((rows, CHUNK), _BF16)
    rows_f32 = jax.ShapeDtypeStruct((rows, CHUNK), _F32)
    km_rows = ROW_TILE // MOBA_BLOCK
    (dq, dk, dvt, dgs, mq, mk, mvt, mgs, kmean, gates) = pl.pallas_call(
        _proj_kernel,
        grid=(n_row_tiles,),
        in_specs=[row_spec(D_MODEL), _resident((1, D_MODEL)), _resident((D_MODEL, N_CHUNKS * CHUNK)),
                  tab_spec, tab_spec, tab_spec],
        out_specs=[row_spec(CHUNK), row_spec(CHUNK), vt_spec(dvt_rows), row_spec(CHUNK), row_spec(CHUNK),
                   row_spec(CHUNK), vt_spec(mvt_rows), row_spec(CHUNK),
                   pl.BlockSpec((1, km_rows, CHUNK), lambda i: (i, 0, 0)),
                   row_spec(4 * CHUNK)],
        out_shape=[rows_bf16, rows_bf16, vt_shape(dvt_rows), rows_f32, rows_f32, rows_bf16,
                   vt_shape(mvt_rows), rows_f32,
                   jax.ShapeDtypeStruct((n_row_tiles, km_rows, CHUNK), _F32),
                   jax.ShapeDtypeStruct((rows, 4 * CHUNK), _F32)],
        compiler_params=_params(1),
        name="proj",
    )(x2d, norm_g[0].reshape(1, D_MODEL), w_in[0].astype(_BF16), cosv, sin_lo, sin_hi)
    kmean = kmean.reshape(B * n_blk, CHUNK)

    n_groups = CHUNK // LANES
    qtab, ttab = _past_items(n_kv)
    seq_spec = pl.BlockSpec((S, LANES), lambda b, g, *_: (b, g))
    v_spec = lambda r: pl.BlockSpec((1, n_kv, r, ATT_TILE), lambda b, g, *_: (b, 0, g, 0))
    lam_spec = pl.BlockSpec((1, DIFF_HEAD_DIM), lambda b, g, *_: (0, 0))
    acc = lambda r: pltpu.VMEM((n_kv, r, ATT_TILE), _F32)
    run_max = pltpu.VMEM((n_kv, 1, ATT_TILE), _F32)
    score_buf = pltpu.VMEM((2, ATT_TILE, ATT_TILE), _F32)
    col_max = lambda n: pltpu.VMEM((2, n, 1, ATT_TILE), _F32)

    o_a = pl.pallas_call(
        _diff_kernel,
        grid_spec=pltpu.PrefetchScalarGridSpec(
            num_scalar_prefetch=2,
            grid=(B, n_groups),
            in_specs=[lam_spec, lam_spec, lam_spec, lam_spec, seq_spec, seq_spec, v_spec(DIFF_V_ROWS),
                      seq_spec, pl.BlockSpec((2 * DIFF_HEAD_DIM, 1), lambda b, g, *_: (0, 0))],
            out_specs=seq_spec,
            scratch_shapes=[acc(DIFF_V_ROWS), acc(DIFF_V_ROWS), run_max, run_max, score_buf, score_buf,
                            col_max(2)],
        ),
        out_shape=rows_bf16,
        compiler_params=_params(2),
        name="diff_attn",
    )(qtab, ttab, lambda_q1[0:1], lambda_k1[0:1], lambda_q2[0:1], lambda_k2[0:1], dq, dk, dvt, dgs,
      subln_g[0].reshape(2 * DIFF_HEAD_DIM, 1))

    o_b = pl.pallas_call(
        _moba_kernel,
        grid_spec=pltpu.PrefetchScalarGridSpec(
            num_scalar_prefetch=2,
            grid=(B, n_groups),
            in_specs=[seq_spec, seq_spec, v_spec(2 * MOBA_V_ROWS),
                      pl.BlockSpec((n_blk, LANES), lambda b, g, *_: (b, g)), seq_spec],
            out_specs=seq_spec,
            scratch_shapes=[acc(MOBA_V_ROWS), acc(MOBA_V_ROWS), run_max, run_max, score_buf, score_buf,
                            col_max(2 * (ATT_TILE // MOBA_BLOCK)),
                            pltpu.VMEM((2, n_kv, n_blk, ATT_TILE), _F32),
                            pltpu.VMEM((2, S, LANES), _BF16)],
        ),
        out_shape=rows_bf16,
        compiler_params=_params(2),
        name="moba_attn",
    )(qtab, ttab, mq, mk, mvt, kmean, mgs)

    out = pl.pallas_call(
        _out_kernel,
        grid=(n_row_tiles,),
        in_specs=[row_spec(CHUNK), row_spec(CHUNK), row_spec(4 * CHUNK), row_spec(D_MODEL),
                  row_spec(PLE_DIM), _resident((DIFF_WIDTH, D_MODEL)), _resident((MOBA_WIDTH, D_MODEL)),
                  _resident((D_MODEL, D_MODEL)), _resident((D_MODEL, D_MODEL)),
                  _resident((PLE_DIM, D_MODEL)), _resident((1, D_MODEL))],
        out_specs=row_spec(D_MODEL),
        out_shape=jax.ShapeDtypeStruct((rows, D_MODEL), _F32),
        scratch_shapes=[pltpu.VMEM(s, _BF16) for s in ((DIFF_WIDTH, D_MODEL), (MOBA_WIDTH, D_MODEL),
                                                      (D_MODEL, D_MODEL), (D_MODEL, D_MODEL),
                                                      (PLE_DIM, D_MODEL))],
        compiler_params=_params(1),
        name="out_proj",
    )(o_a, o_b, gates, x2d, p[0].reshape(rows, PLE_DIM), w_branch_diff[0], w_branch_moba[0],
      w_out[0], w_ple_gate[0], w_ple[0], final_g.reshape(1, D_MODEL))
    return out.reshape(B, S, D_MODEL)
```

```python
import collections
import functools
import math

import jax
import jax.numpy as jnp
import numpy as np
from jax import lax
from jax.experimental import pallas as pl
from jax.experimental.pallas import tpu as pltpu

D_MODEL = 1024
PLE_DIM = 256
DIFF_HEADS = 4
DIFF_HEAD_DIM = 64
DIFF_WIDTH = DIFF_HEADS * 2 * DIFF_HEAD_DIM
MOBA_HEADS = 8
MOBA_HEAD_DIM = 64
MOBA_WIDTH = MOBA_HEADS * MOBA_HEAD_DIM
MOBA_BLOCK = 256
MOBA_TOPK = 3
ROT_DIM = 16
ROPE_THETA = 500000.0
EPS = 1e-6
SUBLN_EPS = 1e-5
LAM_INIT = 0.8 - 0.6 * math.exp(-0.3 * 0)

LANES = 128
MXU_COLS = 256
BF16_ROWS = 16
CHUNK = 512
N_CHUNKS = 12
ROW_TILE = 512
ATT_TILE = 512
PIPE_UNROLL = 24
VMEM_LIMIT = 56 * 1024 * 1024
_SCORE_SCALE = DIFF_HEAD_DIM ** -0.5 * math.log2(math.e)
DIFF_V_ROWS = 2 * DIFF_HEAD_DIM + BF16_ROWS
MOBA_V_ROWS = MOBA_HEAD_DIM + BF16_ROWS
assert DIFF_HEAD_DIM == MOBA_HEAD_DIM and ATT_TILE % MOBA_BLOCK == 0 and ROW_TILE == ATT_TILE

_F32 = jnp.float32
_BF16 = jnp.bfloat16
_NT = (((1,), (1,)), ((), ()))


def _rope_rows(a, cosv, sin_lo, sin_hi):
    outs = []
    for j in range(CHUNK // LANES):
        xs = a[:, LANES * j:LANES * (j + 1)]
        up = pltpu.roll(xs, LANES - ROT_DIM // 2, 1)
        dn = pltpu.roll(xs, ROT_DIM // 2, 1)
        outs.append(xs * cosv + up * sin_lo + dn * sin_hi)
    return jnp.concatenate(outs, axis=1)


def _proj_kernel(x_ref, g_ref, w_ref, cos_ref, slo_ref, shi_ref,
                 dq_ref, dk_ref, dvt_ref, dg_ref, mq_ref, mk_ref, mvt_ref, mg_ref,
                 km_ref, gates_ref):
    for j in range(ROW_TILE // MOBA_BLOCK):
        r = slice(MOBA_BLOCK * j, MOBA_BLOCK * (j + 1))
        x = x_ref[r]
        ms = jnp.mean(x * x, axis=-1, keepdims=True)
        h = (x * lax.rsqrt(ms + EPS) * g_ref[...]).astype(_BF16)
        rope = functools.partial(_rope_rows, cosv=cos_ref[r], sin_lo=slo_ref[r], sin_hi=shi_ref[r])

        def proj(c):
            return jnp.dot(h, w_ref[:, CHUNK * c:CHUNK * (c + 1)], preferred_element_type=_F32)

        def store_transposed(ref, a, head_rows):
            at = a.T.astype(_BF16)
            ones = jnp.ones((BF16_ROWS, MOBA_BLOCK), _BF16)
            stride = head_rows + BF16_ROWS
            for g in range(CHUNK // head_rows):
                ref[0, 0, stride * g:stride * g + head_rows, r] = at[head_rows * g:head_rows * (g + 1)]
                ref[0, 0, stride * g + head_rows:stride * (g + 1), r] = ones

        dq_ref[r] = (rope(proj(0)) * _SCORE_SCALE).astype(_BF16)
        dk_ref[r] = rope(proj(1)).astype(_BF16)
        store_transposed(dvt_ref, proj(2), 2 * DIFF_HEAD_DIM)
        dg_ref[r] = jax.nn.silu(proj(3))
        mq_ref[r] = rope(proj(4))
        mk = rope(proj(5))
        mk_ref[r] = mk.astype(_BF16)
        km_ref[0, j:j + 1, :] = jnp.mean(mk, axis=0, keepdims=True)
        store_transposed(mvt_ref, proj(6), MOBA_HEAD_DIM)
        mg_ref[r] = jax.nn.silu(proj(7))
        for c in range(8, N_CHUNKS):
            gates_ref[r, CHUNK * (c - 8):CHUNK * (c - 7)] = jax.nn.sigmoid(proj(c))


def _split_heads(q):
    lane = lax.broadcasted_iota(jnp.int32, q.shape, 1)
    zero = jnp.zeros_like(q)
    return jnp.where(lane < 64, q, zero), jnp.where(lane >= 64, q, zero)


def _tile_rows(i):
    return pl.ds(pl.multiple_of(i * ATT_TILE, ATT_TILE), ATT_TILE)


def _past_items(n_q):
    items = [(qi, t) for qi in range(1, n_q) for t in range(qi)]
    assert len(items) % PIPE_UNROLL == 0 and PIPE_UNROLL % 2 == 0
    items.append(items[-1])
    return (np.array([i[0] for i in items], np.int32), np.array([i[1] for i in items], np.int32))


_Stream = collections.namedtuple("_Stream", "s_ref acc_ref m_ref")


def _run_attention(qtab_ref, ttab_ref, streams, n_seg, k_ref, cm_ref, q_of, vt_of, bias_of,
                   interleave):
    n_q = streams[0].acc_ref.shape[0]
    seg_rows = ATT_TILE // n_seg
    assert interleave in ("part", "stream") and seg_rows % MXU_COLS == 0

    def softmax(st, qi, tile, slot, first):
        stream = streams[st]
        biases = [bias_of(st, qi, tile, j) for j in range(n_seg)]
        cands = [cm_ref[slot, st * n_seg + j] if b is None else cm_ref[slot, st * n_seg + j] + b
                 for j, b in enumerate(biases)]
        m_old = None if first else stream.m_ref[qi]
        m_new = functools.reduce(jnp.maximum, cands if first else [m_old] + cands)
        stream.m_ref[qi] = m_new
        p = [jnp.exp2(stream.s_ref[slot, seg_rows * j:seg_rows * (j + 1), :]
                      - (m_new if b is None else m_new - b)).astype(_BF16)
             for j, b in enumerate(biases)]
        p = p[0] if n_seg == 1 else jnp.concatenate(p, axis=0)
        return p, (None if first else jnp.exp2(m_old - m_new))

    def pieces(diagonal):
        if diagonal:
            return [(c, MXU_COLS, c + MXU_COLS) for c in range(0, ATT_TILE, MXU_COLS)]
        width = MXU_COLS if interleave == "part" else ATT_TILE
        return [(c, width, ATT_TILE) for c in range(0, ATT_TILE, width)]

    def qk_piece(st, k, qi, slot, diagonal, c0, width, keys):
        cols = slice(c0, c0 + width)
        rows = pl.ds(pl.multiple_of(qi * ATT_TILE + c0, width), width)
        s = lax.dot_general(k[:keys], q_of(st, rows), _NT, preferred_element_type=_F32)
        if diagonal:
            kpos = lax.broadcasted_iota(jnp.int32, s.shape, 0)
            qpos = lax.broadcasted_iota(jnp.int32, s.shape, 1) + c0
            s = jnp.where(kpos <= qpos, s, -jnp.inf)
        s_ref = streams[st].s_ref
        s_ref[slot, :keys, cols] = s
        if keys < ATT_TILE:
            s_ref[slot, keys:, cols] = jnp.full((ATT_TILE - keys, width), -jnp.inf, _F32)
        for j in range(n_seg):
            seg = s[seg_rows * j:seg_rows * (j + 1)]
            cm_ref[slot, st * n_seg + j, :, cols] = (
                jnp.max(seg, axis=0, keepdims=True) if seg.shape[0]
                else jnp.full((1, width), -jnp.inf, _F32))

    def pv_piece(st, vt, p, alpha, qi, c0, width, keys):
        cols = slice(c0, c0 + width)
        acc_ref = streams[st].acc_ref
        pv = jnp.dot(vt[:, :keys], p[:keys, cols], preferred_element_type=_F32)
        acc_ref[qi, :, cols] = pv if alpha is None else alpha[:, cols] * acc_ref[qi, :, cols] + pv

    def step(nxt, cur, first=False):
        k = None if nxt is None else k_ref[_tile_rows(nxt[1]), :]
        for st in range(len(streams)):
            qk_ops, pv_ops = [], []
            if nxt is not None:
                qk_ops = [functools.partial(qk_piece, st, k, nxt[0], nxt[2], nxt[3], *pc)
                          for pc in pieces(nxt[3])]
            if cur is not None:
                p, alpha = softmax(st, cur[0], cur[1], cur[2], first)
                vt = vt_of(st, cur[1])
                pv_ops = [functools.partial(pv_piece, st, vt, p, alpha, cur[0], *pc)
                          for pc in pieces(first)]
            if interleave == "part":
                ops = [op for pair in zip(qk_ops, pv_ops) for op in pair]
                ops += qk_ops[len(pv_ops):] + pv_ops[len(qk_ops):]
            else:
                ops = qk_ops + pv_ops
            for op in ops:
                op()

    step((0, 0, 0, True), None)

    def diag_items(j, _):
        for u in range(2):
            t = 2 * j + u
            step((t + 1, t + 1, (u + 1) % 2, True), (t, t, u), first=True)
        return 0

    lax.fori_loop(0, n_q // 2 - 1, diag_items, 0)
    step((n_q - 1, n_q - 1, 1, True), (n_q - 2, n_q - 2, 0), first=True)
    step((qtab_ref[0], ttab_ref[0], 0, False), (n_q - 1, n_q - 1, 1), first=True)

    def past_items(j, _):
        for u in range(PIPE_UNROLL):
            t = PIPE_UNROLL * j + u
            step((qtab_ref[t + 1], ttab_ref[t + 1], (u + 1) % 2, False),
                 (qtab_ref[t], ttab_ref[t], u % 2))
        return 0

    lax.fori_loop(0, n_q * (n_q - 1) // (2 * PIPE_UNROLL), past_items, 0)


def _diff_kernel(qtab_ref, ttab_ref, lq1_ref, lk1_ref, lq2_ref, lk2_ref, q_ref, k_ref, vt_ref,
                 dg_ref, sg_ref, o_ref, acc1_ref, acc2_ref, m1_ref, m2_ref, s1_ref, s2_ref, cm_ref):
    n_q = acc1_ref.shape[0]
    streams = (_Stream(s1_ref, acc1_ref, m1_ref), _Stream(s2_ref, acc2_ref, m2_ref))
    _run_attention(
        qtab_ref, ttab_ref, streams, 1, k_ref, cm_ref,
        q_of=lambda st, rows: _split_heads(q_ref[rows, :])[st],
        vt_of=lambda st, tile: vt_ref[0, tile],
        bias_of=lambda st, qi, tile, j: None,
        interleave="stream")

    dv = 2 * DIFF_HEAD_DIM
    lam = (jnp.exp(jnp.sum(lq1_ref[...] * lk1_ref[...], axis=-1, keepdims=True))
           - jnp.exp(jnp.sum(lq2_ref[...] * lk2_ref[...], axis=-1, keepdims=True)) + LAM_INIT)

    def finalize(qi, _):
        ot = (acc1_ref[qi, :dv] / acc1_ref[qi, dv:dv + 1]
              - lam * (acc2_ref[qi, :dv] / acc2_ref[qi, dv:dv + 1]))
        ms = jnp.mean(ot * ot, axis=0, keepdims=True)
        ot = ot * lax.rsqrt(ms + SUBLN_EPS) * sg_ref[...] * (1.0 - LAM_INIT)
        o_ref[_tile_rows(qi), :] = (ot.T * dg_ref[_tile_rows(qi), :]).astype(_BF16)
        return 0

    lax.fori_loop(0, n_q // 2, lambda j, c: finalize(2 * j + 1, finalize(2 * j, c)), 0)


def _moba_kernel(qtab_ref, ttab_ref, q_ref, k_ref, vt_ref, km_ref, mg_ref, o_ref,
                 acca_ref, accb_ref, ma_ref, mb_ref, sa_ref, sb_ref, cm_ref, bias_ref, qs_ref):
    n_q = acca_ref.shape[0]
    nb = km_ref.shape[0]
    per_tile = ATT_TILE // MOBA_BLOCK
    km_both = jnp.concatenate(_split_heads(km_ref[...]), axis=0)
    km_hi = km_both.astype(_BF16)
    km_hl = jnp.concatenate([km_hi, (km_both - km_hi.astype(_F32)).astype(_BF16)], axis=0)
    blk = lax.broadcasted_iota(jnp.int32, (nb, ATT_TILE), 0)
    col = lax.broadcasted_iota(jnp.int32, (nb, ATT_TILE), 1)
    col_blk = lax.shift_right_logical(col, MOBA_BLOCK.bit_length() - 1)

    def gate(qi, _):
        q = q_ref[_tile_rows(qi), :]
        own = per_tile * qi + col_blk
        q_hi = q.astype(_BF16)
        q_lo = (q - q_hi.astype(_F32)).astype(_BF16)
        g_hl = lax.dot_general(km_hl, q_hi, _NT, preferred_element_type=_F32)
        g_both = (g_hl[:2 * nb] + g_hl[2 * nb:]
                  + lax.dot_general(km_hi, q_lo, _NT, preferred_element_type=_F32))
        for hh in range(2):
            g = jnp.where(blk < own, g_both[nb * hh:nb * (hh + 1)], -jnp.inf)
            sel = jnp.zeros(g.shape, jnp.bool_)
            for _ in range(MOBA_TOPK):
                mx = jnp.max(g, axis=0, keepdims=True)
                first_idx = jnp.min(jnp.where(g == mx, blk, nb), axis=0, keepdims=True)
                pick = (blk == first_idx) & (mx > -jnp.inf)
                sel = sel | pick
                g = jnp.where(pick, -jnp.inf, g)
            bias_ref[hh, qi] = jnp.where(sel | (blk == own), 0.0, -jnp.inf)
        for hh, qh in enumerate(_split_heads((q * _SCORE_SCALE).astype(_BF16))):
            qs_ref[hh, _tile_rows(qi), :] = qh
        return 0

    lax.fori_loop(0, n_q // 4,
                  lambda j, c: gate(4 * j + 3, gate(4 * j + 2, gate(4 * j + 1, gate(4 * j, c)))), 0)

    streams = (_Stream(sa_ref, acca_ref, ma_ref), _Stream(sb_ref, accb_ref, mb_ref))
    _run_attention(
        qtab_ref, ttab_ref, streams, per_tile, k_ref, cm_ref,
        q_of=lambda st, rows: qs_ref[st, rows, :],
        vt_of=lambda st, tile: vt_ref[0, tile, MOBA_V_ROWS * st:MOBA_V_ROWS * (st + 1), :],
        bias_of=lambda st, qi, tile, j: bias_ref[st, qi, pl.ds(per_tile * tile + j, 1), :],
        interleave="part")

    dh = MOBA_HEAD_DIM

    def finalize(qi, _):
        ot = jnp.concatenate([acca_ref[qi, :dh] / acca_ref[qi, dh:dh + 1],
                              accb_ref[qi, :dh] / accb_ref[qi, dh:dh + 1]], axis=0)
        o_ref[_tile_rows(qi), :] = (ot.T * mg_ref[_tile_rows(qi), :]).astype(_BF16)
        return 0

    lax.fori_loop(0, n_q // 2, lambda j, c: finalize(2 * j + 1, finalize(2 * j, c)), 0)


def _out_kernel(oa_ref, ob_ref, gates_ref, x_ref, p_ref, wbd_ref, wbm_ref, wout_ref, wpg_ref,
                wple_ref, fg_ref, o_ref, wbd_bf, wbm_bf, wout_bf, wpg_bf, wple_bf):
    @pl.when(pl.program_id(0) == 0)
    def _():
        for src, dst in ((wbd_ref, wbd_bf), (wbm_ref, wbm_bf), (wout_ref, wout_bf),
                         (wpg_ref, wpg_bf), (wple_ref, wple_bf)):
            dst[...] = src[...].astype(_BF16)

    half = ROW_TILE // 2
    for r in (slice(0, half), slice(half, ROW_TILE)):
        ya = jnp.dot(oa_ref[r], wbd_bf[...], preferred_element_type=_F32)
        yb = jnp.dot(ob_ref[r], wbm_bf[...], preferred_element_type=_F32)
        merged = gates_ref[r, :D_MODEL] * ya + gates_ref[r, D_MODEL:] * yb
        x1 = x_ref[r] + jnp.dot(merged.astype(_BF16), wout_bf[...], preferred_element_type=_F32)
        t = jnp.dot(x1.astype(_BF16), wpg_bf[...], preferred_element_type=_F32)
        pe = jnp.dot(p_ref[r].astype(_BF16), wple_bf[...], preferred_element_type=_F32)
        x2 = x1 + jax.nn.sigmoid(t) * pe
        ms = jnp.mean(x2 * x2, axis=-1, keepdims=True)
        o_ref[r] = x2 * lax.rsqrt(ms + EPS) * fg_ref[...]


def _rope_lane_tables(seq):
    half = ROT_DIM // 2
    inv = ROPE_THETA ** (-np.arange(0, ROT_DIM, 2, dtype=np.float64) / ROT_DIM)
    ang = np.arange(seq, dtype=np.float64)[:, None] * inv[None, :]
    cos, sin = np.cos(ang), np.sin(ang)
    pad = MOBA_HEAD_DIM - ROT_DIM
    one, zero = np.ones((seq, pad)), np.zeros((seq, pad))
    zh = np.zeros((seq, half))
    cos64 = np.concatenate([cos, cos, one], axis=1)
    slo64 = np.concatenate([-sin, zh, zero], axis=1)
    shi64 = np.concatenate([zh, sin, zero], axis=1)
    return tuple(jnp.asarray(np.tile(t, (1, LANES // MOBA_HEAD_DIM)), _F32)
                 for t in (cos64, slo64, shi64))


def _resident(shape):
    return pl.BlockSpec(shape, lambda *_: (0,) * len(shape), pipeline_mode=pl.Buffered(1))


def _params(n_axes):
    return pltpu.CompilerParams(dimension_semantics=("arbitrary",) * n_axes,
                                vmem_limit_bytes=VMEM_LIMIT)


def kernel(x, p, norm_g, w_in, lambda_q1, lambda_k1, lambda_q2, lambda_k2, subln_g,
           w_branch_diff, w_branch_moba, w_out, w_ple, w_ple_gate, final_g):
    B, S, _ = x.shape
    assert w_in.shape == (1, D_MODEL, N_CHUNKS * CHUNK) and S % (2 * ATT_TILE) == 0
    rows = B * S
    n_row_tiles = rows // ROW_TILE
    n_kv = S // ATT_TILE
    n_blk = S // MOBA_BLOCK
    x2d = x.reshape(rows, D_MODEL)
    cosv, sin_lo, sin_hi = _rope_lane_tables(S)

    row_spec = lambda w: pl.BlockSpec((ROW_TILE, w), lambda i: (i, 0))
    tab_spec = pl.BlockSpec((ROW_TILE, LANES), lambda i: (i % n_kv, 0))
    dvt_rows, mvt_rows = DIFF_HEADS * DIFF_V_ROWS, MOBA_HEADS * MOBA_V_ROWS
    vt_spec = lambda r: pl.BlockSpec((1, 1, r, ATT_TILE), lambda i: (i // n_kv, i % n_kv, 0, 0))
    vt_shape = lambda r: jax.ShapeDtypeStruct((B, n_kv, r, ATT_TILE), _BF16)
    rows_bf16 = jax.ShapeDtypeStruct((rows, CHUNK), _BF16)
    rows_f32 = jax.ShapeDtypeStruct((rows, CHUNK), _F32)
    km_rows = ROW_TILE // MOBA_BLOCK
    (dq, dk, dvt, dgs, mq, mk, mvt, mgs, kmean, gates) = pl.pallas_call(
        _proj_kernel,
        grid=(n_row_tiles,),
        in_specs=[row_spec(D_MODEL), _resident((1, D_MODEL)), _resident((D_MODEL, N_CHUNKS * CHUNK)),
                  tab_spec, tab_spec, tab_spec],
        out_specs=[row_spec(CHUNK), row_spec(CHUNK), vt_spec(dvt_rows), row_spec(CHUNK), row_spec(CHUNK),
                   row_spec(CHUNK), vt_spec(mvt_rows), row_spec(CHUNK),
                   pl.BlockSpec((1, km_rows, CHUNK), lambda i: (i, 0, 0)),
                   row_spec(4 * CHUNK)],
        out_shape=[rows_bf16, rows_bf16, vt_shape(dvt_rows), rows_f32, rows_f32, rows_bf16,
                   vt_shape(mvt_rows), rows_f32,
                   jax.ShapeDtypeStruct((n_row_tiles, km_rows, CHUNK), _F32),
                   jax.ShapeDtypeStruct((rows, 4 * CHUNK), _F32)],
        compiler_params=_params(1),
        name="proj",
    )(x2d, norm_g[0].reshape(1, D_MODEL), w_in[0].astype(_BF16), cosv, sin_lo, sin_hi)
    kmean = kmean.reshape(B * n_blk, CHUNK)

    n_groups = CHUNK // LANES
    qtab, ttab = _past_items(n_kv)
    seq_spec = pl.BlockSpec((S, LANES), lambda b, g, *_: (b, g))
    v_spec = lambda r: pl.BlockSpec((1, n_kv, r, ATT_TILE), lambda b, g, *_: (b, 0, g, 0))
    lam_spec = pl.BlockSpec((1, DIFF_HEAD_DIM), lambda b, g, *_: (0, 0))
    acc = lambda r: pltpu.VMEM((n_kv, r, ATT_TILE), _F32)
    run_max = pltpu.VMEM((n_kv, 1, ATT_TILE), _F32)
    score_buf = pltpu.VMEM((2, ATT_TILE, ATT_TILE), _F32)
    col_max = lambda n: pltpu.VMEM((2, n, 1, ATT_TILE), _F32)

    o_a = pl.pallas_call(
        _diff_kernel,
        grid_spec=pltpu.PrefetchScalarGridSpec(
            num_scalar_prefetch=2,
            grid=(B, n_groups),
            in_specs=[lam_spec, lam_spec, lam_spec, lam_spec, seq_spec, seq_spec, v_spec(DIFF_V_ROWS),
                      seq_spec, pl.BlockSpec((2 * DIFF_HEAD_DIM, 1), lambda b, g, *_: (0, 0))],
            out_specs=seq_spec,
            scratch_shapes=[acc(DIFF_V_ROWS), acc(DIFF_V_ROWS), run_max, run_max, score_buf, score_buf,
                            col_max(2)],
        ),
        out_shape=rows_bf16,
        compiler_params=_params(2),
        name="diff_attn",
    )(qtab, ttab, lambda_q1[0:1], lambda_k1[0:1], lambda_q2[0:1], lambda_k2[0:1], dq, dk, dvt, dgs,
      subln_g[0].reshape(2 * DIFF_HEAD_DIM, 1))

    o_b = pl.pallas_call(
        _moba_kernel,
        grid_spec=pltpu.PrefetchScalarGridSpec(
            num_scalar_prefetch=2,
            grid=(B, n_groups),
            in_specs=[seq_spec, seq_spec, v_spec(2 * MOBA_V_ROWS),
                      pl.BlockSpec((n_blk, LANES), lambda b, g, *_: (b, g)), seq_spec],
            out_specs=seq_spec,
            scratch_shapes=[acc(MOBA_V_ROWS), acc(MOBA_V_ROWS), run_max, run_max, score_buf, score_buf,
                            col_max(2 * (ATT_TILE // MOBA_BLOCK)),
                            pltpu.VMEM((2, n_kv, n_blk, ATT_TILE), _F32),
                            pltpu.VMEM((2, S, LANES), _BF16)],
        ),
        out_shape=rows_bf16,
        compiler_params=_params(2),
        name="moba_attn",
    )(qtab, ttab, mq, mk, mvt, kmean, mgs)

    out = pl.pallas_call(
        _out_kernel,
        grid=(n_row_tiles,),
        in_specs=[row_spec(CHUNK), row_spec(CHUNK), row_spec(4 * CHUNK), row_spec(D_MODEL),
                  row_spec(PLE_DIM), _resident((DIFF_WIDTH, D_MODEL)), _resident((MOBA_WIDTH, D_MODEL)),
                  _resident((D_MODEL, D_MODEL)), _resident((D_MODEL, D_MODEL)),
                  _resident((PLE_DIM, D_MODEL)), _resident((1, D_MODEL))],
        out_specs=row_spec(D_MODEL),
        out_shape=jax.ShapeDtypeStruct((rows, D_MODEL), _F32),
        scratch_shapes=[pltpu.VMEM(s, _BF16) for s in ((DIFF_WIDTH, D_MODEL), (MOBA_WIDTH, D_MODEL),
                                                      (D_MODEL, D_MODEL), (D_MODEL, D_MODEL),
                                                      (PLE_DIM, D_MODEL))],
        compiler_params=_params(1),
        name="out_proj",
    )(o_a, o_b, gates, x2d, p[0].reshape(rows, PLE_DIM), w_branch_diff[0], w_branch_moba[0],
      w_out[0], w_ple_gate[0], w_ple[0], final_g.reshape(1, D_MODEL))
    return out.reshape(B, S, D_MODEL)
```

```python
import collections
import functools
import math

import jax
import jax.numpy as jnp
import numpy as np
from jax import lax
from jax.experimental import pallas as pl
from jax.experimental.pallas import tpu as pltpu

D_MODEL = 1024
PLE_DIM = 256
DIFF_HEADS = 4
DIFF_HEAD_DIM = 64
DIFF_WIDTH = DIFF_HEADS * 2 * DIFF_HEAD_DIM
MOBA_HEADS = 8
MOBA_HEAD_DIM = 64
MOBA_WIDTH = MOBA_HEADS * MOBA_HEAD_DIM
MOBA_BLOCK = 256
MOBA_TOPK = 3
ROT_DIM = 16
ROPE_THETA = 500000.0
EPS = 1e-6
SUBLN_EPS = 1e-5
LAM_INIT = 0.8 - 0.6 * math.exp(-0.3 * 0)

LANES = 128
MXU_COLS = 256
BF16_ROWS = 16
CHUNK = 512
N_CHUNKS = 12
ROW_TILE = 512
ATT_TILE = 512
PIPE_UNROLL = 24
VMEM_LIMIT = 56 * 1024 * 1024
_SCORE_SCALE = DIFF_HEAD_DIM ** -0.5 * math.log2(math.e)
DIFF_V_ROWS = 2 * DIFF_HEAD_DIM + BF16_ROWS
MOBA_V_ROWS = MOBA_HEAD_DIM + BF16_ROWS
assert DIFF_HEAD_DIM == MOBA_HEAD_DIM and ATT_TILE % MOBA_BLOCK == 0 and ROW_TILE == ATT_TILE

_F32 = jnp.float32
_BF16 = jnp.bfloat16
_NT = (((1,), (1,)), ((), ()))


def _rope_rows(a, cosv, sin_lo, sin_hi):
    outs = []
    for j in range(CHUNK // LANES):
        xs = a[:, LANES * j:LANES * (j + 1)]
        up = pltpu.roll(xs, LANES - ROT_DIM // 2, 1)
        dn = pltpu.roll(xs, ROT_DIM // 2, 1)
        outs.append(xs * cosv + up * sin_lo + dn * sin_hi)
    return jnp.concatenate(outs, axis=1)


def _proj_kernel(x_ref, g_ref, w_ref, cos_ref, slo_ref, shi_ref,
                 dq_ref, dk_ref, dvt_ref, dg_ref, mq_ref, mk_ref, mvt_ref, mg_ref,
                 km_ref, gates_ref):
    for j in range(ROW_TILE // MOBA_BLOCK):
        r = slice(MOBA_BLOCK * j, MOBA_BLOCK * (j + 1))
        x = x_ref[r]
        ms = jnp.mean(x * x, axis=-1, keepdims=True)
        h = (x * lax.rsqrt(ms + EPS) * g_ref[...]).astype(_BF16)
        rope = functools.partial(_rope_rows, cosv=cos_ref[r], sin_lo=slo_ref[r], sin_hi=shi_ref[r])

        def proj(c):
            return jnp.dot(h, w_ref[:, CHUNK * c:CHUNK * (c + 1)].astype(_BF16),
                           preferred_element_type=_F32)

        def store_transposed(ref, a, head_rows):
            at = a.T.astype(_BF16)
            ones = jnp.ones((BF16_ROWS, MOBA_BLOCK), _BF16)
            stride = head_rows + BF16_ROWS
            for g in range(CHUNK // head_rows):
                ref[0, 0, stride * g:stride * g + head_rows, r] = at[head_rows * g:head_rows * (g + 1)]
                ref[0, 0, stride * g + head_rows:stride * (g + 1), r] = ones

        dq_ref[r] = (rope(proj(0)) * _SCORE_SCALE).astype(_BF16)
        dk_ref[r] = rope(proj(1)).astype(_BF16)
        store_transposed(dvt_ref, proj(2), 2 * DIFF_HEAD_DIM)
        dg_ref[r] = jax.nn.silu(proj(3))
        mq_ref[r] = rope(proj(4))
        mk = rope(proj(5))
        mk_ref[r] = mk.astype(_BF16)
        km_ref[0, j:j + 1, :] = jnp.mean(mk, axis=0, keepdims=True)
        store_transposed(mvt_ref, proj(6), MOBA_HEAD_DIM)
        mg_ref[r] = jax.nn.silu(proj(7))
        for c in range(8, N_CHUNKS):
            gates_ref[r, CHUNK * (c - 8):CHUNK * (c - 7)] = jax.nn.sigmoid(proj(c))


def _split_heads(q):
    lane = lax.broadcasted_iota(jnp.int32, q.shape, 1)
    zero = jnp.zeros_like(q)
    return jnp.where(lane < 64, q, zero), jnp.where(lane >= 64, q, zero)


def _tile_rows(i):
    return pl.ds(pl.multiple_of(i * ATT_TILE, ATT_TILE), ATT_TILE)


def _past_items(n_q):
    items = [(qi, t) for qi in range(1, n_q) for t in range(qi)]
    assert len(items) % PIPE_UNROLL == 0 and PIPE_UNROLL % 2 == 0
    items.append(items[-1])
    return (np.array([i[0] for i in items], np.int32), np.array([i[1] for i in items], np.int32))


_Stream = collections.namedtuple("_Stream", "s_ref acc_ref m_ref")


def _run_attention(qtab_ref, ttab_ref, streams, n_seg, k_ref, cm_ref, q_of, vt_of, bias_of,
                   interleave):
    n_q = streams[0].acc_ref.shape[0]
    seg_rows = ATT_TILE // n_seg
    assert interleave in ("part", "stream") and seg_rows % MXU_COLS == 0

    def softmax(st, qi, tile, slot, first):
        stream = streams[st]
        biases = [bias_of(st, qi, tile, j) for j in range(n_seg)]
        cands = [cm_ref[slot, st * n_seg + j] if b is None else cm_ref[slot, st * n_seg + j] + b
                 for j, b in enumerate(biases)]
        m_old = None if first else stream.m_ref[qi]
        m_new = functools.reduce(jnp.maximum, cands if first else [m_old] + cands)
        stream.m_ref[qi] = m_new
        p = [jnp.exp2(stream.s_ref[slot, seg_rows * j:seg_rows * (j + 1), :]
                      - (m_new if b is None else m_new - b)).astype(_BF16)
             for j, b in enumerate(biases)]
        p = p[0] if n_seg == 1 else jnp.concatenate(p, axis=0)
        return p, (None if first else jnp.exp2(m_old - m_new))

    def pieces(diagonal):
        if diagonal:
            return [(c, MXU_COLS, c + MXU_COLS) for c in range(0, ATT_TILE, MXU_COLS)]
        width = MXU_COLS if interleave == "part" else ATT_TILE
        return [(c, width, ATT_TILE) for c in range(0, ATT_TILE, width)]

    def qk_piece(st, k, qi, slot, diagonal, c0, width, keys):
        cols = slice(c0, c0 + width)
        rows = pl.ds(pl.multiple_of(qi * ATT_TILE + c0, width), width)
        s = lax.dot_general(k[:keys], q_of(st, rows), _NT, preferred_element_type=_F32)
        if diagonal:
            kpos = lax.broadcasted_iota(jnp.int32, s.shape, 0)
            qpos = lax.broadcasted_iota(jnp.int32, s.shape, 1) + c0
            s = jnp.where(kpos <= qpos, s, -jnp.inf)
        s_ref = streams[st].s_ref
        s_ref[slot, :keys, cols] = s
        if keys < ATT_TILE:
            s_ref[slot, keys:, cols] = jnp.full((ATT_TILE - keys, width), -jnp.inf, _F32)
        for j in range(n_seg):
            seg = s[seg_rows * j:seg_rows * (j + 1)]
            cm_ref[slot, st * n_seg + j, :, cols] = (
                jnp.max(seg, axis=0, keepdims=True) if seg.shape[0]
                else jnp.full((1, width), -jnp.inf, _F32))

    def pv_piece(st, vt, p, alpha, qi, c0, width, keys):
        cols = slice(c0, c0 + width)
        acc_ref = streams[st].acc_ref
        pv = jnp.dot(vt[:, :keys], p[:keys, cols], preferred_element_type=_F32)
        acc_ref[qi, :, cols] = pv if alpha is None else alpha[:, cols] * acc_ref[qi, :, cols] + pv

    def step(nxt, cur, first=False):
        k = None if nxt is None else k_ref[_tile_rows(nxt[1]), :]
        for st in range(len(streams)):
            qk_ops, pv_ops = [], []
            if nxt is not None:
                qk_ops = [functools.partial(qk_piece, st, k, nxt[0], nxt[2], nxt[3], *pc)
                          for pc in pieces(nxt[3])]
            if cur is not None:
                p, alpha = softmax(st, cur[0], cur[1], cur[2], first)
                vt = vt_of(st, cur[1])
                pv_ops = [functools.partial(pv_piece, st, vt, p, alpha, cur[0], *pc)
                          for pc in pieces(first)]
            if interleave == "part":
                ops = [op for pair in zip(qk_ops, pv_ops) for op in pair]
                ops += qk_ops[len(pv_ops):] + pv_ops[len(qk_ops):]
            else:
                ops = qk_ops + pv_ops
            for op in ops:
                op()

    step((0, 0, 0, True), None)

    def diag_items(j, _):
        for u in range(2):
            t = 2 * j + u
            step((t + 1, t + 1, (u + 1) % 2, True), (t, t, u), first=True)
        return 0

    lax.fori_loop(0, n_q // 2 - 1, diag_items, 0)
    step((n_q - 1, n_q - 1, 1, True), (n_q - 2, n_q - 2, 0), first=True)
    step((qtab_ref[0], ttab_ref[0], 0, False), (n_q - 1, n_q - 1, 1), first=True)

    def past_items(j, _):
        for u in range(PIPE_UNROLL):
            t = PIPE_UNROLL * j + u
            step((qtab_ref[t + 1], ttab_ref[t + 1], (u + 1) % 2, False),
                 (qtab_ref[t], ttab_ref[t], u % 2))
        return 0

    lax.fori_loop(0, n_q * (n_q - 1) // (2 * PIPE_UNROLL), past_items, 0)


def _diff_kernel(qtab_ref, ttab_ref, lq1_ref, lk1_ref, lq2_ref, lk2_ref, q_ref, k_ref, vt_ref,
                 dg_ref, sg_ref, o_ref, acc1_ref, acc2_ref, m1_ref, m2_ref, s1_ref, s2_ref, cm_ref):
    n_q = acc1_ref.shape[0]
    streams = (_Stream(s1_ref, acc1_ref, m1_ref), _Stream(s2_ref, acc2_ref, m2_ref))
    _run_attention(
        qtab_ref, ttab_ref, streams, 1, k_ref, cm_ref,
        q_of=lambda st, rows: _split_heads(q_ref[rows, :])[st],
        vt_of=lambda st, tile: vt_ref[0, tile],
        bias_of=lambda st, qi, tile, j: None,
        interleave="stream")

    dv = 2 * DIFF_HEAD_DIM
    lam = (jnp.exp(jnp.sum(lq1_ref[...] * lk1_ref[...], axis=-1, keepdims=True))
           - jnp.exp(jnp.sum(lq2_ref[...] * lk2_ref[...], axis=-1, keepdims=True)) + LAM_INIT)

    def finalize(qi, _):
        ot = (acc1_ref[qi, :dv] / acc1_ref[qi, dv:dv + 1]
              - lam * (acc2_ref[qi, :dv] / acc2_ref[qi, dv:dv + 1]))
        ms = jnp.mean(ot * ot, axis=0, keepdims=True)
        ot = ot * lax.rsqrt(ms + SUBLN_EPS) * sg_ref[...] * (1.0 - LAM_INIT)
        o_ref[_tile_rows(qi), :] = (ot.T * dg_ref[_tile_rows(qi), :]).astype(_BF16)
        return 0

    lax.fori_loop(0, n_q // 2, lambda j, c: finalize(2 * j + 1, finalize(2 * j, c)), 0)


def _moba_kernel(qtab_ref, ttab_ref, q_ref, k_ref, vt_ref, km_ref, mg_ref, o_ref,
                 acca_ref, accb_ref, ma_ref, mb_ref, sa_ref, sb_ref, cm_ref, bias_ref, qs_ref):
    n_q = acca_ref.shape[0]
    nb = km_ref.shape[0]
    per_tile = ATT_TILE // MOBA_BLOCK
    km_both = jnp.concatenate(_split_heads(km_ref[...]), axis=0)
    km_hi = km_both.astype(_BF16)
    km_hl = jnp.concatenate([km_hi, (km_both - km_hi.astype(_F32)).astype(_BF16)], axis=0)
    blk = lax.broadcasted_iota(jnp.int32, (nb, ATT_TILE), 0)
    col = lax.broadcasted_iota(jnp.int32, (nb, ATT_TILE), 1)
    col_blk = lax.shift_right_logical(col, MOBA_BLOCK.bit_length() - 1)

    def gate(qi, _):
        q = q_ref[_tile_rows(qi), :]
        own = per_tile * qi + col_blk
        q_hi = q.astype(_BF16)
        q_lo = (q - q_hi.astype(_F32)).astype(_BF16)
        g_hl = lax.dot_general(km_hl, q_hi, _NT, preferred_element_type=_F32)
        g_both = (g_hl[:2 * nb] + g_hl[2 * nb:]
                  + lax.dot_general(km_hi, q_lo, _NT, preferred_element_type=_F32))
        for hh in range(2):
            g = jnp.where(blk < own, g_both[nb * hh:nb * (hh + 1)], -jnp.inf)
            sel = jnp.zeros(g.shape, jnp.bool_)
            for _ in range(MOBA_TOPK):
                mx = jnp.max(g, axis=0, keepdims=True)
                first_idx = jnp.min(jnp.where(g == mx, blk, nb), axis=0, keepdims=True)
                pick = (blk == first_idx) & (mx > -jnp.inf)
                sel = sel | pick
                g = jnp.where(pick, -jnp.inf, g)
            bias_ref[hh, qi] = jnp.where(sel | (blk == own), 0.0, -jnp.inf)
        for hh, qh in enumerate(_split_heads((q * _SCORE_SCALE).astype(_BF16))):
            qs_ref[hh, _tile_rows(qi), :] = qh
        return 0

    lax.fori_loop(0, n_q // 4,
                  lambda j, c: gate(4 * j + 3, gate(4 * j + 2, gate(4 * j + 1, gate(4 * j, c)))), 0)

    streams = (_Stream(sa_ref, acca_ref, ma_ref), _Stream(sb_ref, accb_ref, mb_ref))
    _run_attention(
        qtab_ref, ttab_ref, streams, per_tile, k_ref, cm_ref,
        q_of=lambda st, rows: qs_ref[st, rows, :],
        vt_of=lambda st, tile: vt_ref[0, tile, MOBA_V_ROWS * st:MOBA_V_ROWS * (st + 1), :],
        bias_of=lambda st, qi, tile, j: bias_ref[st, qi, pl.ds(per_tile * tile + j, 1), :],
        interleave="part")

    dh = MOBA_HEAD_DIM

    def finalize(qi, _):
        ot = jnp.concatenate([acca_ref[qi, :dh] / acca_ref[qi, dh:dh + 1],
                              accb_ref[qi, :dh] / accb_ref[qi, dh:dh + 1]], axis=0)
        o_ref[_tile_rows(qi), :] = (ot.T * mg_ref[_tile_rows(qi), :]).astype(_BF16)
        return 0

    lax.fori_loop(0, n_q // 2, lambda j, c: finalize(2 * j + 1, finalize(2 * j, c)), 0)


def _out_kernel(oa_ref, ob_ref, gates_ref, x_ref, p_ref, wbd_ref, wbm_ref, wout_ref, wpg_ref,
                wple_ref, fg_ref, o_ref, wbd_bf, wbm_bf, wout_bf, wpg_bf, wple_bf):
    @pl.when(pl.program_id(0) == 0)
    def _():
        for src, dst in ((wbd_ref, wbd_bf), (wbm_ref, wbm_bf), (wout_ref, wout_bf),
                         (wpg_ref, wpg_bf), (wple_ref, wple_bf)):
            dst[...] = src[...].astype(_BF16)

    half = ROW_TILE // 2
    for r in (slice(0, half), slice(half, ROW_TILE)):
        ya = jnp.dot(oa_ref[r], wbd_bf[...], preferred_element_type=_F32)
        yb = jnp.dot(ob_ref[r], wbm_bf[...], preferred_element_type=_F32)
        merged = gates_ref[r, :D_MODEL] * ya + gates_ref[r, D_MODEL:] * yb
        x1 = x_ref[r] + jnp.dot(merged.astype(_BF16), wout_bf[...], preferred_element_type=_F32)
        t = jnp.dot(x1.astype(_BF16), wpg_bf[...], preferred_element_type=_F32)
        pe = jnp.dot(p_ref[r].astype(_BF16), wple_bf[...], preferred_element_type=_F32)
        x2 = x1 + jax.nn.sigmoid(t) * pe
        ms = jnp.mean(x2 * x2, axis=-1, keepdims=True)
        o_ref[r] = x2 * lax.rsqrt(ms + EPS) * fg_ref[...]


def _rope_lane_tables(seq):
    half = ROT_DIM // 2
    inv = ROPE_THETA ** (-np.arange(0, ROT_DIM, 2, dtype=np.float64) / ROT_DIM)
    ang = np.arange(seq, dtype=np.float64)[:, None] * inv[None, :]
    cos, sin = np.cos(ang), np.sin(ang)
    pad = MOBA_HEAD_DIM - ROT_DIM
    one, zero = np.ones((seq, pad)), np.zeros((seq, pad))
    zh = np.zeros((seq, half))
    cos64 = np.concatenate([cos, cos, one], axis=1)
    slo64 = np.concatenate([-sin, zh, zero], axis=1)
    shi64 = np.concatenate([zh, sin, zero], axis=1)
    return tuple(jnp.asarray(np.tile(t, (1, LANES // MOBA_HEAD_DIM)), _F32)
                 for t in (cos64, slo64, shi64))


def _resident(shape):
    return pl.BlockSpec(shape, lambda *_: (0,) * len(shape), pipeline_mode=pl.Buffered(1))


def _params(n_axes):
    return pltpu.CompilerParams(dimension_semantics=("arbitrary",) * n_axes,
                                vmem_limit_bytes=VMEM_LIMIT)


def kernel(x, p, norm_g, w_in, lambda_q1, lambda_k1, lambda_q2, lambda_k2, subln_g,
           w_branch_diff, w_branch_moba, w_out, w_ple, w_ple_gate, final_g):
    B, S, _ = x.shape
    assert w_in.shape == (1, D_MODEL, N_CHUNKS * CHUNK) and S % (2 * ATT_TILE) == 0
    rows = B * S
    n_row_tiles = rows // ROW_TILE
    n_kv = S // ATT_TILE
    n_blk = S // MOBA_BLOCK
    x2d = x.reshape(rows, D_MODEL)
    cosv, sin_lo, sin_hi = _rope_lane_tables(S)

    row_spec = lambda w: pl.BlockSpec((ROW_TILE, w), lambda i: (i, 0))
    tab_spec = pl.BlockSpec((ROW_TILE, LANES), lambda i: (i % n_kv, 0))
    dvt_rows, mvt_rows = DIFF_HEADS * DIFF_V_ROWS, MOBA_HEADS * MOBA_V_ROWS
    vt_spec = lambda r: pl.BlockSpec((1, 1, r, ATT_TILE), lambda i: (i // n_kv, i % n_kv, 0, 0))
    vt_shape = lambda r: jax.ShapeDtypeStruct((B, n_kv, r, ATT_TILE), _BF16)
    rows_bf16 = jax.ShapeDtypeStruct((rows, CHUNK), _BF16)
    rows_f32 = jax.ShapeDtypeStruct((rows, CHUNK), _F32)
    km_rows = ROW_TILE // MOBA_BLOCK
    (dq, dk, dvt, dgs, mq, mk, mvt, mgs, kmean, gates) = pl.pallas_call(
        _proj_kernel,
        grid=(n_row_tiles,),
        in_specs=[row_spec(D_MODEL), _resident((1, D_MODEL)), _resident((D_MODEL, N_CHUNKS * CHUNK)),
                  tab_spec, tab_spec, tab_spec],
        out_specs=[row_spec(CHUNK), row_spec(CHUNK), vt_spec(dvt_rows), row_spec(CHUNK), row_spec(CHUNK),
                   row_spec(CHUNK), vt_spec(mvt_rows), row_spec(CHUNK),
                   pl.BlockSpec((1, km_rows, CHUNK), lambda i: (i, 0, 0)),
                   row_spec(4 * CHUNK)],
        out_shape=[rows_bf16, rows_bf16, vt_shape(dvt_rows), rows_f32, rows_f32, rows_bf16,
                   vt_shape(mvt_rows), rows_f32,
                   jax.ShapeDtypeStruct((n_row_tiles, km_rows, CHUNK), _F32),
                   jax.ShapeDtypeStruct((rows, 4 * CHUNK), _F32)],
        compiler_params=_params(1),
        name="proj",
    )(x2d, norm_g[0].reshape(1, D_MODEL), w_in[0], cosv, sin_lo, sin_hi)
    kmean = kmean.reshape(B * n_blk, CHUNK)

    n_groups = CHUNK // LANES
    qtab, ttab = _past_items(n_kv)
    seq_spec = pl.BlockSpec((S, LANES), lambda b, g, *_: (b, g))
    v_spec = lambda r: pl.BlockSpec((1, n_kv, r, ATT_TILE), lambda b, g, *_: (b, 0, g, 0))
    lam_spec = pl.BlockSpec((1, DIFF_HEAD_DIM), lambda b, g, *_: (0, 0))
    acc = lambda r: pltpu.VMEM((n_kv, r, ATT_TILE), _F32)
    run_max = pltpu.VMEM((n_kv, 1, ATT_TILE), _F32)
    score_buf = pltpu.VMEM((2, ATT_TILE, ATT_TILE), _F32)
    col_max = lambda n: pltpu.VMEM((2, n, 1, ATT_TILE), _F32)

    o_a = pl.pallas_call(
        _diff_kernel,
        grid_spec=pltpu.PrefetchScalarGridSpec(
            num_scalar_prefetch=2,
            grid=(B, n_groups),
            in_specs=[lam_spec, lam_spec, lam_spec, lam_spec, seq_spec, seq_spec, v_spec(DIFF_V_ROWS),
                      seq_spec, pl.BlockSpec((2 * DIFF_HEAD_DIM, 1), lambda b, g, *_: (0, 0))],
            out_specs=seq_spec,
            scratch_shapes=[acc(DIFF_V_ROWS), acc(DIFF_V_ROWS), run_max, run_max, score_buf, score_buf,
                            col_max(2)],
        ),
        out_shape=rows_bf16,
        compiler_params=_params(2),
        name="diff_attn",
    )(qtab, ttab, lambda_q1[0:1], lambda_k1[0:1], lambda_q2[0:1], lambda_k2[0:1], dq, dk, dvt, dgs,
      subln_g[0].reshape(2 * DIFF_HEAD_DIM, 1))

    o_b = pl.pallas_call(
        _moba_kernel,
        grid_spec=pltpu.PrefetchScalarGridSpec(
            num_scalar_prefetch=2,
            grid=(B, n_groups),
            in_specs=[seq_spec, seq_spec, v_spec(2 * MOBA_V_ROWS),
                      pl.BlockSpec((n_blk, LANES), lambda b, g, *_: (b, g)), seq_spec],
            out_specs=seq_spec,
            scratch_shapes=[acc(MOBA_V_ROWS), acc(MOBA_V_ROWS), run_max, run_max, score_buf, score_buf,
                            col_max(2 * (ATT_TILE // MOBA_BLOCK)),
                            pltpu.VMEM((2, n_kv, n_blk, ATT_TILE), _F32),
                            pltpu.VMEM((2, S, LANES), _BF16)],
        ),
        out_shape=rows_bf16,
        compiler_params=_params(2),
        name="moba_attn",
    )(qtab, ttab, mq, mk, mvt, kmean, mgs)

    out = pl.pallas_call(
        _out_kernel,
        grid=(n_row_tiles,),
        in_specs=[row_spec(CHUNK), row_spec(CHUNK), row_spec(4 * CHUNK), row_spec(D_MODEL),
                  row_spec(PLE_DIM), _resident((DIFF_WIDTH, D_MODEL)), _resident((MOBA_WIDTH, D_MODEL)),
                  _resident((D_MODEL, D_MODEL)), _resident((D_MODEL, D_MODEL)),
                  _resident((PLE_DIM, D_MODEL)), _resident((1, D_MODEL))],
        out_specs=row_spec(D_MODEL),
        out_shape=jax.ShapeDtypeStruct((rows, D_MODEL), _F32),
        scratch_shapes=[pltpu.VMEM(s, _BF16) for s in ((DIFF_WIDTH, D_MODEL), (MOBA_WIDTH, D_MODEL),
                                                      (D_MODEL, D_MODEL), (D_MODEL, D_MODEL),
                                                      (PLE_DIM, D_MODEL))],
        compiler_params=_params(1),
        name="out_proj",
    )(o_a, o_b, gates, x2d, p[0].reshape(rows, PLE_DIM), w_branch_diff[0], w_branch_moba[0],
      w_out[0], w_ple_gate[0], w_ple[0], final_g.reshape(1, D_MODEL))
    return out.reshape(B, S, D_MODEL)
```

```python
import collections
import functools
import math

import jax
import jax.numpy as jnp
import numpy as np
from jax import lax
from jax.experimental import pallas as pl
from jax.experimental.pallas import tpu as pltpu

D_MODEL = 1024
PLE_DIM = 256
DIFF_HEADS = 4
DIFF_HEAD_DIM = 64
DIFF_WIDTH = DIFF_HEADS * 2 * DIFF_HEAD_DIM
MOBA_HEADS = 8
MOBA_HEAD_DIM = 64
MOBA_WIDTH = MOBA_HEADS * MOBA_HEAD_DIM
MOBA_BLOCK = 256
MOBA_TOPK = 3
ROT_DIM = 16
ROPE_THETA = 500000.0
EPS = 1e-6
SUBLN_EPS = 1e-5
LAM_INIT = 0.8 - 0.6 * math.exp(-0.3 * 0)

LANES = 128
MXU_COLS = 256
BF16_ROWS = 16
CHUNK = 512
N_CHUNKS = 12
ROW_TILE = 512
ATT_TILE = 512
PIPE_UNROLL = 24
VMEM_LIMIT = 56 * 1024 * 1024
_SCORE_SCALE = DIFF_HEAD_DIM ** -0.5 * math.log2(math.e)
DIFF_V_ROWS = 2 * DIFF_HEAD_DIM + BF16_ROWS
MOBA_V_ROWS = MOBA_HEAD_DIM + BF16_ROWS
assert DIFF_HEAD_DIM == MOBA_HEAD_DIM and ATT_TILE % MOBA_BLOCK == 0 and ROW_TILE == ATT_TILE

_F32 = jnp.float32
_BF16 = jnp.bfloat16
_NT = (((1,), (1,)), ((), ()))


def _rope_rows(a, cosv, sin_lo, sin_hi):
    outs = []
    for j in range(CHUNK // LANES):
        xs = a[:, LANES * j:LANES * (j + 1)]
        up = pltpu.roll(xs, LANES - ROT_DIM // 2, 1)
        dn = pltpu.roll(xs, ROT_DIM // 2, 1)
        outs.append(xs * cosv + up * sin_lo + dn * sin_hi)
    return jnp.concatenate(outs, axis=1)


def _proj_kernel(x_ref, g_ref, w_ref, cos_ref, slo_ref, shi_ref,
                 dq_ref, dk_ref, dvt_ref, dg_ref, mq_ref, mk_ref, mvt_ref, mg_ref,
                 km_ref, gates_ref):
    for j in range(ROW_TILE // MOBA_BLOCK):
        r = slice(MOBA_BLOCK * j, MOBA_BLOCK * (j + 1))
        x = x_ref[r]
        ms = jnp.mean(x * x, axis=-1, keepdims=True)
        h = (x * lax.rsqrt(ms + EPS) * g_ref[...]).astype(_BF16)
        rope = functools.partial(_rope_rows, cosv=cos_ref[r], sin_lo=slo_ref[r], sin_hi=shi_ref[r])

        def proj(c):
            return jnp.dot(h, w_ref[:, CHUNK * c:CHUNK * (c + 1)].astype(_BF16),
                           preferred_element_type=_F32)

        def store_transposed(ref, a, head_rows):
            at = a.T.astype(_BF16)
            ones = jnp.ones((BF16_ROWS, MOBA_BLOCK), _BF16)
            stride = head_rows + BF16_ROWS
            for g in range(CHUNK // head_rows):
                ref[0, 0, stride * g:stride * g + head_rows, r] = at[head_rows * g:head_rows * (g + 1)]
                ref[0, 0, stride * g + head_rows:stride * (g + 1), r] = ones

        dq_ref[r] = (rope(proj(0)) * _SCORE_SCALE).astype(_BF16)
        dk_ref[r] = rope(proj(1)).astype(_BF16)
        store_transposed(dvt_ref, proj(2), 2 * DIFF_HEAD_DIM)
        dg_ref[r] = jax.nn.silu(proj(3))
        mq_ref[r] = rope(proj(4))
        mk = rope(proj(5))
        mk_ref[r] = mk.astype(_BF16)
        km_ref[0, j:j + 1, :] = jnp.mean(mk, axis=0, keepdims=True)
        store_transposed(mvt_ref, proj(6), MOBA_HEAD_DIM)
        mg_ref[r] = jax.nn.silu(proj(7))
        for c in range(8, N_CHUNKS):
            gates_ref[r, CHUNK * (c - 8):CHUNK * (c - 7)] = jax.nn.sigmoid(proj(c)).astype(_BF16)


def _split_heads(q):
    lane = lax.broadcasted_iota(jnp.int32, q.shape, 1)
    zero = jnp.zeros_like(q)
    return jnp.where(lane < 64, q, zero), jnp.where(lane >= 64, q, zero)


def _tile_rows(i):
    return pl.ds(pl.multiple_of(i * ATT_TILE, ATT_TILE), ATT_TILE)


def _past_items(n_q):
    items = [(qi, t) for qi in range(1, n_q) for t in range(qi)]
    assert len(items) % PIPE_UNROLL == 0 and PIPE_UNROLL % 2 == 0
    items.append(items[-1])
    return (np.array([i[0] for i in items], np.int32), np.array([i[1] for i in items], np.int32))


_Stream = collections.namedtuple("_Stream", "s_ref acc_ref m_ref")


def _run_attention(qtab_ref, ttab_ref, streams, n_seg, k_ref, cm_ref, q_of, vt_of, bias_of,
                   interleave):
    n_q = streams[0].acc_ref.shape[0]
    seg_rows = ATT_TILE // n_seg
    assert interleave in ("part", "stream") and seg_rows % MXU_COLS == 0

    def softmax(st, qi, tile, slot, first):
        stream = streams[st]
        biases = [bias_of(st, qi, tile, j) for j in range(n_seg)]
        cands = [cm_ref[slot, st * n_seg + j] if b is None else cm_ref[slot, st * n_seg + j] + b
                 for j, b in enumerate(biases)]
        m_old = None if first else stream.m_ref[qi]
        m_new = functools.reduce(jnp.maximum, cands if first else [m_old] + cands)
        stream.m_ref[qi] = m_new
        p = [jnp.exp2(stream.s_ref[slot, seg_rows * j:seg_rows * (j + 1), :]
                      - (m_new if b is None else m_new - b)).astype(_BF16)
             for j, b in enumerate(biases)]
        p = p[0] if n_seg == 1 else jnp.concatenate(p, axis=0)
        return p, (None if first else jnp.exp2(m_old - m_new))

    def pieces(diagonal):
        if diagonal:
            return [(c, MXU_COLS, c + MXU_COLS) for c in range(0, ATT_TILE, MXU_COLS)]
        width = MXU_COLS if interleave == "part" else ATT_TILE
        return [(c, width, ATT_TILE) for c in range(0, ATT_TILE, width)]

    def qk_piece(st, k, qi, slot, diagonal, c0, width, keys):
        cols = slice(c0, c0 + width)
        rows = pl.ds(pl.multiple_of(qi * ATT_TILE + c0, width), width)
        s = lax.dot_general(k[:keys], q_of(st, rows), _NT, preferred_element_type=_F32)
        if diagonal:
            kpos = lax.broadcasted_iota(jnp.int32, s.shape, 0)
            qpos = lax.broadcasted_iota(jnp.int32, s.shape, 1) + c0
            s = jnp.where(kpos <= qpos, s, -jnp.inf)
        s_ref = streams[st].s_ref
        s_ref[slot, :keys, cols] = s
        if keys < ATT_TILE:
            s_ref[slot, keys:, cols] = jnp.full((ATT_TILE - keys, width), -jnp.inf, _F32)
        for j in range(n_seg):
            seg = s[seg_rows * j:seg_rows * (j + 1)]
            cm_ref[slot, st * n_seg + j, :, cols] = (
                jnp.max(seg, axis=0, keepdims=True) if seg.shape[0]
                else jnp.full((1, width), -jnp.inf, _F32))

    def pv_piece(st, vt, p, alpha, qi, c0, width, keys):
        cols = slice(c0, c0 + width)
        acc_ref = streams[st].acc_ref
        pv = jnp.dot(vt[:, :keys], p[:keys, cols], preferred_element_type=_F32)
        acc_ref[qi, :, cols] = pv if alpha is None else alpha[:, cols] * acc_ref[qi, :, cols] + pv

    def step(nxt, cur, first=False):
        k = None if nxt is None else k_ref[_tile_rows(nxt[1]), :]
        for st in range(len(streams)):
            qk_ops, pv_ops = [], []
            if nxt is not None:
                qk_ops = [functools.partial(qk_piece, st, k, nxt[0], nxt[2], nxt[3], *pc)
                          for pc in pieces(nxt[3])]
            if cur is not None:
                p, alpha = softmax(st, cur[0], cur[1], cur[2], first)
                vt = vt_of(st, cur[1])
                pv_ops = [functools.partial(pv_piece, st, vt, p, alpha, cur[0], *pc)
                          for pc in pieces(first)]
            if interleave == "part":
                ops = [op for pair in zip(qk_ops, pv_ops) for op in pair]
                ops += qk_ops[len(pv_ops):] + pv_ops[len(qk_ops):]
            else:
                ops = qk_ops + pv_ops
            for op in ops:
                op()

    step((0, 0, 0, True), None)

    def diag_items(j, _):
        for u in range(2):
            t = 2 * j + u
            step((t + 1, t + 1, (u + 1) % 2, True), (t, t, u), first=True)
        return 0

    lax.fori_loop(0, n_q // 2 - 1, diag_items, 0)
    step((n_q - 1, n_q - 1, 1, True), (n_q - 2, n_q - 2, 0), first=True)
    step((qtab_ref[0], ttab_ref[0], 0, False), (n_q - 1, n_q - 1, 1), first=True)

    def past_items(j, _):
        for u in range(PIPE_UNROLL):
            t = PIPE_UNROLL * j + u
            step((qtab_ref[t + 1], ttab_ref[t + 1], (u + 1) % 2, False),
                 (qtab_ref[t], ttab_ref[t], u % 2))
        return 0

    lax.fori_loop(0, n_q * (n_q - 1) // (2 * PIPE_UNROLL), past_items, 0)


def _diff_kernel(qtab_ref, ttab_ref, lq1_ref, lk1_ref, lq2_ref, lk2_ref, q_ref, k_ref, vt_ref,
                 dg_ref, sg_ref, o_ref, acc1_ref, acc2_ref, m1_ref, m2_ref, s1_ref, s2_ref, cm_ref):
    n_q = acc1_ref.shape[0]
    streams = (_Stream(s1_ref, acc1_ref, m1_ref), _Stream(s2_ref, acc2_ref, m2_ref))
    _run_attention(
        qtab_ref, ttab_ref, streams, 1, k_ref, cm_ref,
        q_of=lambda st, rows: _split_heads(q_ref[rows, :])[st],
        vt_of=lambda st, tile: vt_ref[0, tile],
        bias_of=lambda st, qi, tile, j: None,
        interleave="stream")

    dv = 2 * DIFF_HEAD_DIM
    lam = (jnp.exp(jnp.sum(lq1_ref[...] * lk1_ref[...], axis=-1, keepdims=True))
           - jnp.exp(jnp.sum(lq2_ref[...] * lk2_ref[...], axis=-1, keepdims=True)) + LAM_INIT)

    def finalize(qi, _):
        ot = (acc1_ref[qi, :dv] / acc1_ref[qi, dv:dv + 1]
              - lam * (acc2_ref[qi, :dv] / acc2_ref[qi, dv:dv + 1]))
        ms = jnp.mean(ot * ot, axis=0, keepdims=True)
        ot = ot * lax.rsqrt(ms + SUBLN_EPS) * sg_ref[...] * (1.0 - LAM_INIT)
        o_ref[_tile_rows(qi), :] = (ot.T * dg_ref[_tile_rows(qi), :]).astype(_BF16)
        return 0

    lax.fori_loop(0, n_q // 2, lambda j, c: finalize(2 * j + 1, finalize(2 * j, c)), 0)


def _moba_kernel(qtab_ref, ttab_ref, q_ref, k_ref, vt_ref, km_ref, mg_ref, o_ref,
                 acca_ref, accb_ref, ma_ref, mb_ref, sa_ref, sb_ref, cm_ref, bias_ref, qs_ref):
    n_q = acca_ref.shape[0]
    nb = km_ref.shape[0]
    per_tile = ATT_TILE // MOBA_BLOCK
    km_both = jnp.concatenate(_split_heads(km_ref[...]), axis=0)
    km_hi = km_both.astype(_BF16)
    km_hl = jnp.concatenate([km_hi, (km_both - km_hi.astype(_F32)).astype(_BF16)], axis=0)
    blk = lax.broadcasted_iota(jnp.int32, (nb, ATT_TILE), 0)
    col = lax.broadcasted_iota(jnp.int32, (nb, ATT_TILE), 1)
    col_blk = lax.shift_right_logical(col, MOBA_BLOCK.bit_length() - 1)

    def gate(qi, _):
        q = q_ref[_tile_rows(qi), :]
        own = per_tile * qi + col_blk
        q_hi = q.astype(_BF16)
        q_lo = (q - q_hi.astype(_F32)).astype(_BF16)
        g_hl = lax.dot_general(km_hl, q_hi, _NT, preferred_element_type=_F32)
        g_both = (g_hl[:2 * nb] + g_hl[2 * nb:]
                  + lax.dot_general(km_hi, q_lo, _NT, preferred_element_type=_F32))
        for hh in range(2):
            g = jnp.where(blk < own, g_both[nb * hh:nb * (hh + 1)], -jnp.inf)
            sel = jnp.zeros(g.shape, jnp.bool_)
            for _ in range(MOBA_TOPK):
                mx = jnp.max(g, axis=0, keepdims=True)
                first_idx = jnp.min(jnp.where(g == mx, blk, nb), axis=0, keepdims=True)
                pick = (blk == first_idx) & (mx > -jnp.inf)
                sel = sel | pick
                g = jnp.where(pick, -jnp.inf, g)
            bias_ref[hh, qi] = jnp.where(sel | (blk == own), 0.0, -jnp.inf)
        for hh, qh in enumerate(_split_heads((q * _SCORE_SCALE).astype(_BF16))):
            qs_ref[hh, _tile_rows(qi), :] = qh
        return 0

    lax.fori_loop(0, n_q // 4,
                  lambda j, c: gate(4 * j + 3, gate(4 * j + 2, gate(4 * j + 1, gate(4 * j, c)))), 0)

    streams = (_Stream(sa_ref, acca_ref, ma_ref), _Stream(sb_ref, accb_ref, mb_ref))
    _run_attention(
        qtab_ref, ttab_ref, streams, per_tile, k_ref, cm_ref,
        q_of=lambda st, rows: qs_ref[st, rows, :],
        vt_of=lambda st, tile: vt_ref[0, tile, MOBA_V_ROWS * st:MOBA_V_ROWS * (st + 1), :],
        bias_of=lambda st, qi, tile, j: bias_ref[st, qi, pl.ds(per_tile * tile + j, 1), :],
        interleave="part")

    dh = MOBA_HEAD_DIM

    def finalize(qi, _):
        ot = jnp.concatenate([acca_ref[qi, :dh] / acca_ref[qi, dh:dh + 1],
                              accb_ref[qi, :dh] / accb_ref[qi, dh:dh + 1]], axis=0)
        o_ref[_tile_rows(qi), :] = (ot.T * mg_ref[_tile_rows(qi), :]).astype(_BF16)
        return 0

    lax.fori_loop(0, n_q // 2, lambda j, c: finalize(2 * j + 1, finalize(2 * j, c)), 0)


def _out_kernel(oa_ref, ob_ref, gates_ref, x_ref, p_ref, wbd_ref, wbm_ref, wout_ref, wpg_ref,
                wple_ref, fg_ref, o_ref, wbd_bf, wbm_bf, wout_bf, wpg_bf, wple_bf):
    @pl.when(pl.program_id(0) == 0)
    def _():
        for src, dst in ((wbd_ref, wbd_bf), (wbm_ref, wbm_bf), (wout_ref, wout_bf),
                         (wpg_ref, wpg_bf), (wple_ref, wple_bf)):
            dst[...] = src[...].astype(_BF16)

    half = ROW_TILE // 2
    for r in (slice(0, half), slice(half, ROW_TILE)):
        ya = jnp.dot(oa_ref[r], wbd_bf[...], preferred_element_type=_F32)
        yb = jnp.dot(ob_ref[r], wbm_bf[...], preferred_element_type=_F32)
        merged = gates_ref[r, :D_MODEL] * ya + gates_ref[r, D_MODEL:] * yb
        x1 = x_ref[r] + jnp.dot(merged.astype(_BF16), wout_bf[...], preferred_element_type=_F32)
        t = jnp.dot(x1.astype(_BF16), wpg_bf[...], preferred_element_type=_F32)
        pe = jnp.dot(p_ref[r].astype(_BF16), wple_bf[...], preferred_element_type=_F32)
        x2 = x1 + jax.nn.sigmoid(t) * pe
        ms = jnp.mean(x2 * x2, axis=-1, keepdims=True)
        o_ref[r] = x2 * lax.rsqrt(ms + EPS) * fg_ref[...]


def _rope_lane_tables(seq):
    half = ROT_DIM // 2
    inv = ROPE_THETA ** (-np.arange(0, ROT_DIM, 2, dtype=np.float64) / ROT_DIM)
    ang = np.arange(seq, dtype=np.float64)[:, None] * inv[None, :]
    cos, sin = np.cos(ang), np.sin(ang)
    pad = MOBA_HEAD_DIM - ROT_DIM
    one, zero = np.ones((seq, pad)), np.zeros((seq, pad))
    zh = np.zeros((seq, half))
    cos64 = np.concatenate([cos, cos, one], axis=1)
    slo64 = np.concatenate([-sin, zh, zero], axis=1)
    shi64 = np.concatenate([zh, sin, zero], axis=1)
    return tuple(jnp.asarray(np.tile(t, (1, LANES // MOBA_HEAD_DIM)), _F32)
                 for t in (cos64, slo64, shi64))


def _resident(shape):
    return pl.BlockSpec(shape, lambda *_: (0,) * len(shape), pipeline_mode=pl.Buffered(1))


def _params(n_axes):
    return pltpu.CompilerParams(dimension_semantics=("arbitrary",) * n_axes,
                                vmem_limit_bytes=VMEM_LIMIT)


def kernel(x, p, norm_g, w_in, lambda_q1, lambda_k1, lambda_q2, lambda_k2, subln_g,
           w_branch_diff, w_branch_moba, w_out, w_ple, w_ple_gate, final_g):
    B, S, _ = x.shape
    assert w_in.shape == (1, D_MODEL, N_CHUNKS * CHUNK) and S % (2 * ATT_TILE) == 0
    rows = B * S
    n_row_tiles = rows // ROW_TILE
    n_kv = S // ATT_TILE
    n_blk = S // MOBA_BLOCK
    x2d = x.reshape(rows, D_MODEL)
    cosv, sin_lo, sin_hi = _rope_lane_tables(S)

    row_spec = lambda w: pl.BlockSpec((ROW_TILE, w), lambda i: (i, 0))
    tab_spec = pl.BlockSpec((ROW_TILE, LANES), lambda i: (i % n_kv, 0))
    dvt_rows, mvt_rows = DIFF_HEADS * DIFF_V_ROWS, MOBA_HEADS * MOBA_V_ROWS
    vt_spec = lambda r: pl.BlockSpec((1, 1, r, ATT_TILE), lambda i: (i // n_kv, i % n_kv, 0, 0))
    vt_shape = lambda r: jax.ShapeDtypeStruct((B, n_kv, r, ATT_TILE), _BF16)
    rows_bf16 = jax.ShapeDtypeStruct((rows, CHUNK), _BF16)
    rows_f32 = jax.ShapeDtypeStruct((rows, CHUNK), _F32)
    km_rows = ROW_TILE // MOBA_BLOCK
    (dq, dk, dvt, dgs, mq, mk, mvt, mgs, kmean, gates) = pl.pallas_call(
        _proj_kernel,
        grid=(n_row_tiles,),
        in_specs=[row_spec(D_MODEL), _resident((1, D_MODEL)), _resident((D_MODEL, N_CHUNKS * CHUNK)),
                  tab_spec, tab_spec, tab_spec],
        out_specs=[row_spec(CHUNK), row_spec(CHUNK), vt_spec(dvt_rows), row_spec(CHUNK), row_spec(CHUNK),
                   row_spec(CHUNK), vt_spec(mvt_rows), row_spec(CHUNK),
                   pl.BlockSpec((1, km_rows, CHUNK), lambda i: (i, 0, 0)),
                   row_spec(4 * CHUNK)],
        out_shape=[rows_bf16, rows_bf16, vt_shape(dvt_rows), rows_f32, rows_f32, rows_bf16,
                   vt_shape(mvt_rows), rows_f32,
                   jax.ShapeDtypeStruct((n_row_tiles, km_rows, CHUNK), _F32),
                   jax.ShapeDtypeStruct((rows, 4 * CHUNK), _BF16)],
        compiler_params=_params(1),
        name="proj",
    )(x2d, norm_g[0].reshape(1, D_MODEL), w_in[0], cosv, sin_lo, sin_hi)
    kmean = kmean.reshape(B * n_blk, CHUNK)

    n_groups = CHUNK // LANES
    qtab, ttab = _past_items(n_kv)
    seq_spec = pl.BlockSpec((S, LANES), lambda b, g, *_: (b, g))
    v_spec = lambda r: pl.BlockSpec((1, n_kv, r, ATT_TILE), lambda b, g, *_: (b, 0, g, 0))
    lam_spec = pl.BlockSpec((1, DIFF_HEAD_DIM), lambda b, g, *_: (0, 0))
    acc = lambda r: pltpu.VMEM((n_kv, r, ATT_TILE), _F32)
    run_max = pltpu.VMEM((n_kv, 1, ATT_TILE), _F32)
    score_buf = pltpu.VMEM((2, ATT_TILE, ATT_TILE), _F32)
    col_max = lambda n: pltpu.VMEM((2, n, 1, ATT_TILE), _F32)

    o_a = pl.pallas_call(
        _diff_kernel,
        grid_spec=pltpu.PrefetchScalarGridSpec(
            num_scalar_prefetch=2,
            grid=(B, n_groups),
            in_specs=[lam_spec, lam_spec, lam_spec, lam_spec, seq_spec, seq_spec, v_spec(DIFF_V_ROWS),
                      seq_spec, pl.BlockSpec((2 * DIFF_HEAD_DIM, 1), lambda b, g, *_: (0, 0))],
            out_specs=seq_spec,
            scratch_shapes=[acc(DIFF_V_ROWS), acc(DIFF_V_ROWS), run_max, run_max, score_buf, score_buf,
                            col_max(2)],
        ),
        out_shape=rows_bf16,
        compiler_params=_params(2),
        name="diff_attn",
    )(qtab, ttab, lambda_q1[0:1], lambda_k1[0:1], lambda_q2[0:1], lambda_k2[0:1], dq, dk, dvt, dgs,
      subln_g[0].reshape(2 * DIFF_HEAD_DIM, 1))

    o_b = pl.pallas_call(
        _moba_kernel,
        grid_spec=pltpu.PrefetchScalarGridSpec(
            num_scalar_prefetch=2,
            grid=(B, n_groups),
            in_specs=[seq_spec, seq_spec, v_spec(2 * MOBA_V_ROWS),
                      pl.BlockSpec((n_blk, LANES), lambda b, g, *_: (b, g)), seq_spec],
            out_specs=seq_spec,
            scratch_shapes=[acc(MOBA_V_ROWS), acc(MOBA_V_ROWS), run_max, run_max, score_buf, score_buf,
                            col_max(2 * (ATT_TILE // MOBA_BLOCK)),
                            pltpu.VMEM((2, n_kv, n_blk, ATT_TILE), _F32),
                            pltpu.VMEM((2, S, LANES), _BF16)],
        ),
        out_shape=rows_bf16,
        compiler_params=_params(2),
        name="moba_attn",
    )(qtab, ttab, mq, mk, mvt, kmean, mgs)

    out = pl.pallas_call(
        _out_kernel,
        grid=(n_row_tiles,),
        in_specs=[row_spec(CHUNK), row_spec(CHUNK), row_spec(4 * CHUNK), row_spec(D_MODEL),
                  row_spec(PLE_DIM), _resident((DIFF_WIDTH, D_MODEL)), _resident((MOBA_WIDTH, D_MODEL)),
                  _resident((D_MODEL, D_MODEL)), _resident((D_MODEL, D_MODEL)),
                  _resident((PLE_DIM, D_MODEL)), _resident((1, D_MODEL))],
        out_specs=row_spec(D_MODEL),
        out_shape=jax.ShapeDtypeStruct((rows, D_MODEL), _F32),
        scratch_shapes=[pltpu.VMEM(s, _BF16) for s in ((DIFF_WIDTH, D_MODEL), (MOBA_WIDTH, D_MODEL),
                                                      (D_MODEL, D_MODEL), (D_MODEL, D_MODEL),
                                                      (PLE_DIM, D_MODEL))],
        compiler_params=_params(1),
        name="out_proj",
    )(o_a, o_b, gates, x2d, p[0].reshape(rows, PLE_DIM), w_branch_diff[0], w_branch_moba[0],
      w_out[0], w_ple_gate[0], w_ple[0], final_g.reshape(1, D_MODEL))
    return out.reshape(B, S, D_MODEL)
```

```python
import collections
import functools
import math

import jax
import jax.numpy as jnp
import numpy as np
from jax import lax
from jax.experimental import pallas as pl
from jax.experimental.pallas import tpu as pltpu

D_MODEL = 1024
PLE_DIM = 256
DIFF_HEADS = 4
DIFF_HEAD_DIM = 64
DIFF_WIDTH = DIFF_HEADS * 2 * DIFF_HEAD_DIM
MOBA_HEADS = 8
MOBA_HEAD_DIM = 64
MOBA_WIDTH = MOBA_HEADS * MOBA_HEAD_DIM
MOBA_BLOCK = 256
MOBA_TOPK = 3
ROT_DIM = 16
ROPE_THETA = 500000.0
EPS = 1e-6
SUBLN_EPS = 1e-5
LAM_INIT = 0.8 - 0.6 * math.exp(-0.3 * 0)

LANES = 128
MXU_COLS = 256
BF16_ROWS = 16
CHUNK = 512
N_CHUNKS = 12
ROW_TILE = 512
ATT_TILE = 512
PIPE_UNROLL = 24
VMEM_LIMIT = 56 * 1024 * 1024
_SCORE_SCALE = DIFF_HEAD_DIM ** -0.5 * math.log2(math.e)
DIFF_V_ROWS = 2 * DIFF_HEAD_DIM + BF16_ROWS
MOBA_V_ROWS = MOBA_HEAD_DIM + BF16_ROWS
assert DIFF_HEAD_DIM == MOBA_HEAD_DIM and ATT_TILE % MOBA_BLOCK == 0 and ROW_TILE == ATT_TILE

_F32 = jnp.float32
_BF16 = jnp.bfloat16
_NT = (((1,), (1,)), ((), ()))


def _rope_rows(a, cosv, sin_lo, sin_hi):
    outs = []
    for j in range(CHUNK // LANES):
        xs = a[:, LANES * j:LANES * (j + 1)]
        up = pltpu.roll(xs, LANES - ROT_DIM // 2, 1)
        dn = pltpu.roll(xs, ROT_DIM // 2, 1)
        outs.append(xs * cosv + up * sin_lo + dn * sin_hi)
    return jnp.concatenate(outs, axis=1)


def _proj_kernel(x_ref, g_ref, w_ref, cos_ref, slo_ref, shi_ref,
                 dq_ref, dk_ref, dvt_ref, dg_ref, mq_ref, mk_ref, mvt_ref, mg_ref,
                 km_ref, gates_ref):
    for j in range(ROW_TILE // MOBA_BLOCK):
        r = slice(MOBA_BLOCK * j, MOBA_BLOCK * (j + 1))
        x = x_ref[r]
        ms = jnp.mean(x * x, axis=-1, keepdims=True)
        h = (x * lax.rsqrt(ms + EPS) * g_ref[...]).astype(_BF16)
        rope = functools.partial(_rope_rows, cosv=cos_ref[r], sin_lo=slo_ref[r], sin_hi=shi_ref[r])

        def proj(c):
            return jnp.dot(h, w_ref[:, CHUNK * c:CHUNK * (c + 1)].astype(_BF16),
                           preferred_element_type=_F32)

        def store_transposed(ref, a, head_rows):
            at = a.T.astype(_BF16)
            ones = jnp.ones((BF16_ROWS, MOBA_BLOCK), _BF16)
            stride = head_rows + BF16_ROWS
            for g in range(CHUNK // head_rows):
                ref[0, 0, stride * g:stride * g + head_rows, r] = at[head_rows * g:head_rows * (g + 1)]
                ref[0, 0, stride * g + head_rows:stride * (g + 1), r] = ones

        dq_ref[r] = (rope(proj(0)) * _SCORE_SCALE).astype(_BF16)
        dk_ref[r] = rope(proj(1)).astype(_BF16)
        store_transposed(dvt_ref, proj(2), 2 * DIFF_HEAD_DIM)
        dg_ref[r] = jax.nn.silu(proj(3))
        mq_ref[r] = rope(proj(4))
        mk = rope(proj(5))
        mk_ref[r] = mk.astype(_BF16)
        km_ref[0, j:j + 1, :] = jnp.mean(mk, axis=0, keepdims=True)
        store_transposed(mvt_ref, proj(6), MOBA_HEAD_DIM)
        mg_ref[r] = jax.nn.silu(proj(7))
        for c in range(8, N_CHUNKS):
            gates_ref[r, CHUNK * (c - 8):CHUNK * (c - 7)] = jax.nn.sigmoid(proj(c))


def _split_heads(q):
    lane = lax.broadcasted_iota(jnp.int32, q.shape, 1)
    zero = jnp.zeros_like(q)
    return jnp.where(lane < 64, q, zero), jnp.where(lane >= 64, q, zero)


def _tile_rows(i):
    return pl.ds(pl.multiple_of(i * ATT_TILE, ATT_TILE), ATT_TILE)


def _past_items(n_q):
    items = [(qi, t) for qi in range(1, n_q) for t in range(qi)]
    assert len(items) % PIPE_UNROLL == 0 and PIPE_UNROLL % 2 == 0
    items.append(items[-1])
    return (np.array([i[0] for i in items], np.int32), np.array([i[1] for i in items], np.int32))


_Stream = collections.namedtuple("_Stream", "s_ref acc_ref m_ref")


def _all_parts(ref, qi):
    return jnp.concatenate([ref[qi, pi] for pi in range(ref.shape[1])], axis=1)


def _run_attention(qtab_ref, ttab_ref, streams, n_seg, k_ref, cm_ref, q_of, vt_of, bias_of,
                   interleave):
    n_q = streams[0].acc_ref.shape[0]
    seg_rows = ATT_TILE // n_seg
    parts, width = streams[0].s_ref.shape[1], streams[0].s_ref.shape[3]
    assert interleave in ("part", "stream") and seg_rows % MXU_COLS == 0
    assert width == (MXU_COLS if interleave == "part" else ATT_TILE) and parts * width == ATT_TILE

    def place(c0, cols):
        assert c0 % width + cols <= width
        return c0 // width, slice(c0 % width, c0 % width + cols)

    def softmax(st, qi, tile, slot, first):
        stream = streams[st]
        biases = [bias_of(st, qi, tile, j) for j in range(n_seg)]
        cms = [jnp.concatenate([cm_ref[slot, st * n_seg + j, pi] for pi in range(parts)], axis=1)
               for j in range(n_seg)]
        cands = [cm if b is None else cm + b for cm, b in zip(cms, biases)]
        m_old = None if first else stream.m_ref[qi]
        m_new = functools.reduce(jnp.maximum, cands if first else [m_old] + cands)
        stream.m_ref[qi] = m_new
        shifts = [m_new if b is None else m_new - b for b in biases]
        p = []
        for pi in range(parts):
            lanes = slice(width * pi, width * (pi + 1))
            segs = [jnp.exp2(stream.s_ref[slot, pi, seg_rows * j:seg_rows * (j + 1), :]
                             - sh[:, lanes]).astype(_BF16) for j, sh in enumerate(shifts)]
            p.append(segs[0] if n_seg == 1 else jnp.concatenate(segs, axis=0))
        return p, (None if first else jnp.exp2(m_old - m_new))

    def pieces(diagonal):
        if diagonal:
            return [(c, MXU_COLS, c + MXU_COLS) for c in range(0, ATT_TILE, MXU_COLS)]
        return [(c, width, ATT_TILE) for c in range(0, ATT_TILE, width)]

    def qk_piece(st, k, qi, slot, diagonal, c0, cols, keys):
        pi, lanes = place(c0, cols)
        rows = pl.ds(pl.multiple_of(qi * ATT_TILE + c0, cols), cols)
        s = lax.dot_general(k[:keys], q_of(st, rows), _NT, preferred_element_type=_F32)
        if diagonal:
            kpos = lax.broadcasted_iota(jnp.int32, s.shape, 0)
            qpos = lax.broadcasted_iota(jnp.int32, s.shape, 1) + c0
            s = jnp.where(kpos <= qpos, s, -jnp.inf)
        s_ref = streams[st].s_ref
        s_ref[slot, pi, :keys, lanes] = s
        if keys < ATT_TILE:
            s_ref[slot, pi, keys:, lanes] = jnp.full((ATT_TILE - keys, cols), -jnp.inf, _F32)
        for j in range(n_seg):
            seg = s[seg_rows * j:seg_rows * (j + 1)]
            cm_ref[slot, st * n_seg + j, pi, :, lanes] = (
                jnp.max(seg, axis=0, keepdims=True) if seg.shape[0]
                else jnp.full((1, cols), -jnp.inf, _F32))

    def pv_piece(st, vt, p, alpha, qi, c0, cols, keys):
        pi, lanes = place(c0, cols)
        acc_ref = streams[st].acc_ref
        pv = jnp.dot(vt[:, :keys], p[pi][:keys, lanes], preferred_element_type=_F32)
        acc_ref[qi, pi, :, lanes] = (
            pv if alpha is None else alpha[:, c0:c0 + cols] * acc_ref[qi, pi, :, lanes] + pv)

    def step(nxt, cur, first=False):
        k = None if nxt is None else k_ref[_tile_rows(nxt[1]), :]
        for st in range(len(streams)):
            qk_ops, pv_ops = [], []
            if nxt is not None:
                qk_ops = [functools.partial(qk_piece, st, k, nxt[0], nxt[2], nxt[3], *pc)
                          for pc in pieces(nxt[3])]
            if cur is not None:
                p, alpha = softmax(st, cur[0], cur[1], cur[2], first)
                vt = vt_of(st, cur[1])
                pv_ops = [functools.partial(pv_piece, st, vt, p, alpha, cur[0], *pc)
                          for pc in pieces(first)]
            if interleave == "part":
                ops = [op for pair in zip(qk_ops, pv_ops) for op in pair]
                ops += qk_ops[len(pv_ops):] + pv_ops[len(qk_ops):]
            else:
                ops = qk_ops + pv_ops
            for op in ops:
                op()

    step((0, 0, 0, True), None)

    def diag_items(j, _):
        for u in range(2):
            t = 2 * j + u
            step((t + 1, t + 1, (u + 1) % 2, True), (t, t, u), first=True)
        return 0

    lax.fori_loop(0, n_q // 2 - 1, diag_items, 0)
    step((n_q - 1, n_q - 1, 1, True), (n_q - 2, n_q - 2, 0), first=True)
    step((qtab_ref[0], ttab_ref[0], 0, False), (n_q - 1, n_q - 1, 1), first=True)

    def past_items(j, _):
        for u in range(PIPE_UNROLL):
            t = PIPE_UNROLL * j + u
            step((qtab_ref[t + 1], ttab_ref[t + 1], (u + 1) % 2, False),
                 (qtab_ref[t], ttab_ref[t], u % 2))
        return 0

    lax.fori_loop(0, n_q * (n_q - 1) // (2 * PIPE_UNROLL), past_items, 0)


def _diff_kernel(qtab_ref, ttab_ref, lq1_ref, lk1_ref, lq2_ref, lk2_ref, q_ref, k_ref, vt_ref,
                 dg_ref, sg_ref, o_ref, acc1_ref, acc2_ref, m1_ref, m2_ref, s1_ref, s2_ref, cm_ref):
    n_q = acc1_ref.shape[0]
    streams = (_Stream(s1_ref, acc1_ref, m1_ref), _Stream(s2_ref, acc2_ref, m2_ref))
    _run_attention(
        qtab_ref, ttab_ref, streams, 1, k_ref, cm_ref,
        q_of=lambda st, rows: _split_heads(q_ref[rows, :])[st],
        vt_of=lambda st, tile: vt_ref[0, tile],
        bias_of=lambda st, qi, tile, j: None,
        interleave="stream")

    dv = 2 * DIFF_HEAD_DIM
    lam = (jnp.exp(jnp.sum(lq1_ref[...] * lk1_ref[...], axis=-1, keepdims=True))
           - jnp.exp(jnp.sum(lq2_ref[...] * lk2_ref[...], axis=-1, keepdims=True)) + LAM_INIT)

    def finalize(qi, _):
        a1, a2 = _all_parts(acc1_ref, qi), _all_parts(acc2_ref, qi)
        ot = a1[:dv] / a1[dv:dv + 1] - lam * (a2[:dv] / a2[dv:dv + 1])
        ms = jnp.mean(ot * ot, axis=0, keepdims=True)
        ot = ot * lax.rsqrt(ms + SUBLN_EPS) * sg_ref[...] * (1.0 - LAM_INIT)
        o_ref[_tile_rows(qi), :] = (ot.T * dg_ref[_tile_rows(qi), :]).astype(_BF16)
        return 0

    lax.fori_loop(0, n_q // 2, lambda j, c: finalize(2 * j + 1, finalize(2 * j, c)), 0)


def _moba_kernel(qtab_ref, ttab_ref, q_ref, k_ref, vt_ref, km_ref, mg_ref, o_ref,
                 acca_ref, accb_ref, ma_ref, mb_ref, sa_ref, sb_ref, cm_ref, bias_ref, qs_ref):
    n_q = acca_ref.shape[0]
    nb = km_ref.shape[0]
    per_tile = ATT_TILE // MOBA_BLOCK
    km_both = jnp.concatenate(_split_heads(km_ref[...]), axis=0)
    km_hi = km_both.astype(_BF16)
    km_hl = jnp.concatenate([km_hi, (km_both - km_hi.astype(_F32)).astype(_BF16)], axis=0)
    blk = lax.broadcasted_iota(jnp.int32, (nb, ATT_TILE), 0)
    col = lax.broadcasted_iota(jnp.int32, (nb, ATT_TILE), 1)
    col_blk = lax.shift_right_logical(col, MOBA_BLOCK.bit_length() - 1)

    def gate(qi, _):
        q = q_ref[_tile_rows(qi), :]
        own = per_tile * qi + col_blk
        q_hi = q.astype(_BF16)
        q_lo = (q - q_hi.astype(_F32)).astype(_BF16)
        g_hl = lax.dot_general(km_hl, q_hi, _NT, preferred_element_type=_F32)
        g_both = (g_hl[:2 * nb] + g_hl[2 * nb:]
                  + lax.dot_general(km_hi, q_lo, _NT, preferred_element_type=_F32))
        for hh in range(2):
            g = jnp.where(blk < own, g_both[nb * hh:nb * (hh + 1)], -jnp.inf)
            sel = jnp.zeros(g.shape, jnp.bool_)
            for _ in range(MOBA_TOPK):
                mx = jnp.max(g, axis=0, keepdims=True)
                first_idx = jnp.min(jnp.where(g == mx, blk, nb), axis=0, keepdims=True)
                pick = (blk == first_idx) & (mx > -jnp.inf)
                sel = sel | pick
                g = jnp.where(pick, -jnp.inf, g)
            bias_ref[hh, qi] = jnp.where(sel | (blk == own), 0.0, -jnp.inf)
        for hh, qh in enumerate(_split_heads((q * _SCORE_SCALE).astype(_BF16))):
            qs_ref[hh, _tile_rows(qi), :] = qh
        return 0

    lax.fori_loop(0, n_q // 4,
                  lambda j, c: gate(4 * j + 3, gate(4 * j + 2, gate(4 * j + 1, gate(4 * j, c)))), 0)

    streams = (_Stream(sa_ref, acca_ref, ma_ref), _Stream(sb_ref, accb_ref, mb_ref))
    _run_attention(
        qtab_ref, ttab_ref, streams, per_tile, k_ref, cm_ref,
        q_of=lambda st, rows: qs_ref[st, rows, :],
        vt_of=lambda st, tile: vt_ref[0, tile, MOBA_V_ROWS * st:MOBA_V_ROWS * (st + 1), :],
        bias_of=lambda st, qi, tile, j: bias_ref[st, qi, pl.ds(per_tile * tile + j, 1), :],
        interleave="part")

    dh = MOBA_HEAD_DIM

    def finalize(qi, _):
        aa, ab = _all_parts(acca_ref, qi), _all_parts(accb_ref, qi)
        ot = jnp.concatenate([aa[:dh] / aa[dh:dh + 1], ab[:dh] / ab[dh:dh + 1]], axis=0)
        o_ref[_tile_rows(qi), :] = (ot.T * mg_ref[_tile_rows(qi), :]).astype(_BF16)
        return 0

    lax.fori_loop(0, n_q // 2, lambda j, c: finalize(2 * j + 1, finalize(2 * j, c)), 0)


def _out_kernel(oa_ref, ob_ref, gates_ref, x_ref, p_ref, wbd_ref, wbm_ref, wout_ref, wpg_ref,
                wple_ref, fg_ref, o_ref, wbd_bf, wbm_bf, wout_bf, wpg_bf, wple_bf):
    @pl.when(pl.program_id(0) == 0)
    def _():
        for src, dst in ((wbd_ref, wbd_bf), (wbm_ref, wbm_bf), (wout_ref, wout_bf),
                         (wpg_ref, wpg_bf), (wple_ref, wple_bf)):
            dst[...] = src[...].astype(_BF16)

    half = ROW_TILE // 2
    for r in (slice(0, half), slice(half, ROW_TILE)):
        ya = jnp.dot(oa_ref[r], wbd_bf[...], preferred_element_type=_F32)
        yb = jnp.dot(ob_ref[r], wbm_bf[...], preferred_element_type=_F32)
        merged = gates_ref[r, :D_MODEL] * ya + gates_ref[r, D_MODEL:] * yb
        x1 = x_ref[r] + jnp.dot(merged.astype(_BF16), wout_bf[...], preferred_element_type=_F32)
        t = jnp.dot(x1.astype(_BF16), wpg_bf[...], preferred_element_type=_F32)
        pe = jnp.dot(p_ref[r].astype(_BF16), wple_bf[...], preferred_element_type=_F32)
        x2 = x1 + jax.nn.sigmoid(t) * pe
        ms = jnp.mean(x2 * x2, axis=-1, keepdims=True)
        o_ref[r] = x2 * lax.rsqrt(ms + EPS) * fg_ref[...]


def _rope_lane_tables(seq):
    half = ROT_DIM // 2
    inv = ROPE_THETA ** (-np.arange(0, ROT_DIM, 2, dtype=np.float64) / ROT_DIM)
    ang = np.arange(seq, dtype=np.float64)[:, None] * inv[None, :]
    cos, sin = np.cos(ang), np.sin(ang)
    pad = MOBA_HEAD_DIM - ROT_DIM
    one, zero = np.ones((seq, pad)), np.zeros((seq, pad))
    zh = np.zeros((seq, half))
    cos64 = np.concatenate([cos, cos, one], axis=1)
    slo64 = np.concatenate([-sin, zh, zero], axis=1)
    shi64 = np.concatenate([zh, sin, zero], axis=1)
    return tuple(jnp.asarray(np.tile(t, (1, LANES // MOBA_HEAD_DIM)), _F32)
                 for t in (cos64, slo64, shi64))


def _resident(shape):
    return pl.BlockSpec(shape, lambda *_: (0,) * len(shape), pipeline_mode=pl.Buffered(1))


def _params(n_axes):
    return pltpu.CompilerParams(dimension_semantics=("arbitrary",) * n_axes,
                                vmem_limit_bytes=VMEM_LIMIT)


def kernel(x, p, norm_g, w_in, lambda_q1, lambda_k1, lambda_q2, lambda_k2, subln_g,
           w_branch_diff, w_branch_moba, w_out, w_ple, w_ple_gate, final_g):
    B, S, _ = x.shape
    assert w_in.shape == (1, D_MODEL, N_CHUNKS * CHUNK) and S % (2 * ATT_TILE) == 0
    rows = B * S
    n_row_tiles = rows // ROW_TILE
    n_kv = S // ATT_TILE
    n_blk = S // MOBA_BLOCK
    x2d = x.reshape(rows, D_MODEL)
    cosv, sin_lo, sin_hi = _rope_lane_tables(S)

    row_spec = lambda w: pl.BlockSpec((ROW_TILE, w), lambda i: (i, 0))
    tab_spec = pl.BlockSpec((ROW_TILE, LANES), lambda i: (i % n_kv, 0))
    dvt_rows, mvt_rows = DIFF_HEADS * DIFF_V_ROWS, MOBA_HEADS * MOBA_V_ROWS
    vt_spec = lambda r: pl.BlockSpec((1, 1, r, ATT_TILE), lambda i: (i // n_kv, i % n_kv, 0, 0))
    vt_shape = lambda r: jax.ShapeDtypeStruct((B, n_kv, r, ATT_TILE), _BF16)
    rows_bf16 = jax.ShapeDtypeStruct((rows, CHUNK), _BF16)
    rows_f32 = jax.ShapeDtypeStruct((rows, CHUNK), _F32)
    km_rows = ROW_TILE // MOBA_BLOCK
    (dq, dk, dvt, dgs, mq, mk, mvt, mgs, kmean, gates) = pl.pallas_call(
        _proj_kernel,
        grid=(n_row_tiles,),
        in_specs=[row_spec(D_MODEL), _resident((1, D_MODEL)), _resident((D_MODEL, N_CHUNKS * CHUNK)),
                  tab_spec, tab_spec, tab_spec],
        out_specs=[row_spec(CHUNK), row_spec(CHUNK), vt_spec(dvt_rows), row_spec(CHUNK), row_spec(CHUNK),
                   row_spec(CHUNK), vt_spec(mvt_rows), row_spec(CHUNK),
                   pl.BlockSpec((1, km_rows, CHUNK), lambda i: (i, 0, 0)),
                   row_spec(4 * CHUNK)],
        out_shape=[rows_bf16, rows_bf16, vt_shape(dvt_rows), rows_f32, rows_f32, rows_bf16,
                   vt_shape(mvt_rows), rows_f32,
                   jax.ShapeDtypeStruct((n_row_tiles, km_rows, CHUNK), _F32),
                   jax.ShapeDtypeStruct((rows, 4 * CHUNK), _F32)],
        compiler_params=_params(1),
        name="proj",
    )(x2d, norm_g[0].reshape(1, D_MODEL), w_in[0], cosv, sin_lo, sin_hi)
    kmean = kmean.reshape(B * n_blk, CHUNK)

    n_groups = CHUNK // LANES
    qtab, ttab = _past_items(n_kv)
    seq_spec = pl.BlockSpec((S, LANES), lambda b, g, *_: (b, g))
    v_spec = lambda r: pl.BlockSpec((1, n_kv, r, ATT_TILE), lambda b, g, *_: (b, 0, g, 0))
    lam_spec = pl.BlockSpec((1, DIFF_HEAD_DIM), lambda b, g, *_: (0, 0))
    acc = lambda r, w: pltpu.VMEM((n_kv, ATT_TILE // w, r, w), _F32)
    run_max = pltpu.VMEM((n_kv, 1, ATT_TILE), _F32)
    score_buf = lambda w: pltpu.VMEM((2, ATT_TILE // w, ATT_TILE, w), _F32)
    col_max = lambda n, w: pltpu.VMEM((2, n, ATT_TILE // w, 1, w), _F32)

    o_a = pl.pallas_call(
        _diff_kernel,
        grid_spec=pltpu.PrefetchScalarGridSpec(
            num_scalar_prefetch=2,
            grid=(B, n_groups),
            in_specs=[lam_spec, lam_spec, lam_spec, lam_spec, seq_spec, seq_spec, v_spec(DIFF_V_ROWS),
                      seq_spec, pl.BlockSpec((2 * DIFF_HEAD_DIM, 1), lambda b, g, *_: (0, 0))],
            out_specs=seq_spec,
            scratch_shapes=[acc(DIFF_V_ROWS, ATT_TILE), acc(DIFF_V_ROWS, ATT_TILE), run_max, run_max,
                            score_buf(ATT_TILE), score_buf(ATT_TILE), col_max(2, ATT_TILE)],
        ),
        out_shape=rows_bf16,
        compiler_params=_params(2),
        name="diff_attn",
    )(qtab, ttab, lambda_q1[0:1], lambda_k1[0:1], lambda_q2[0:1], lambda_k2[0:1], dq, dk, dvt, dgs,
      subln_g[0].reshape(2 * DIFF_HEAD_DIM, 1))

    o_b = pl.pallas_call(
        _moba_kernel,
        grid_spec=pltpu.PrefetchScalarGridSpec(
            num_scalar_prefetch=2,
            grid=(B, n_groups),
            in_specs=[seq_spec, seq_spec, v_spec(2 * MOBA_V_ROWS),
                      pl.BlockSpec((n_blk, LANES), lambda b, g, *_: (b, g)), seq_spec],
            out_specs=seq_spec,
            scratch_shapes=[acc(MOBA_V_ROWS, MXU_COLS), acc(MOBA_V_ROWS, MXU_COLS), run_max, run_max,
                            score_buf(MXU_COLS), score_buf(MXU_COLS),
                            col_max(2 * (ATT_TILE // MOBA_BLOCK), MXU_COLS),
                            pltpu.VMEM((2, n_kv, n_blk, ATT_TILE), _F32),
                            pltpu.VMEM((2, S, LANES), _BF16)],
        ),
        out_shape=rows_bf16,
        compiler_params=_params(2),
        name="moba_attn",
    )(qtab, ttab, mq, mk, mvt, kmean, mgs)

    out = pl.pallas_call(
        _out_kernel,
        grid=(n_row_tiles,),
        in_specs=[row_spec(CHUNK), row_spec(CHUNK), row_spec(4 * CHUNK), row_spec(D_MODEL),
                  row_spec(PLE_DIM), _resident((DIFF_WIDTH, D_MODEL)), _resident((MOBA_WIDTH, D_MODEL)),
                  _resident((D_MODEL, D_MODEL)), _resident((D_MODEL, D_MODEL)),
                  _resident((PLE_DIM, D_MODEL)), _resident((1, D_MODEL))],
        out_specs=row_spec(D_MODEL),
        out_shape=jax.ShapeDtypeStruct((rows, D_MODEL), _F32),
        scratch_shapes=[pltpu.VMEM(s, _BF16) for s in ((DIFF_WIDTH, D_MODEL), (MOBA_WIDTH, D_MODEL),
                                                      (D_MODEL, D_MODEL), (D_MODEL, D_MODEL),
                                                      (PLE_DIM, D_MODEL))],
        compiler_params=_params(1),
        name="out_proj",
    )(o_a, o_b, gates, x2d, p[0].reshape(rows, PLE_DIM), w_branch_diff[0], w_branch_moba[0],
      w_out[0], w_ple_gate[0], w_ple[0], final_g.reshape(1, D_MODEL))
    return out.reshape(B, S, D_MODEL)
```

```python
import collections
import functools
import math

import jax
import jax.numpy as jnp
import numpy as np
from jax import lax
from jax.experimental import pallas as pl
from jax.experimental.pallas import tpu as pltpu

D_MODEL = 1024
PLE_DIM = 256
DIFF_HEADS = 4
DIFF_HEAD_DIM = 64
DIFF_WIDTH = DIFF_HEADS * 2 * DIFF_HEAD_DIM
MOBA_HEADS = 8
MOBA_HEAD_DIM = 64
MOBA_WIDTH = MOBA_HEADS * MOBA_HEAD_DIM
MOBA_BLOCK = 256
MOBA_TOPK = 3
ROT_DIM = 16
ROPE_THETA = 500000.0
EPS = 1e-6
SUBLN_EPS = 1e-5
LAM_INIT = 0.8 - 0.6 * math.exp(-0.3 * 0)

LANES = 128
MXU_COLS = 256
BF16_ROWS = 16
CHUNK = 512
N_CHUNKS = 12
ROW_TILE = 512
ATT_TILE = 512
PIPE_UNROLL = 24
VMEM_LIMIT = 56 * 1024 * 1024
_SCORE_SCALE = DIFF_HEAD_DIM ** -0.5 * math.log2(math.e)
DIFF_V_ROWS = 2 * DIFF_HEAD_DIM + BF16_ROWS
MOBA_V_ROWS = MOBA_HEAD_DIM + BF16_ROWS
assert DIFF_HEAD_DIM == MOBA_HEAD_DIM and ATT_TILE % MOBA_BLOCK == 0 and ROW_TILE == ATT_TILE

_F32 = jnp.float32
_BF16 = jnp.bfloat16
_NT = (((1,), (1,)), ((), ()))


def _rope_rows(a, cosv, sin_lo, sin_hi):
    outs = []
    for j in range(CHUNK // LANES):
        xs = a[:, LANES * j:LANES * (j + 1)]
        up = pltpu.roll(xs, LANES - ROT_DIM // 2, 1)
        dn = pltpu.roll(xs, ROT_DIM // 2, 1)
        outs.append(xs * cosv + up * sin_lo + dn * sin_hi)
    return jnp.concatenate(outs, axis=1)


def _split_heads(q):
    lane = lax.broadcasted_iota(jnp.int32, q.shape, 1)
    zero = jnp.zeros_like(q)
    return jnp.where(lane < 64, q, zero), jnp.where(lane >= 64, q, zero)


def _moba_routing(bias_ref, cols, pair, q, km_all, own):
    nb = km_all.shape[0]
    blk = lax.broadcasted_iota(jnp.int32, (nb, q.shape[0]), 0)
    km_both = jnp.concatenate(_split_heads(km_all), axis=0)
    km_hi = km_both.astype(_BF16)
    km_hl = jnp.concatenate([km_hi, (km_both - km_hi.astype(_F32)).astype(_BF16)], axis=0)
    q_hi = q.astype(_BF16)
    q_lo = (q - q_hi.astype(_F32)).astype(_BF16)
    g_hl = lax.dot_general(km_hl, q_hi, _NT, preferred_element_type=_F32)
    g_both = (g_hl[:2 * nb] + g_hl[2 * nb:]
              + lax.dot_general(km_hi, q_lo, _NT, preferred_element_type=_F32))
    for hh in range(2):
        g = jnp.where(blk < own, g_both[nb * hh:nb * (hh + 1)], -jnp.inf)
        sel = jnp.zeros(g.shape, jnp.bool_)
        for _ in range(MOBA_TOPK):
            mx = jnp.max(g, axis=0, keepdims=True)
            first_idx = jnp.min(jnp.where(g == mx, blk, nb), axis=0, keepdims=True)
            pick = (blk == first_idx) & (mx > -jnp.inf)
            sel = sel | pick
            g = jnp.where(pick, -jnp.inf, g)
        bias_ref[0, 2 * pair + hh, 0, :, cols] = jnp.where(sel | (blk == own), 0.0, -jnp.inf)


def _proj_kernel(x_ref, g_ref, w_ref, cos_ref, slo_ref, shi_ref,
                 dq_ref, dk_ref, dvt_ref, dg_ref, mq_ref, mk_ref, mvt_ref, mg_ref,
                 bias_ref, gates_ref, km_ref):
    per_tile = ROW_TILE // MOBA_BLOCK
    n_kv = km_ref.shape[0] // per_tile

    @pl.when(pl.program_id(0) == 0)
    def _():
        km_ref[...] = jnp.zeros(km_ref.shape, _F32)

    for j in range(per_tile):
        r = slice(MOBA_BLOCK * j, MOBA_BLOCK * (j + 1))
        x = x_ref[r]
        ms = jnp.mean(x * x, axis=-1, keepdims=True)
        h = (x * lax.rsqrt(ms + EPS) * g_ref[...]).astype(_BF16)
        rope = functools.partial(_rope_rows, cosv=cos_ref[r], sin_lo=slo_ref[r], sin_hi=shi_ref[r])

        def proj(c):
            return jnp.dot(h, w_ref[:, CHUNK * c:CHUNK * (c + 1)].astype(_BF16),
                           preferred_element_type=_F32)

        def store_transposed(ref, a, head_rows):
            at = a.T.astype(_BF16)
            ones = jnp.ones((BF16_ROWS, MOBA_BLOCK), _BF16)
            stride = head_rows + BF16_ROWS
            for g in range(CHUNK // head_rows):
                ref[0, 0, stride * g:stride * g + head_rows, r] = at[head_rows * g:head_rows * (g + 1)]
                ref[0, 0, stride * g + head_rows:stride * (g + 1), r] = ones

        dq_ref[r] = (rope(proj(0)) * _SCORE_SCALE).astype(_BF16)
        dk_ref[r] = rope(proj(1)).astype(_BF16)
        store_transposed(dvt_ref, proj(2), 2 * DIFF_HEAD_DIM)
        dg_ref[r] = jax.nn.silu(proj(3))
        mq = rope(proj(4))
        mq_ref[r] = (mq * _SCORE_SCALE).astype(_BF16)
        own = per_tile * (pl.program_id(0) % n_kv) + j
        mk = rope(proj(5))
        mk_ref[r] = mk.astype(_BF16)
        km_ref[pl.ds(own, 1), :] = jnp.mean(mk, axis=0, keepdims=True)

        def route(pair):
            lanes = slice(LANES * pair, LANES * (pair + 1))
            _moba_routing(bias_ref, r, pair, mq[:, lanes], km_ref[:, lanes], own)

        store_transposed(mvt_ref, proj(6), MOBA_HEAD_DIM)
        mg_ref[r] = jax.nn.silu(proj(7))
        for pair in range(CHUNK // LANES):
            route(pair)
        for c in range(8, N_CHUNKS):
            gates_ref[r, CHUNK * (c - 8):CHUNK * (c - 7)] = jax.nn.sigmoid(proj(c))


def _tile_rows(i):
    return pl.ds(pl.multiple_of(i * ATT_TILE, ATT_TILE), ATT_TILE)


def _past_items(n_q):
    items = [(qi, t) for qi in range(1, n_q) for t in range(qi)]
    assert len(items) % PIPE_UNROLL == 0 and PIPE_UNROLL % 2 == 0
    items.append(items[-1])
    return (np.array([i[0] for i in items], np.int32), np.array([i[1] for i in items], np.int32))


_Stream = collections.namedtuple("_Stream", "s_ref acc_ref m_ref")


def _run_attention(qtab_ref, ttab_ref, streams, n_seg, k_ref, cm_ref, q_of, vt_of, bias_of,
                   interleave):
    n_q = streams[0].acc_ref.shape[0]
    seg_rows = ATT_TILE // n_seg
    assert interleave in ("part", "stream") and seg_rows % MXU_COLS == 0

    def softmax(st, qi, tile, slot, first):
        stream = streams[st]
        biases = [bias_of(st, qi, tile, j) for j in range(n_seg)]
        cands = [cm_ref[slot, st * n_seg + j] if b is None else cm_ref[slot, st * n_seg + j] + b
                 for j, b in enumerate(biases)]
        m_old = None if first else stream.m_ref[qi]
        m_new = functools.reduce(jnp.maximum, cands if first else [m_old] + cands)
        stream.m_ref[qi] = m_new
        p = [jnp.exp2(stream.s_ref[slot, seg_rows * j:seg_rows * (j + 1), :]
                      - (m_new if b is None else m_new - b)).astype(_BF16)
             for j, b in enumerate(biases)]
        p = p[0] if n_seg == 1 else jnp.concatenate(p, axis=0)
        return p, (None if first else jnp.exp2(m_old - m_new))

    def pieces(diagonal):
        if diagonal:
            return [(c, MXU_COLS, c + MXU_COLS) for c in range(0, ATT_TILE, MXU_COLS)]
        width = MXU_COLS if interleave == "part" else ATT_TILE
        return [(c, width, ATT_TILE) for c in range(0, ATT_TILE, width)]

    def qk_piece(st, k, qi, slot, diagonal, c0, width, keys):
        cols = slice(c0, c0 + width)
        rows = pl.ds(pl.multiple_of(qi * ATT_TILE + c0, width), width)
        s = lax.dot_general(k[:keys], q_of(st, rows), _NT, preferred_element_type=_F32)
        if diagonal:
            kpos = lax.broadcasted_iota(jnp.int32, s.shape, 0)
            qpos = lax.broadcasted_iota(jnp.int32, s.shape, 1) + c0
            s = jnp.where(kpos <= qpos, s, -jnp.inf)
        s_ref = streams[st].s_ref
        s_ref[slot, :keys, cols] = s
        if keys < ATT_TILE:
            s_ref[slot, keys:, cols] = jnp.full((ATT_TILE - keys, width), -jnp.inf, _F32)
        for j in range(n_seg):
            seg = s[seg_rows * j:seg_rows * (j + 1)]
            cm_ref[slot, st * n_seg + j, :, cols] = (
                jnp.max(seg, axis=0, keepdims=True) if seg.shape[0]
                else jnp.full((1, width), -jnp.inf, _F32))

    def pv_piece(st, vt, p, alpha, qi, c0, width, keys):
        cols = slice(c0, c0 + width)
        acc_ref = streams[st].acc_ref
        pv = jnp.dot(vt[:, :keys], p[:keys, cols], preferred_element_type=_F32)
        acc_ref[qi, :, cols] = pv if alpha is None else alpha[:, cols] * acc_ref[qi, :, cols] + pv

    def step(nxt, cur, first=False):
        k = None if nxt is None else k_ref[_tile_rows(nxt[1]), :]
        for st in range(len(streams)):
            qk_ops, pv_ops = [], []
            if nxt is not None:
                qk_ops = [functools.partial(qk_piece, st, k, nxt[0], nxt[2], nxt[3], *pc)
                          for pc in pieces(nxt[3])]
            if cur is not None:
                p, alpha = softmax(st, cur[0], cur[1], cur[2], first)
                vt = vt_of(st, cur[1])
                pv_ops = [functools.partial(pv_piece, st, vt, p, alpha, cur[0], *pc)
                          for pc in pieces(first)]
            if interleave == "part":
                ops = [op for pair in zip(qk_ops, pv_ops) for op in pair]
                ops += qk_ops[len(pv_ops):] + pv_ops[len(qk_ops):]
            else:
                ops = qk_ops + pv_ops
            for op in ops:
                op()

    step((0, 0, 0, True), None)

    def diag_items(j, _):
        for u in range(2):
            t = 2 * j + u
            step((t + 1, t + 1, (u + 1) % 2, True), (t, t, u), first=True)
        return 0

    lax.fori_loop(0, n_q // 2 - 1, diag_items, 0)
    step((n_q - 1, n_q - 1, 1, True), (n_q - 2, n_q - 2, 0), first=True)
    step((qtab_ref[0], ttab_ref[0], 0, False), (n_q - 1, n_q - 1, 1), first=True)

    def past_items(j, _):
        for u in range(PIPE_UNROLL):
            t = PIPE_UNROLL * j + u
            step((qtab_ref[t + 1], ttab_ref[t + 1], (u + 1) % 2, False),
                 (qtab_ref[t], ttab_ref[t], u % 2))
        return 0

    lax.fori_loop(0, n_q * (n_q - 1) // (2 * PIPE_UNROLL), past_items, 0)


def _diff_kernel(qtab_ref, ttab_ref, lq1_ref, lk1_ref, lq2_ref, lk2_ref, q_ref, k_ref, vt_ref,
                 dg_ref, sg_ref, o_ref, acc1_ref, acc2_ref, m1_ref, m2_ref, s1_ref, s2_ref, cm_ref):
    n_q = acc1_ref.shape[0]
    streams = (_Stream(s1_ref, acc1_ref, m1_ref), _Stream(s2_ref, acc2_ref, m2_ref))
    _run_attention(
        qtab_ref, ttab_ref, streams, 1, k_ref, cm_ref,
        q_of=lambda st, rows: _split_heads(q_ref[rows, :])[st],
        vt_of=lambda st, tile: vt_ref[0, tile],
        bias_of=lambda st, qi, tile, j: None,
        interleave="stream")

    dv = 2 * DIFF_HEAD_DIM
    lam = (jnp.exp(jnp.sum(lq1_ref[...] * lk1_ref[...], axis=-1, keepdims=True))
           - jnp.exp(jnp.sum(lq2_ref[...] * lk2_ref[...], axis=-1, keepdims=True)) + LAM_INIT)

    def finalize(qi, _):
        ot = (acc1_ref[qi, :dv] / acc1_ref[qi, dv:dv + 1]
              - lam * (acc2_ref[qi, :dv] / acc2_ref[qi, dv:dv + 1]))
        ms = jnp.mean(ot * ot, axis=0, keepdims=True)
        ot = ot * lax.rsqrt(ms + SUBLN_EPS) * sg_ref[...] * (1.0 - LAM_INIT)
        o_ref[_tile_rows(qi), :] = (ot.T * dg_ref[_tile_rows(qi), :]).astype(_BF16)
        return 0

    lax.fori_loop(0, n_q // 2, lambda j, c: finalize(2 * j + 1, finalize(2 * j, c)), 0)


def _moba_kernel(qtab_ref, ttab_ref, q_ref, k_ref, vt_ref, bias_ref, mg_ref, o_ref,
                 acca_ref, accb_ref, ma_ref, mb_ref, sa_ref, sb_ref, cm_ref):
    n_q = acca_ref.shape[0]
    per_tile = ATT_TILE // MOBA_BLOCK
    streams = (_Stream(sa_ref, acca_ref, ma_ref), _Stream(sb_ref, accb_ref, mb_ref))
    _run_attention(
        qtab_ref, ttab_ref, streams, per_tile, k_ref, cm_ref,
        q_of=lambda st, rows: _split_heads(q_ref[rows, :])[st],
        vt_of=lambda st, tile: vt_ref[0, tile, MOBA_V_ROWS * st:MOBA_V_ROWS * (st + 1), :],
        bias_of=lambda st, qi, tile, j: bias_ref[0, st, qi, pl.ds(per_tile * tile + j, 1), :],
        interleave="part")

    dh = MOBA_HEAD_DIM

    def finalize(qi, _):
        ot = jnp.concatenate([acca_ref[qi, :dh] / acca_ref[qi, dh:dh + 1],
                              accb_ref[qi, :dh] / accb_ref[qi, dh:dh + 1]], axis=0)
        o_ref[_tile_rows(qi), :] = (ot.T * mg_ref[_tile_rows(qi), :]).astype(_BF16)
        return 0

    lax.fori_loop(0, n_q // 2, lambda j, c: finalize(2 * j + 1, finalize(2 * j, c)), 0)


def _out_kernel(oa_ref, ob_ref, gates_ref, x_ref, p_ref, wbd_ref, wbm_ref, wout_ref, wpg_ref,
                wple_ref, fg_ref, o_ref, wbd_bf, wbm_bf, wout_bf, wpg_bf, wple_bf):
    @pl.when(pl.program_id(0) == 0)
    def _():
        for src, dst in ((wbd_ref, wbd_bf), (wbm_ref, wbm_bf), (wout_ref, wout_bf),
                         (wpg_ref, wpg_bf), (wple_ref, wple_bf)):
            dst[...] = src[...].astype(_BF16)

    half = ROW_TILE // 2
    for r in (slice(0, half), slice(half, ROW_TILE)):
        ya = jnp.dot(oa_ref[r], wbd_bf[...], preferred_element_type=_F32)
        yb = jnp.dot(ob_ref[r], wbm_bf[...], preferred_element_type=_F32)
        merged = gates_ref[r, :D_MODEL] * ya + gates_ref[r, D_MODEL:] * yb
        x1 = x_ref[r] + jnp.dot(merged.astype(_BF16), wout_bf[...], preferred_element_type=_F32)
        t = jnp.dot(x1.astype(_BF16), wpg_bf[...], preferred_element_type=_F32)
        pe = jnp.dot(p_ref[r].astype(_BF16), wple_bf[...], preferred_element_type=_F32)
        x2 = x1 + jax.nn.sigmoid(t) * pe
        ms = jnp.mean(x2 * x2, axis=-1, keepdims=True)
        o_ref[r] = x2 * lax.rsqrt(ms + EPS) * fg_ref[...]


def _rope_lane_tables(seq):
    half = ROT_DIM // 2
    inv = ROPE_THETA ** (-np.arange(0, ROT_DIM, 2, dtype=np.float64) / ROT_DIM)
    ang = np.arange(seq, dtype=np.float64)[:, None] * inv[None, :]
    cos, sin = np.cos(ang), np.sin(ang)
    pad = MOBA_HEAD_DIM - ROT_DIM
    one, zero = np.ones((seq, pad)), np.zeros((seq, pad))
    zh = np.zeros((seq, half))
    cos64 = np.concatenate([cos, cos, one], axis=1)
    slo64 = np.concatenate([-sin, zh, zero], axis=1)
    shi64 = np.concatenate([zh, sin, zero], axis=1)
    return tuple(jnp.asarray(np.tile(t, (1, LANES // MOBA_HEAD_DIM)), _F32)
                 for t in (cos64, slo64, shi64))


def _resident(shape):
    return pl.BlockSpec(shape, lambda *_: (0,) * len(shape), pipeline_mode=pl.Buffered(1))


def _params(n_axes):
    return pltpu.CompilerParams(dimension_semantics=("arbitrary",) * n_axes,
                                vmem_limit_bytes=VMEM_LIMIT)


def kernel(x, p, norm_g, w_in, lambda_q1, lambda_k1, lambda_q2, lambda_k2, subln_g,
           w_branch_diff, w_branch_moba, w_out, w_ple, w_ple_gate, final_g):
    B, S, _ = x.shape
    assert w_in.shape == (1, D_MODEL, N_CHUNKS * CHUNK) and S % (2 * ATT_TILE) == 0
    rows = B * S
    n_row_tiles = rows // ROW_TILE
    n_kv = S // ATT_TILE
    n_blk = S // MOBA_BLOCK
    x2d = x.reshape(rows, D_MODEL)
    cosv, sin_lo, sin_hi = _rope_lane_tables(S)

    row_spec = lambda w: pl.BlockSpec((ROW_TILE, w), lambda i: (i, 0))
    tab_spec = pl.BlockSpec((ROW_TILE, LANES), lambda i: (i % n_kv, 0))
    dvt_rows, mvt_rows = DIFF_HEADS * DIFF_V_ROWS, MOBA_HEADS * MOBA_V_ROWS
    vt_spec = lambda r: pl.BlockSpec((1, 1, r, ATT_TILE), lambda i: (i // n_kv, i % n_kv, 0, 0))
    vt_shape = lambda r: jax.ShapeDtypeStruct((B, n_kv, r, ATT_TILE), _BF16)
    rows_bf16 = jax.ShapeDtypeStruct((rows, CHUNK), _BF16)
    rows_f32 = jax.ShapeDtypeStruct((rows, CHUNK), _F32)
    (dq, dk, dvt, dgs, mq, mk, mvt, mgs, bias, gates) = pl.pallas_call(
        _proj_kernel,
        grid=(n_row_tiles,),
        in_specs=[row_spec(D_MODEL), _resident((1, D_MODEL)), _resident((D_MODEL, N_CHUNKS * CHUNK)),
                  tab_spec, tab_spec, tab_spec],
        out_specs=[row_spec(CHUNK), row_spec(CHUNK), vt_spec(dvt_rows), row_spec(CHUNK), row_spec(CHUNK),
                   row_spec(CHUNK), vt_spec(mvt_rows), row_spec(CHUNK),
                   pl.BlockSpec((1, MOBA_HEADS, 1, n_blk, ROW_TILE),
                                lambda i: (i // n_kv, 0, i % n_kv, 0, 0)),
                   row_spec(4 * CHUNK)],
        out_shape=[rows_bf16, rows_bf16, vt_shape(dvt_rows), rows_f32, rows_bf16, rows_bf16,
                   vt_shape(mvt_rows), rows_f32,
                   jax.ShapeDtypeStruct((B, MOBA_HEADS, n_kv, n_blk, ROW_TILE), _F32),
                   jax.ShapeDtypeStruct((rows, 4 * CHUNK), _F32)],
        scratch_shapes=[pltpu.VMEM((n_blk, CHUNK), _F32)],
        compiler_params=_params(1),
        name="proj",
    )(x2d, norm_g[0].reshape(1, D_MODEL), w_in[0], cosv, sin_lo, sin_hi)

    n_groups = CHUNK // LANES
    qtab, ttab = _past_items(n_kv)
    seq_spec = pl.BlockSpec((S, LANES), lambda b, g, *_: (b, g))
    v_spec = lambda r: pl.BlockSpec((1, n_kv, r, ATT_TILE), lambda b, g, *_: (b, 0, g, 0))
    lam_spec = pl.BlockSpec((1, DIFF_HEAD_DIM), lambda b, g, *_: (0, 0))
    acc = lambda r: pltpu.VMEM((n_kv, r, ATT_TILE), _F32)
    run_max = pltpu.VMEM((n_kv, 1, ATT_TILE), _F32)
    score_buf = pltpu.VMEM((2, ATT_TILE, ATT_TILE), _F32)
    col_max = lambda n: pltpu.VMEM((2, n, 1, ATT_TILE), _F32)

    o_a = pl.pallas_call(
        _diff_kernel,
        grid_spec=pltpu.PrefetchScalarGridSpec(
            num_scalar_prefetch=2,
            grid=(B, n_groups),
            in_specs=[lam_spec, lam_spec, lam_spec, lam_spec, seq_spec, seq_spec, v_spec(DIFF_V_ROWS),
                      seq_spec, pl.BlockSpec((2 * DIFF_HEAD_DIM, 1), lambda b, g, *_: (0, 0))],
            out_specs=seq_spec,
            scratch_shapes=[acc(DIFF_V_ROWS), acc(DIFF_V_ROWS), run_max, run_max, score_buf, score_buf,
                            col_max(2)],
        ),
        out_shape=rows_bf16,
        compiler_params=_params(2),
        name="diff_attn",
    )(qtab, ttab, lambda_q1[0:1], lambda_k1[0:1], lambda_q2[0:1], lambda_k2[0:1], dq, dk, dvt, dgs,
      subln_g[0].reshape(2 * DIFF_HEAD_DIM, 1))

    o_b = pl.pallas_call(
        _moba_kernel,
        grid_spec=pltpu.PrefetchScalarGridSpec(
            num_scalar_prefetch=2,
            grid=(B, n_groups),
            in_specs=[seq_spec, seq_spec, v_spec(2 * MOBA_V_ROWS),
                      pl.BlockSpec((1, LANES // MOBA_HEAD_DIM, n_kv, n_blk, ATT_TILE),
                                   lambda b, g, *_: (b, g, 0, 0, 0)), seq_spec],
            out_specs=seq_spec,
            scratch_shapes=[acc(MOBA_V_ROWS), acc(MOBA_V_ROWS), run_max, run_max, score_buf, score_buf,
                            col_max(2 * (ATT_TILE // MOBA_BLOCK))],
        ),
        out_shape=rows_bf16,
        compiler_params=_params(2),
        name="moba_attn",
    )(qtab, ttab, mq, mk, mvt, bias, mgs)

    out = pl.pallas_call(
        _out_kernel,
        grid=(n_row_tiles,),
        in_specs=[row_spec(CHUNK), row_spec(CHUNK), row_spec(4 * CHUNK), row_spec(D_MODEL),
                  row_spec(PLE_DIM), _resident((DIFF_WIDTH, D_MODEL)), _resident((MOBA_WIDTH, D_MODEL)),
                  _resident((D_MODEL, D_MODEL)), _resident((D_MODEL, D_MODEL)),
                  _resident((PLE_DIM, D_MODEL)), _resident((1, D_MODEL))],
        out_specs=row_spec(D_MODEL),
        out_shape=jax.ShapeDtypeStruct((rows, D_MODEL), _F32),
        scratch_shapes=[pltpu.VMEM(s, _BF16) for s in ((DIFF_WIDTH, D_MODEL), (MOBA_WIDTH, D_MODEL),
                                                      (D_MODEL, D_MODEL), (D_MODEL, D_MODEL),
                                                      (PLE_DIM, D_MODEL))],
        compiler_params=_params(1),
        name="out_proj",
    )(o_a, o_b, gates, x2d, p[0].reshape(rows, PLE_DIM), w_branch_diff[0], w_branch_moba[0],
      w_out[0], w_ple_gate[0], w_ple[0], final_g.reshape(1, D_MODEL))
    return out.reshape(B, S, D_MODEL)
```

```python
import collections
import functools
import math

import jax
import jax.numpy as jnp
import numpy as np
from jax import lax
from jax.experimental import pallas as pl
from jax.experimental.pallas import tpu as pltpu

D_MODEL = 1024
PLE_DIM = 256
DIFF_HEADS = 4
DIFF_HEAD_DIM = 64
DIFF_WIDTH = DIFF_HEADS * 2 * DIFF_HEAD_DIM
MOBA_HEADS = 8
MOBA_HEAD_DIM = 64
MOBA_WIDTH = MOBA_HEADS * MOBA_HEAD_DIM
MOBA_BLOCK = 256
MOBA_TOPK = 3
ROT_DIM = 16
ROPE_THETA = 500000.0
EPS = 1e-6
SUBLN_EPS = 1e-5
LAM_INIT = 0.8 - 0.6 * math.exp(-0.3 * 0)

LANES = 128
MXU_COLS = 256
BF16_ROWS = 16
CHUNK = 512
N_CHUNKS = 12
ROW_TILE = 512
ATT_TILE = 512
PIPE_UNROLL = 24
VMEM_LIMIT = 56 * 1024 * 1024
_SCORE_SCALE = DIFF_HEAD_DIM ** -0.5 * math.log2(math.e)
DIFF_V_ROWS = 2 * DIFF_HEAD_DIM + BF16_ROWS
MOBA_V_ROWS = MOBA_HEAD_DIM + BF16_ROWS
assert DIFF_HEAD_DIM == MOBA_HEAD_DIM and ATT_TILE % MOBA_BLOCK == 0 and ROW_TILE == ATT_TILE

_F32 = jnp.float32
_BF16 = jnp.bfloat16
_NT = (((1,), (1,)), ((), ()))


def _rope_rows(a, cosv, sin_lo, sin_hi):
    outs = []
    for j in range(CHUNK // LANES):
        xs = a[:, LANES * j:LANES * (j + 1)]
        up = pltpu.roll(xs, LANES - ROT_DIM // 2, 1)
        dn = pltpu.roll(xs, ROT_DIM // 2, 1)
        outs.append(xs * cosv + up * sin_lo + dn * sin_hi)
    return jnp.concatenate(outs, axis=1)


def _split_heads(q):
    lane = lax.broadcasted_iota(jnp.int32, q.shape, 1)
    zero = jnp.zeros_like(q)
    return jnp.where(lane < 64, q, zero), jnp.where(lane >= 64, q, zero)


def _moba_routing(bias_ref, cols, first_head, q, km_all, own):
    nb = km_all.shape[0]
    n_heads = q.shape[1] // MOBA_HEAD_DIM
    assert n_heads * nb == LANES
    blk = lax.broadcasted_iota(jnp.int32, (nb, q.shape[0]), 0)
    lane_head = lax.shift_right_logical(lax.broadcasted_iota(jnp.int32, km_all.shape, 1),
                                        MOBA_HEAD_DIM.bit_length() - 1)
    km_rows = jnp.concatenate([jnp.where(lane_head == h, km_all, 0.0) for h in range(n_heads)], axis=0)
    km_hi = km_rows.astype(_BF16)
    km_lo = (km_rows - km_hi.astype(_F32)).astype(_BF16)
    q_hi = q.astype(_BF16)
    q_lo = (q - q_hi.astype(_F32)).astype(_BF16)
    g_t = lax.dot_general(jnp.concatenate([q_hi, q_hi, q_lo], axis=1),
                          jnp.concatenate([km_hi, km_lo, km_hi], axis=1), _NT,
                          preferred_element_type=_F32)
    g_both = g_t.T
    for hh in range(n_heads):
        g = jnp.where(blk < own, g_both[nb * hh:nb * (hh + 1)], -jnp.inf)
        sel = jnp.zeros(g.shape, jnp.bool_)
        for _ in range(MOBA_TOPK):
            mx = jnp.max(g, axis=0, keepdims=True)
            first_idx = jnp.min(jnp.where(g == mx, blk, nb), axis=0, keepdims=True)
            pick = (blk == first_idx) & (mx > -jnp.inf)
            sel = sel | pick
            g = jnp.where(pick, -jnp.inf, g)
        bias_ref[0, first_head + hh, 0, :, cols] = jnp.where(sel | (blk == own), 0.0, -jnp.inf)


def _proj_kernel(x_ref, g_ref, w_ref, cos_ref, slo_ref, shi_ref,
                 dq_ref, dk_ref, dvt_ref, dg_ref, mq_ref, mk_ref, mvt_ref, mg_ref,
                 bias_ref, gates_ref, km_ref):
    per_tile = ROW_TILE // MOBA_BLOCK
    n_kv = km_ref.shape[0] // per_tile

    @pl.when(pl.program_id(0) == 0)
    def _():
        km_ref[...] = jnp.zeros(km_ref.shape, _F32)

    def block_stages(j):
        r = slice(MOBA_BLOCK * j, MOBA_BLOCK * (j + 1))
        x = x_ref[r]
        ms = jnp.mean(x * x, axis=-1, keepdims=True)
        h = (x * lax.rsqrt(ms + EPS) * g_ref[...]).astype(_BF16)
        rope = functools.partial(_rope_rows, cosv=cos_ref[r], sin_lo=slo_ref[r], sin_hi=shi_ref[r])

        def proj(c):
            return jnp.dot(h, w_ref[:, CHUNK * c:CHUNK * (c + 1)].astype(_BF16),
                           preferred_element_type=_F32)

        def store_transposed(ref, a, head_rows):
            at = a.T.astype(_BF16)
            ones = jnp.ones((BF16_ROWS, MOBA_BLOCK), _BF16)
            stride = head_rows + BF16_ROWS
            for g in range(CHUNK // head_rows):
                ref[0, 0, stride * g:stride * g + head_rows, r] = at[head_rows * g:head_rows * (g + 1)]
                ref[0, 0, stride * g + head_rows:stride * (g + 1), r] = ones

        own = per_tile * (pl.program_id(0) % n_kv) + j
        live = {}

        def moba_qk():
            live["mq"] = rope(proj(4))
            mq_ref[r] = (live["mq"] * _SCORE_SCALE).astype(_BF16)
            mk = rope(proj(5))
            mk_ref[r] = mk.astype(_BF16)
            km_ref[pl.ds(own, 1), :] = jnp.mean(mk, axis=0, keepdims=True)

        def route():
            for half in range(CHUNK // MXU_COLS):
                lanes = slice(MXU_COLS * half, MXU_COLS * (half + 1))
                _moba_routing(bias_ref, r, half * (MXU_COLS // MOBA_HEAD_DIM),
                              live["mq"][:, lanes], km_ref[:, lanes], own)

        def diff_qk():
            dq_ref[r] = (rope(proj(0)) * _SCORE_SCALE).astype(_BF16)
            dk_ref[r] = rope(proj(1)).astype(_BF16)

        def diff_vg():
            store_transposed(dvt_ref, proj(2), 2 * DIFF_HEAD_DIM)
            dg_ref[r] = jax.nn.silu(proj(3))

        def rest():
            store_transposed(mvt_ref, proj(6), MOBA_HEAD_DIM)
            mg_ref[r] = jax.nn.silu(proj(7))
            for c in range(8, N_CHUNKS):
                gates_ref[r, CHUNK * (c - 8):CHUNK * (c - 7)] = jax.nn.sigmoid(proj(c))

        return moba_qk, route, diff_qk, diff_vg, rest

    first, second = block_stages(0), block_stages(1)
    first[0]()
    second[0]()
    first[2]()
    first[1]()
    first[3]()
    second[1]()
    first[4]()
    second[2]()
    second[3]()
    second[4]()


def _tile_rows(i):
    return pl.ds(pl.multiple_of(i * ATT_TILE, ATT_TILE), ATT_TILE)


def _past_items(n_q):
    items = [(qi, t) for qi in range(1, n_q) for t in range(qi)]
    assert len(items) % PIPE_UNROLL == 0 and PIPE_UNROLL % 2 == 0
    items.append(items[-1])
    return (np.array([i[0] for i in items], np.int32), np.array([i[1] for i in items], np.int32))


_Stream = collections.namedtuple("_Stream", "s_ref acc_ref m_ref")


def _run_attention(qtab_ref, ttab_ref, streams, n_seg, k_ref, cm_ref, q_of, vt_of, bias_of,
                   interleave):
    n_q = streams[0].acc_ref.shape[0]
    seg_rows = ATT_TILE // n_seg
    assert interleave in ("part", "stream") and seg_rows % MXU_COLS == 0

    def softmax(st, qi, tile, slot, first):
        stream = streams[st]
        biases = [bias_of(st, qi, tile, j) for j in range(n_seg)]
        cands = [cm_ref[slot, st * n_seg + j] if b is None else cm_ref[slot, st * n_seg + j] + b
                 for j, b in enumerate(biases)]
        m_old = None if first else stream.m_ref[qi]
        m_new = functools.reduce(jnp.maximum, cands if first else [m_old] + cands)
        stream.m_ref[qi] = m_new
        p = [jnp.exp2(stream.s_ref[slot, seg_rows * j:seg_rows * (j + 1), :]
                      - (m_new if b is None else m_new - b)).astype(_BF16)
             for j, b in enumerate(biases)]
        p = p[0] if n_seg == 1 else jnp.concatenate(p, axis=0)
        return p, (None if first else jnp.exp2(m_old - m_new))

    def pieces(diagonal):
        if diagonal:
            return [(c, MXU_COLS, c + MXU_COLS) for c in range(0, ATT_TILE, MXU_COLS)]
        width = MXU_COLS if interleave == "part" else ATT_TILE
        return [(c, width, ATT_TILE) for c in range(0, ATT_TILE, width)]

    def qk_piece(st, k, qi, slot, diagonal, c0, width, keys):
        cols = slice(c0, c0 + width)
        rows = pl.ds(pl.multiple_of(qi * ATT_TILE + c0, width), width)
        s = lax.dot_general(k[:keys], q_of(st, rows), _NT, preferred_element_type=_F32)
        if diagonal:
            kpos = lax.broadcasted_iota(jnp.int32, s.shape, 0)
            qpos = lax.broadcasted_iota(jnp.int32, s.shape, 1) + c0
            s = jnp.where(kpos <= qpos, s, -jnp.inf)
        s_ref = streams[st].s_ref
        s_ref[slot, :keys, cols] = s
        if keys < ATT_TILE:
            s_ref[slot, keys:, cols] = jnp.full((ATT_TILE - keys, width), -jnp.inf, _F32)
        for j in range(n_seg):
            seg = s[seg_rows * j:seg_rows * (j + 1)]
            cm_ref[slot, st * n_seg + j, :, cols] = (
                jnp.max(seg, axis=0, keepdims=True) if seg.shape[0]
                else jnp.full((1, width), -jnp.inf, _F32))

    def pv_piece(st, vt, p, alpha, qi, c0, width, keys):
        cols = slice(c0, c0 + width)
        acc_ref = streams[st].acc_ref
        pv = jnp.dot(vt[:, :keys], p[:keys, cols], preferred_element_type=_F32)
        acc_ref[qi, :, cols] = pv if alpha is None else alpha[:, cols] * acc_ref[qi, :, cols] + pv

    def step(nxt, cur, first=False):
        k = None if nxt is None else k_ref[_tile_rows(nxt[1]), :]
        for st in range(len(streams)):
            qk_ops, pv_ops = [], []
            if nxt is not None:
                qk_ops = [functools.partial(qk_piece, st, k, nxt[0], nxt[2], nxt[3], *pc)
                          for pc in pieces(nxt[3])]
            if cur is not None:
                p, alpha = softmax(st, cur[0], cur[1], cur[2], first)
                vt = vt_of(st, cur[1])
                pv_ops = [functools.partial(pv_piece, st, vt, p, alpha, cur[0], *pc)
                          for pc in pieces(first)]
            if interleave == "part":
                ops = [op for pair in zip(qk_ops, pv_ops) for op in pair]
                ops += qk_ops[len(pv_ops):] + pv_ops[len(qk_ops):]
            else:
                ops = qk_ops + pv_ops
            for op in ops:
                op()

    step((0, 0, 0, True), None)

    def diag_items(j, _):
        for u in range(2):
            t = 2 * j + u
            step((t + 1, t + 1, (u + 1) % 2, True), (t, t, u), first=True)
        return 0

    lax.fori_loop(0, n_q // 2 - 1, diag_items, 0)
    step((n_q - 1, n_q - 1, 1, True), (n_q - 2, n_q - 2, 0), first=True)
    step((qtab_ref[0], ttab_ref[0], 0, False), (n_q - 1, n_q - 1, 1), first=True)

    def past_items(j, _):
        for u in range(PIPE_UNROLL):
            t = PIPE_UNROLL * j + u
            step((qtab_ref[t + 1], ttab_ref[t + 1], (u + 1) % 2, False),
                 (qtab_ref[t], ttab_ref[t], u % 2))
        return 0

    lax.fori_loop(0, n_q * (n_q - 1) // (2 * PIPE_UNROLL), past_items, 0)


def _diff_kernel(qtab_ref, ttab_ref, lq1_ref, lk1_ref, lq2_ref, lk2_ref, q_ref, k_ref, vt_ref,
                 dg_ref, sg_ref, o_ref, acc1_ref, acc2_ref, m1_ref, m2_ref, s1_ref, s2_ref, cm_ref):
    n_q = acc1_ref.shape[0]
    streams = (_Stream(s1_ref, acc1_ref, m1_ref), _Stream(s2_ref, acc2_ref, m2_ref))
    _run_attention(
        qtab_ref, ttab_ref, streams, 1, k_ref, cm_ref,
        q_of=lambda st, rows: _split_heads(q_ref[rows, :])[st],
        vt_of=lambda st, tile: vt_ref[0, tile],
        bias_of=lambda st, qi, tile, j: None,
        interleave="stream")

    dv = 2 * DIFF_HEAD_DIM
    lam = (jnp.exp(jnp.sum(lq1_ref[...] * lk1_ref[...], axis=-1, keepdims=True))
           - jnp.exp(jnp.sum(lq2_ref[...] * lk2_ref[...], axis=-1, keepdims=True)) + LAM_INIT)

    def finalize(qi, _):
        ot = (acc1_ref[qi, :dv] / acc1_ref[qi, dv:dv + 1]
              - lam * (acc2_ref[qi, :dv] / acc2_ref[qi, dv:dv + 1]))
        ms = jnp.mean(ot * ot, axis=0, keepdims=True)
        ot = ot * lax.rsqrt(ms + SUBLN_EPS) * sg_ref[...] * (1.0 - LAM_INIT)
        o_ref[_tile_rows(qi), :] = (ot.T * dg_ref[_tile_rows(qi), :]).astype(_BF16)
        return 0

    lax.fori_loop(0, n_q // 2, lambda j, c: finalize(2 * j + 1, finalize(2 * j, c)), 0)


def _moba_kernel(qtab_ref, ttab_ref, q_ref, k_ref, vt_ref, bias_ref, mg_ref, o_ref,
                 acca_ref, accb_ref, ma_ref, mb_ref, sa_ref, sb_ref, cm_ref):
    n_q = acca_ref.shape[0]
    per_tile = ATT_TILE // MOBA_BLOCK
    streams = (_Stream(sa_ref, acca_ref, ma_ref), _Stream(sb_ref, accb_ref, mb_ref))
    _run_attention(
        qtab_ref, ttab_ref, streams, per_tile, k_ref, cm_ref,
        q_of=lambda st, rows: _split_heads(q_ref[rows, :])[st],
        vt_of=lambda st, tile: vt_ref[0, tile, MOBA_V_ROWS * st:MOBA_V_ROWS * (st + 1), :],
        bias_of=lambda st, qi, tile, j: bias_ref[0, st, qi, pl.ds(per_tile * tile + j, 1), :],
        interleave="part")

    dh = MOBA_HEAD_DIM

    def finalize(qi, _):
        ot = jnp.concatenate([acca_ref[qi, :dh] / acca_ref[qi, dh:dh + 1],
                              accb_ref[qi, :dh] / accb_ref[qi, dh:dh + 1]], axis=0)
        o_ref[_tile_rows(qi), :] = (ot.T * mg_ref[_tile_rows(qi), :]).astype(_BF16)
        return 0

    lax.fori_loop(0, n_q // 2, lambda j, c: finalize(2 * j + 1, finalize(2 * j, c)), 0)


def _out_kernel(oa_ref, ob_ref, gates_ref, x_ref, p_ref, wbd_ref, wbm_ref, wout_ref, wpg_ref,
                wple_ref, fg_ref, o_ref, wbd_bf, wbm_bf, wout_bf, wpg_bf, wple_bf):
    @pl.when(pl.program_id(0) == 0)
    def _():
        for src, dst in ((wbd_ref, wbd_bf), (wbm_ref, wbm_bf), (wout_ref, wout_bf),
                         (wpg_ref, wpg_bf), (wple_ref, wple_bf)):
            dst[...] = src[...].astype(_BF16)

    half = ROW_TILE // 2
    for r in (slice(0, half), slice(half, ROW_TILE)):
        ya = jnp.dot(oa_ref[r], wbd_bf[...], preferred_element_type=_F32)
        yb = jnp.dot(ob_ref[r], wbm_bf[...], preferred_element_type=_F32)
        merged = gates_ref[r, :D_MODEL] * ya + gates_ref[r, D_MODEL:] * yb
        x1 = x_ref[r] + jnp.dot(merged.astype(_BF16), wout_bf[...], preferred_element_type=_F32)
        t = jnp.dot(x1.astype(_BF16), wpg_bf[...], preferred_element_type=_F32)
        pe = jnp.dot(p_ref[r].astype(_BF16), wple_bf[...], preferred_element_type=_F32)
        x2 = x1 + jax.nn.sigmoid(t) * pe
        ms = jnp.mean(x2 * x2, axis=-1, keepdims=True)
        o_ref[r] = x2 * lax.rsqrt(ms + EPS) * fg_ref[...]


def _rope_lane_tables(seq):
    half = ROT_DIM // 2
    inv = ROPE_THETA ** (-np.arange(0, ROT_DIM, 2, dtype=np.float64) / ROT_DIM)
    ang = np.arange(seq, dtype=np.float64)[:, None] * inv[None, :]
    cos, sin = np.cos(ang), np.sin(ang)
    pad = MOBA_HEAD_DIM - ROT_DIM
    one, zero = np.ones((seq, pad)), np.zeros((seq, pad))
    zh = np.zeros((seq, half))
    cos64 = np.concatenate([cos, cos, one], axis=1)
    slo64 = np.concatenate([-sin, zh, zero], axis=1)
    shi64 = np.concatenate([zh, sin, zero], axis=1)
    return tuple(jnp.asarray(np.tile(t, (1, LANES // MOBA_HEAD_DIM)), _F32)
                 for t in (cos64, slo64, shi64))


def _resident(shape):
    return pl.BlockSpec(shape, lambda *_: (0,) * len(shape), pipeline_mode=pl.Buffered(1))


def _params(n_axes):
    return pltpu.CompilerParams(dimension_semantics=("arbitrary",) * n_axes,
                                vmem_limit_bytes=VMEM_LIMIT)


def kernel(x, p, norm_g, w_in, lambda_q1, lambda_k1, lambda_q2, lambda_k2, subln_g,
           w_branch_diff, w_branch_moba, w_out, w_ple, w_ple_gate, final_g):
    B, S, _ = x.shape
    assert w_in.shape == (1, D_MODEL, N_CHUNKS * CHUNK) and S % (2 * ATT_TILE) == 0
    rows = B * S
    n_row_tiles = rows // ROW_TILE
    n_kv = S // ATT_TILE
    n_blk = S // MOBA_BLOCK
    x2d = x.reshape(rows, D_MODEL)
    cosv, sin_lo, sin_hi = _rope_lane_tables(S)

    row_spec = lambda w: pl.BlockSpec((ROW_TILE, w), lambda i: (i, 0))
    tab_spec = pl.BlockSpec((ROW_TILE, LANES), lambda i: (i % n_kv, 0))
    dvt_rows, mvt_rows = DIFF_HEADS * DIFF_V_ROWS, MOBA_HEADS * MOBA_V_ROWS
    vt_spec = lambda r: pl.BlockSpec((1, 1, r, ATT_TILE), lambda i: (i // n_kv, i % n_kv, 0, 0))
    vt_shape = lambda r: jax.ShapeDtypeStruct((B, n_kv, r, ATT_TILE), _BF16)
    rows_bf16 = jax.ShapeDtypeStruct((rows, CHUNK), _BF16)
    rows_f32 = jax.ShapeDtypeStruct((rows, CHUNK), _F32)
    (dq, dk, dvt, dgs, mq, mk, mvt, mgs, bias, gates) = pl.pallas_call(
        _proj_kernel,
        grid=(n_row_tiles,),
        in_specs=[row_spec(D_MODEL), _resident((1, D_MODEL)), _resident((D_MODEL, N_CHUNKS * CHUNK)),
                  tab_spec, tab_spec, tab_spec],
        out_specs=[row_spec(CHUNK), row_spec(CHUNK), vt_spec(dvt_rows), row_spec(CHUNK), row_spec(CHUNK),
                   row_spec(CHUNK), vt_spec(mvt_rows), row_spec(CHUNK),
                   pl.BlockSpec((1, MOBA_HEADS, 1, n_blk, ROW_TILE),
                                lambda i: (i // n_kv, 0, i % n_kv, 0, 0)),
                   row_spec(4 * CHUNK)],
        out_shape=[rows_bf16, rows_bf16, vt_shape(dvt_rows), rows_f32, rows_bf16, rows_bf16,
                   vt_shape(mvt_rows), rows_f32,
                   jax.ShapeDtypeStruct((B, MOBA_HEADS, n_kv, n_blk, ROW_TILE), _F32),
                   jax.ShapeDtypeStruct((rows, 4 * CHUNK), _F32)],
        scratch_shapes=[pltpu.VMEM((n_blk, CHUNK), _F32)],
        compiler_params=_params(1),
        name="proj",
    )(x2d, norm_g[0].reshape(1, D_MODEL), w_in[0], cosv, sin_lo, sin_hi)

    n_groups = CHUNK // LANES
    qtab, ttab = _past_items(n_kv)
    seq_spec = pl.BlockSpec((S, LANES), lambda b, g, *_: (b, g))
    v_spec = lambda r: pl.BlockSpec((1, n_kv, r, ATT_TILE), lambda b, g, *_: (b, 0, g, 0))
    lam_spec = pl.BlockSpec((1, DIFF_HEAD_DIM), lambda b, g, *_: (0, 0))
    acc = lambda r: pltpu.VMEM((n_kv, r, ATT_TILE), _F32)
    run_max = pltpu.VMEM((n_kv, 1, ATT_TILE), _F32)
    score_buf = pltpu.VMEM((2, ATT_TILE, ATT_TILE), _F32)
    col_max = lambda n: pltpu.VMEM((2, n, 1, ATT_TILE), _F32)

    o_a = pl.pallas_call(
        _diff_kernel,
        grid_spec=pltpu.PrefetchScalarGridSpec(
            num_scalar_prefetch=2,
            grid=(B, n_groups),
            in_specs=[lam_spec, lam_spec, lam_spec, lam_spec, seq_spec, seq_spec, v_spec(DIFF_V_ROWS),
                      seq_spec, pl.BlockSpec((2 * DIFF_HEAD_DIM, 1), lambda b, g, *_: (0, 0))],
            out_specs=seq_spec,
            scratch_shapes=[acc(DIFF_V_ROWS), acc(DIFF_V_ROWS), run_max, run_max, score_buf, score_buf,
                            col_max(2)],
        ),
        out_shape=rows_bf16,
        compiler_params=_params(2),
        name="diff_attn",
    )(qtab, ttab, lambda_q1[0:1], lambda_k1[0:1], lambda_q2[0:1], lambda_k2[0:1], dq, dk, dvt, dgs,
      subln_g[0].reshape(2 * DIFF_HEAD_DIM, 1))

    o_b = pl.pallas_call(
        _moba_kernel,
        grid_spec=pltpu.PrefetchScalarGridSpec(
            num_scalar_prefetch=2,
            grid=(B, n_groups),
            in_specs=[seq_spec, seq_spec, v_spec(2 * MOBA_V_ROWS),
                      pl.BlockSpec((1, LANES // MOBA_HEAD_DIM, n_kv, n_blk, ATT_TILE),
                                   lambda b, g, *_: (b, g, 0, 0, 0)), seq_spec],
            out_specs=seq_spec,
            scratch_shapes=[acc(MOBA_V_ROWS), acc(MOBA_V_ROWS), run_max, run_max, score_buf, score_buf,
                            col_max(2 * (ATT_TILE // MOBA_BLOCK))],
        ),
        out_shape=rows_bf16,
        compiler_params=_params(2),
        name="moba_attn",
    )(qtab, ttab, mq, mk, mvt, bias, mgs)

    out = pl.pallas_call(
        _out_kernel,
        grid=(n_row_tiles,),
        in_specs=[row_spec(CHUNK), row_spec(CHUNK), row_spec(4 * CHUNK), row_spec(D_MODEL),
                  row_spec(PLE_DIM), _resident((DIFF_WIDTH, D_MODEL)), _resident((MOBA_WIDTH, D_MODEL)),
                  _resident((D_MODEL, D_MODEL)), _resident((D_MODEL, D_MODEL)),
                  _resident((PLE_DIM, D_MODEL)), _resident((1, D_MODEL))],
        out_specs=row_spec(D_MODEL),
        out_shape=jax.ShapeDtypeStruct((rows, D_MODEL), _F32),
        scratch_shapes=[pltpu.VMEM(s, _BF16) for s in ((DIFF_WIDTH, D_MODEL), (MOBA_WIDTH, D_MODEL),
                                                      (D_MODEL, D_MODEL), (D_MODEL, D_MODEL),
                                                      (PLE_DIM, D_MODEL))],
        compiler_params=_params(1),
        name="out_proj",
    )(o_a, o_b, gates, x2d, p[0].reshape(rows, PLE_DIM), w_branch_diff[0], w_branch_moba[0],
      w_out[0], w_ple_gate[0], w_ple[0], final_g.reshape(1, D_MODEL))
    return out.reshape(B, S, D_MODEL)
```

```python
import collections
import functools
import math

import jax
import jax.numpy as jnp
import numpy as np
from jax import lax
from jax.experimental import pallas as pl
from jax.experimental.pallas import tpu as pltpu

D_MODEL = 1024
PLE_DIM = 256
DIFF_HEADS = 4
DIFF_HEAD_DIM = 64
DIFF_WIDTH = DIFF_HEADS * 2 * DIFF_HEAD_DIM
MOBA_HEADS = 8
MOBA_HEAD_DIM = 64
MOBA_WIDTH = MOBA_HEADS * MOBA_HEAD_DIM
MOBA_BLOCK = 256
MOBA_TOPK = 3
ROT_DIM = 16
ROPE_THETA = 500000.0
EPS = 1e-6
SUBLN_EPS = 1e-5
LAM_INIT = 0.8 - 0.6 * math.exp(-0.3 * 0)

LANES = 128
MXU_COLS = 256
BF16_ROWS = 16
CHUNK = 512
N_CHUNKS = 12
ROW_TILE = 512
ATT_TILE = 512
PIPE_UNROLL = 24
DIAG_UNROLL = 14
VMEM_LIMIT = 56 * 1024 * 1024
_SCORE_SCALE = DIFF_HEAD_DIM ** -0.5 * math.log2(math.e)
DIFF_V_ROWS = 2 * DIFF_HEAD_DIM + BF16_ROWS
MOBA_V_ROWS = MOBA_HEAD_DIM + BF16_ROWS
assert DIFF_HEAD_DIM == MOBA_HEAD_DIM and ATT_TILE % MOBA_BLOCK == 0 and ROW_TILE == ATT_TILE

_F32 = jnp.float32
_BF16 = jnp.bfloat16
_NT = (((1,), (1,)), ((), ()))


def _rope_rows(a, cosv, sin_lo, sin_hi):
    outs = []
    for j in range(CHUNK // LANES):
        xs = a[:, LANES * j:LANES * (j + 1)]
        up = pltpu.roll(xs, LANES - ROT_DIM // 2, 1)
        dn = pltpu.roll(xs, ROT_DIM // 2, 1)
        outs.append(xs * cosv + up * sin_lo + dn * sin_hi)
    return jnp.concatenate(outs, axis=1)


def _proj_kernel(x_ref, g_ref, w_ref, cos_ref, slo_ref, shi_ref,
                 dq_ref, dk_ref, dvt_ref, dg_ref, mq_ref, mk_ref, mvt_ref, mg_ref,
                 km_ref, gates_ref):
    for j in range(ROW_TILE // MOBA_BLOCK):
        r = slice(MOBA_BLOCK * j, MOBA_BLOCK * (j + 1))
        x = x_ref[r]
        ms = jnp.mean(x * x, axis=-1, keepdims=True)
        h = (x * lax.rsqrt(ms + EPS) * g_ref[...]).astype(_BF16)
        rope = functools.partial(_rope_rows, cosv=cos_ref[r], sin_lo=slo_ref[r], sin_hi=shi_ref[r])

        def proj(c):
            return jnp.dot(h, w_ref[:, CHUNK * c:CHUNK * (c + 1)].astype(_BF16),
                           preferred_element_type=_F32)

        def store_transposed(ref, a, head_rows):
            at = a.T.astype(_BF16)
            ones = jnp.ones((BF16_ROWS, MOBA_BLOCK), _BF16)
            stride = head_rows + BF16_ROWS
            for g in range(CHUNK // head_rows):
                ref[0, 0, stride * g:stride * g + head_rows, r] = at[head_rows * g:head_rows * (g + 1)]
                ref[0, 0, stride * g + head_rows:stride * (g + 1), r] = ones

        dq_ref[r] = (rope(proj(0)) * _SCORE_SCALE).astype(_BF16)
        dk_ref[r] = rope(proj(1)).astype(_BF16)
        store_transposed(dvt_ref, proj(2), 2 * DIFF_HEAD_DIM)
        dg_ref[r] = jax.nn.silu(proj(3))
        mq_ref[r] = rope(proj(4))
        mk = rope(proj(5))
        mk_ref[r] = mk.astype(_BF16)
        km_ref[0, j:j + 1, :] = jnp.mean(mk, axis=0, keepdims=True)
        store_transposed(mvt_ref, proj(6), MOBA_HEAD_DIM)
        mg_ref[r] = jax.nn.silu(proj(7))
        for c in range(8, N_CHUNKS):
            gates_ref[r, CHUNK * (c - 8):CHUNK * (c - 7)] = jax.nn.sigmoid(proj(c))


def _split_heads(q):
    lane = lax.broadcasted_iota(jnp.int32, q.shape, 1)
    zero = jnp.zeros_like(q)
    return jnp.where(lane < 64, q, zero), jnp.where(lane >= 64, q, zero)


def _tile_rows(i):
    return pl.ds(pl.multiple_of(i * ATT_TILE, ATT_TILE), ATT_TILE)


def _past_items(n_q):
    items = [(qi, t) for qi in range(1, n_q) for t in range(qi)]
    assert len(items) % PIPE_UNROLL == 0 and PIPE_UNROLL % 2 == 0
    items.append(items[-1])
    return (np.array([i[0] for i in items], np.int32), np.array([i[1] for i in items], np.int32))


_Stream = collections.namedtuple("_Stream", "s_ref acc_ref m_ref")


def _run_attention(qtab_ref, ttab_ref, streams, n_seg, k_ref, cm_ref, q_of, vt_of, bias_of,
                   interleave):
    n_q = streams[0].acc_ref.shape[0]
    seg_rows = ATT_TILE // n_seg
    assert interleave in ("part", "stream") and seg_rows % MXU_COLS == 0

    def softmax(st, qi, tile, slot, first):
        stream = streams[st]
        biases = [bias_of(st, qi, tile, j) for j in range(n_seg)]
        cands = [cm_ref[slot, st * n_seg + j] if b is None else cm_ref[slot, st * n_seg + j] + b
                 for j, b in enumerate(biases)]
        m_old = None if first else stream.m_ref[qi]
        m_new = functools.reduce(jnp.maximum, cands if first else [m_old] + cands)
        stream.m_ref[qi] = m_new
        p = [jnp.exp2(stream.s_ref[slot, seg_rows * j:seg_rows * (j + 1), :]
                      - (m_new if b is None else m_new - b)).astype(_BF16)
             for j, b in enumerate(biases)]
        p = p[0] if n_seg == 1 else jnp.concatenate(p, axis=0)
        return p, (None if first else jnp.exp2(m_old - m_new))

    def pieces(diagonal):
        if diagonal:
            return [(c, MXU_COLS, c + MXU_COLS) for c in range(0, ATT_TILE, MXU_COLS)]
        width = MXU_COLS if interleave == "part" else ATT_TILE
        return [(c, width, ATT_TILE) for c in range(0, ATT_TILE, width)]

    def qk_piece(st, k, qi, slot, diagonal, c0, width, keys):
        cols = slice(c0, c0 + width)
        rows = pl.ds(pl.multiple_of(qi * ATT_TILE + c0, width), width)
        s = lax.dot_general(k[:keys], q_of(st, rows), _NT, preferred_element_type=_F32)
        if diagonal:
            kpos = lax.broadcasted_iota(jnp.int32, s.shape, 0)
            qpos = lax.broadcasted_iota(jnp.int32, s.shape, 1) + c0
            s = jnp.where(kpos <= qpos, s, -jnp.inf)
        s_ref = streams[st].s_ref
        s_ref[slot, :keys, cols] = s
        if keys < ATT_TILE:
            s_ref[slot, keys:, cols] = jnp.full((ATT_TILE - keys, width), -jnp.inf, _F32)
        for j in range(n_seg):
            seg = s[seg_rows * j:seg_rows * (j + 1)]
            cm_ref[slot, st * n_seg + j, :, cols] = (
                jnp.max(seg, axis=0, keepdims=True) if seg.shape[0]
                else jnp.full((1, width), -jnp.inf, _F32))

    def pv_piece(st, vt, p, alpha, qi, c0, width, keys):
        cols = slice(c0, c0 + width)
        acc_ref = streams[st].acc_ref
        pv = jnp.dot(vt[:, :keys], p[:keys, cols], preferred_element_type=_F32)
        acc_ref[qi, :, cols] = pv if alpha is None else alpha[:, cols] * acc_ref[qi, :, cols] + pv

    def step(nxt, cur, first=False):
        k = None if nxt is None else k_ref[_tile_rows(nxt[1]), :]
        for st in range(len(streams)):
            qk_ops, pv_ops = [], []
            if nxt is not None:
                qk_ops = [functools.partial(qk_piece, st, k, nxt[0], nxt[2], nxt[3], *pc)
                          for pc in pieces(nxt[3])]
            if cur is not None:
                p, alpha = softmax(st, cur[0], cur[1], cur[2], first)
                vt = vt_of(st, cur[1])
                pv_ops = [functools.partial(pv_piece, st, vt, p, alpha, cur[0], *pc)
                          for pc in pieces(first)]
            if interleave == "part":
                ops = [op for pair in zip(qk_ops, pv_ops) for op in pair]
                ops += qk_ops[len(pv_ops):] + pv_ops[len(qk_ops):]
            else:
                ops = qk_ops + pv_ops
            for op in ops:
                op()

    step((0, 0, 0, True), None)

    assert (n_q - 2) % DIAG_UNROLL == 0

    def diag_items(j, _):
        for u in range(DIAG_UNROLL):
            t = DIAG_UNROLL * j + u
            step((t + 1, t + 1, (u + 1) % 2, True), (t, t, u % 2), first=True)
        return 0

    lax.fori_loop(0, (n_q - 2) // DIAG_UNROLL, diag_items, 0)
    step((n_q - 1, n_q - 1, 1, True), (n_q - 2, n_q - 2, 0), first=True)
    step((qtab_ref[0], ttab_ref[0], 0, False), (n_q - 1, n_q - 1, 1), first=True)

    def past_items(j, _):
        for u in range(PIPE_UNROLL):
            t = PIPE_UNROLL * j + u
            step((qtab_ref[t + 1], ttab_ref[t + 1], (u + 1) % 2, False),
                 (qtab_ref[t], ttab_ref[t], u % 2))
        return 0

    lax.fori_loop(0, n_q * (n_q - 1) // (2 * PIPE_UNROLL), past_items, 0)


def _diff_kernel(qtab_ref, ttab_ref, lq1_ref, lk1_ref, lq2_ref, lk2_ref, q_ref, k_ref, vt_ref,
                 dg_ref, sg_ref, o_ref, acc1_ref, acc2_ref, m1_ref, m2_ref, s1_ref, s2_ref, cm_ref):
    n_q = acc1_ref.shape[0]
    streams = (_Stream(s1_ref, acc1_ref, m1_ref), _Stream(s2_ref, acc2_ref, m2_ref))
    _run_attention(
        qtab_ref, ttab_ref, streams, 1, k_ref, cm_ref,
        q_of=lambda st, rows: _split_heads(q_ref[rows, :])[st],
        vt_of=lambda st, tile: vt_ref[0, tile],
        bias_of=lambda st, qi, tile, j: None,
        interleave="stream")

    dv = 2 * DIFF_HEAD_DIM
    lam = (jnp.exp(jnp.sum(lq1_ref[...] * lk1_ref[...], axis=-1, keepdims=True))
           - jnp.exp(jnp.sum(lq2_ref[...] * lk2_ref[...], axis=-1, keepdims=True)) + LAM_INIT)

    def finalize(qi, _):
        ot = (acc1_ref[qi, :dv] / acc1_ref[qi, dv:dv + 1]
              - lam * (acc2_ref[qi, :dv] / acc2_ref[qi, dv:dv + 1]))
        ms = jnp.mean(ot * ot, axis=0, keepdims=True)
        ot = ot * lax.rsqrt(ms + SUBLN_EPS) * sg_ref[...] * (1.0 - LAM_INIT)
        o_ref[_tile_rows(qi), :] = (ot.T * dg_ref[_tile_rows(qi), :]).astype(_BF16)
        return 0

    lax.fori_loop(0, n_q // 2, lambda j, c: finalize(2 * j + 1, finalize(2 * j, c)), 0)


def _moba_kernel(qtab_ref, ttab_ref, q_ref, k_ref, vt_ref, km_ref, mg_ref, o_ref,
                 acca_ref, accb_ref, ma_ref, mb_ref, sa_ref, sb_ref, cm_ref, bias_ref, qs_ref):
    n_q = acca_ref.shape[0]
    nb = km_ref.shape[0]
    per_tile = ATT_TILE // MOBA_BLOCK
    km_both = jnp.concatenate(_split_heads(km_ref[...]), axis=0)
    km_hi = km_both.astype(_BF16)
    km_hl = jnp.concatenate([km_hi, (km_both - km_hi.astype(_F32)).astype(_BF16)], axis=0)
    blk = lax.broadcasted_iota(jnp.int32, (nb, ATT_TILE), 0)
    col = lax.broadcasted_iota(jnp.int32, (nb, ATT_TILE), 1)
    col_blk = lax.shift_right_logical(col, MOBA_BLOCK.bit_length() - 1)

    def gate(qi, _):
        q = q_ref[_tile_rows(qi), :]
        own = per_tile * qi + col_blk
        q_hi = q.astype(_BF16)
        q_lo = (q - q_hi.astype(_F32)).astype(_BF16)
        g_hl = lax.dot_general(km_hl, q_hi, _NT, preferred_element_type=_F32)
        g_both = (g_hl[:2 * nb] + g_hl[2 * nb:]
                  + lax.dot_general(km_hi, q_lo, _NT, preferred_element_type=_F32))
        for hh in range(2):
            g = jnp.where(blk < own, g_both[nb * hh:nb * (hh + 1)], -jnp.inf)
            sel = jnp.zeros(g.shape, jnp.bool_)
            for _ in range(MOBA_TOPK):
                mx = jnp.max(g, axis=0, keepdims=True)
                first_idx = jnp.min(jnp.where(g == mx, blk, nb), axis=0, keepdims=True)
                pick = (blk == first_idx) & (mx > -jnp.inf)
                sel = sel | pick
                g = jnp.where(pick, -jnp.inf, g)
            bias_ref[hh, qi] = jnp.where(sel | (blk == own), 0.0, -jnp.inf)
        for hh, qh in enumerate(_split_heads((q * _SCORE_SCALE).astype(_BF16))):
            qs_ref[hh, _tile_rows(qi), :] = qh
        return 0

    lax.fori_loop(0, n_q // 4,
                  lambda j, c: gate(4 * j + 3, gate(4 * j + 2, gate(4 * j + 1, gate(4 * j, c)))), 0)

    streams = (_Stream(sa_ref, acca_ref, ma_ref), _Stream(sb_ref, accb_ref, mb_ref))
    _run_attention(
        qtab_ref, ttab_ref, streams, per_tile, k_ref, cm_ref,
        q_of=lambda st, rows: qs_ref[st, rows, :],
        vt_of=lambda st, tile: vt_ref[0, tile, MOBA_V_ROWS * st:MOBA_V_ROWS * (st + 1), :],
        bias_of=lambda st, qi, tile, j: bias_ref[st, qi, pl.ds(per_tile * tile + j, 1), :],
        interleave="part")

    dh = MOBA_HEAD_DIM

    def finalize(qi, _):
        ot = jnp.concatenate([acca_ref[qi, :dh] / acca_ref[qi, dh:dh + 1],
                              accb_ref[qi, :dh] / accb_ref[qi, dh:dh + 1]], axis=0)
        o_ref[_tile_rows(qi), :] = (ot.T * mg_ref[_tile_rows(qi), :]).astype(_BF16)
        return 0

    lax.fori_loop(0, n_q // 2, lambda j, c: finalize(2 * j + 1, finalize(2 * j, c)), 0)


def _out_kernel(oa_ref, ob_ref, gates_ref, x_ref, p_ref, wbd_ref, wbm_ref, wout_ref, wpg_ref,
                wple_ref, fg_ref, o_ref, wbd_bf, wbm_bf, wout_bf, wpg_bf, wple_bf):
    @pl.when(pl.program_id(0) == 0)
    def _():
        for src, dst in ((wbd_ref, wbd_bf), (wbm_ref, wbm_bf), (wout_ref, wout_bf),
                         (wpg_ref, wpg_bf), (wple_ref, wple_bf)):
            dst[...] = src[...].astype(_BF16)

    half = ROW_TILE // 2
    for r in (slice(0, half), slice(half, ROW_TILE)):
        ya = jnp.dot(oa_ref[r], wbd_bf[...], preferred_element_type=_F32)
        yb = jnp.dot(ob_ref[r], wbm_bf[...], preferred_element_type=_F32)
        merged = gates_ref[r, :D_MODEL] * ya + gates_ref[r, D_MODEL:] * yb
        x1 = x_ref[r] + jnp.dot(merged.astype(_BF16), wout_bf[...], preferred_element_type=_F32)
        t = jnp.dot(x1.astype(_BF16), wpg_bf[...], preferred_element_type=_F32)
        pe = jnp.dot(p_ref[r].astype(_BF16), wple_bf[...], preferred_element_type=_F32)
        x2 = x1 + jax.nn.sigmoid(t) * pe
        ms = jnp.mean(x2 * x2, axis=-1, keepdims=True)
        o_ref[r] = x2 * lax.rsqrt(ms + EPS) * fg_ref[...]


def _rope_lane_tables(seq):
    half = ROT_DIM // 2
    inv = ROPE_THETA ** (-np.arange(0, ROT_DIM, 2, dtype=np.float64) / ROT_DIM)
    ang = np.arange(seq, dtype=np.float64)[:, None] * inv[None, :]
    cos, sin = np.cos(ang), np.sin(ang)
    pad = MOBA_HEAD_DIM - ROT_DIM
    one, zero = np.ones((seq, pad)), np.zeros((seq, pad))
    zh = np.zeros((seq, half))
    cos64 = np.concatenate([cos, cos, one], axis=1)
    slo64 = np.concatenate([-sin, zh, zero], axis=1)
    shi64 = np.concatenate([zh, sin, zero], axis=1)
    return tuple(jnp.asarray(np.tile(t, (1, LANES // MOBA_HEAD_DIM)), _F32)
                 for t in (cos64, slo64, shi64))


def _resident(shape):
    return pl.BlockSpec(shape, lambda *_: (0,) * len(shape), pipeline_mode=pl.Buffered(1))


def _params(n_axes):
    return pltpu.CompilerParams(dimension_semantics=("arbitrary",) * n_axes,
                                vmem_limit_bytes=VMEM_LIMIT)


def kernel(x, p, norm_g, w_in, lambda_q1, lambda_k1, lambda_q2, lambda_k2, subln_g,
           w_branch_diff, w_branch_moba, w_out, w_ple, w_ple_gate, final_g):
    B, S, _ = x.shape
    assert w_in.shape == (1, D_MODEL, N_CHUNKS * CHUNK) and S % (2 * ATT_TILE) == 0
    rows = B * S
    n_row_tiles = rows // ROW_TILE
    n_kv = S // ATT_TILE
    n_blk = S // MOBA_BLOCK
    x2d = x.reshape(rows, D_MODEL)
    cosv, sin_lo, sin_hi = _rope_lane_tables(S)

    row_spec = lambda w: pl.BlockSpec((ROW_TILE, w), lambda i: (i, 0))
    tab_spec = pl.BlockSpec((ROW_TILE, LANES), lambda i: (i % n_kv, 0))
    dvt_rows, mvt_rows = DIFF_HEADS * DIFF_V_ROWS, MOBA_HEADS * MOBA_V_ROWS
    vt_spec = lambda r: pl.BlockSpec((1, 1, r, ATT_TILE), lambda i: (i // n_kv, i % n_kv, 0, 0))
    vt_shape = lambda r: jax.ShapeDtypeStruct((B, n_kv, r, ATT_TILE), _BF16)
    rows_bf16 = jax.ShapeDtypeStruct((rows, CHUNK), _BF16)
    rows_f32 = jax.ShapeDtypeStruct((rows, CHUNK), _F32)
    km_rows = ROW_TILE // MOBA_BLOCK
    (dq, dk, dvt, dgs, mq, mk, mvt, mgs, kmean, gates) = pl.pallas_call(
        _proj_kernel,
        grid=(n_row_tiles,),
        in_specs=[row_spec(D_MODEL), _resident((1, D_MODEL)), _resident((D_MODEL, N_CHUNKS * CHUNK)),
                  tab_spec, tab_spec, tab_spec],
        out_specs=[row_spec(CHUNK), row_spec(CHUNK), vt_spec(dvt_rows), row_spec(CHUNK), row_spec(CHUNK),
                   row_spec(CHUNK), vt_spec(mvt_rows), row_spec(CHUNK),
                   pl.BlockSpec((1, km_rows, CHUNK), lambda i: (i, 0, 0)),
                   row_spec(4 * CHUNK)],
        out_shape=[rows_bf16, rows_bf16, vt_shape(dvt_rows), rows_f32, rows_f32, rows_bf16,
                   vt_shape(mvt_rows), rows_f32,
                   jax.ShapeDtypeStruct((n_row_tiles, km_rows, CHUNK), _F32),
                   jax.ShapeDtypeStruct((rows, 4 * CHUNK), _F32)],
        compiler_params=_params(1),
        name="proj",
    )(x2d, norm_g[0].reshape(1, D_MODEL), w_in[0], cosv, sin_lo, sin_hi)
    kmean = kmean.reshape(B * n_blk, CHUNK)

    n_groups = CHUNK // LANES
    qtab, ttab = _past_items(n_kv)
    seq_spec = pl.BlockSpec((S, LANES), lambda b, g, *_: (b, g))
    v_spec = lambda r: pl.BlockSpec((1, n_kv, r, ATT_TILE), lambda b, g, *_: (b, 0, g, 0))
    lam_spec = pl.BlockSpec((1, DIFF_HEAD_DIM), lambda b, g, *_: (0, 0))
    acc = lambda r: pltpu.VMEM((n_kv, r, ATT_TILE), _F32)
    run_max = pltpu.VMEM((n_kv, 1, ATT_TILE), _F32)
    score_buf = pltpu.VMEM((2, ATT_TILE, ATT_TILE), _F32)
    col_max = lambda n: pltpu.VMEM((2, n, 1, ATT_TILE), _F32)

    o_a = pl.pallas_call(
        _diff_kernel,
        grid_spec=pltpu.PrefetchScalarGridSpec(
            num_scalar_prefetch=2,
            grid=(B, n_groups),
            in_specs=[lam_spec, lam_spec, lam_spec, lam_spec, seq_spec, seq_spec, v_spec(DIFF_V_ROWS),
                      seq_spec, pl.BlockSpec((2 * DIFF_HEAD_DIM, 1), lambda b, g, *_: (0, 0))],
            out_specs=seq_spec,
            scratch_shapes=[acc(DIFF_V_ROWS), acc(DIFF_V_ROWS), run_max, run_max, score_buf, score_buf,
                            col_max(2)],
        ),
        out_shape=rows_bf16,
        compiler_params=_params(2),
        name="diff_attn",
    )(qtab, ttab, lambda_q1[0:1], lambda_k1[0:1], lambda_q2[0:1], lambda_k2[0:1], dq, dk, dvt, dgs,
      subln_g[0].reshape(2 * DIFF_HEAD_DIM, 1))

    o_b = pl.pallas_call(
        _moba_kernel,
        grid_spec=pltpu.PrefetchScalarGridSpec(
            num_scalar_prefetch=2,
            grid=(B, n_groups),
            in_specs=[seq_spec, seq_spec, v_spec(2 * MOBA_V_ROWS),
                      pl.BlockSpec((n_blk, LANES), lambda b, g, *_: (b, g)), seq_spec],
            out_specs=seq_spec,
            scratch_shapes=[acc(MOBA_V_ROWS), acc(MOBA_V_ROWS), run_max, run_max, score_buf, score_buf,
                            col_max(2 * (ATT_TILE // MOBA_BLOCK)),
                            pltpu.VMEM((2, n_kv, n_blk, ATT_TILE), _F32),
                            pltpu.VMEM((2, S, LANES), _BF16)],
        ),
        out_shape=rows_bf16,
        compiler_params=_params(2),
        name="moba_attn",
    )(qtab, ttab, mq, mk, mvt, kmean, mgs)

    out = pl.pallas_call(
        _out_kernel,
        grid=(n_row_tiles,),
        in_specs=[row_spec(CHUNK), row_spec(CHUNK), row_spec(4 * CHUNK), row_spec(D_MODEL),
                  row_spec(PLE_DIM), _resident((DIFF_WIDTH, D_MODEL)), _resident((MOBA_WIDTH, D_MODEL)),
                  _resident((D_MODEL, D_MODEL)), _resident((D_MODEL, D_MODEL)),
                  _resident((PLE_DIM, D_MODEL)), _resident((1, D_MODEL))],
        out_specs=row_spec(D_MODEL),
        out_shape=jax.ShapeDtypeStruct((rows, D_MODEL), _F32),
        scratch_shapes=[pltpu.VMEM(s, _BF16) for s in ((DIFF_WIDTH, D_MODEL), (MOBA_WIDTH, D_MODEL),
                                                      (D_MODEL, D_MODEL), (D_MODEL, D_MODEL),
                                                      (PLE_DIM, D_MODEL))],
        compiler_params=_params(1),
        name="out_proj",
    )(o_a, o_b, gates, x2d, p[0].reshape(rows, PLE_DIM), w_branch_diff[0], w_branch_moba[0],
      w_out[0], w_ple_gate[0], w_ple[0], final_g.reshape(1, D_MODEL))
    return out.reshape(B, S, D_MODEL)
```
